```python
import jax, jax.numpy as jnp
from jax import lax
import numpy as np

D_MODEL = 2048
BATCH = 1
SEQ = 8192
DEPTH = 1

HG_HEADS = 8
HG_HEAD_DIM = 128
HG_WIDTH = HG_HEADS * HG_HEAD_DIM
HG_CHUNK = 64
ATT_HEADS = 16
ATT_KV_HEADS = 4
ATT_HEAD_DIM = 64
ATT_GROUP = ATT_HEADS // ATT_KV_HEADS
ATT_WIDTH = ATT_HEADS * ATT_HEAD_DIM
KV_WIDTH = ATT_KV_HEADS * ATT_HEAD_DIM
WINDOW = 128
ATT_BLOCK = 128
ROPE_THETA = 10000.0
D_FF = 4 * D_MODEL
ALPHA = (2 * DEPTH) ** 0.25
BETA = (8 * DEPTH) ** -0.25
LN_EPS = 1e-5
RMS_EPS = 1e-6

SPLIT_SIZES = (HG_WIDTH, HG_WIDTH, HG_WIDTH, HG_WIDTH,
               ATT_WIDTH, KV_WIDTH, KV_WIDTH,
               D_MODEL, D_MODEL)
D_IN = sum(SPLIT_SIZES)

kernel_name = "hgrn2_swa_sink_gated_hybrid"


def _split_points():
    pts, acc = [], 0
    for s in SPLIT_SIZES[:-1]:
        acc += s
        pts.append(acc)
    return pts


def layer_norm(x, gain, bias):
    xf = x.astype(jnp.float32)
    mu = jnp.mean(xf, axis=-1, keepdims=True)
    var = jnp.mean(jnp.square(xf - mu), axis=-1, keepdims=True)
    y = (xf - mu) * lax.rsqrt(var + LN_EPS) * gain.astype(jnp.float32) + bias.astype(jnp.float32)
    return y.astype(x.dtype)


def hgrn2_mixer(q, f_logit, i, g, lb, norm_gain):
    B, S, _ = q.shape
    f32 = jnp.float32
    lb = lb.astype(f32)
    f = lb + (1.0 - lb) * jax.nn.sigmoid(f_logit.astype(f32))
    log_f = jnp.log(f)
    k = 1.0 - f
    qf = jax.nn.silu(q.astype(f32))
    nc = S // HG_CHUNK

    def to_chunks(t):
        return t.reshape(B, nc, HG_CHUNK, HG_HEADS, HG_HEAD_DIM).transpose(1, 0, 3, 2, 4)

    qc, kc, vc, lc = to_chunks(qf), to_chunks(k), to_chunks(i.astype(f32)), to_chunks(log_f)
    causal = jnp.tril(jnp.ones((HG_CHUNK, HG_CHUNK), dtype=bool))[:, :, None]

    def step(state, inp):
        q_, k_, v_, l_ = inp
        b = jnp.cumsum(l_, axis=2)
        o_inter = jnp.einsum('bhtk,bhkv->bhtv', q_ * jnp.exp(b), state)
        diff = b[:, :, :, None, :] - b[:, :, None, :, :]
        decay = jnp.exp(jnp.where(causal, diff, -jnp.inf))
        scores = jnp.einsum('bhtk,bhtsk,bhsk->bhts', q_, decay, k_)
        o = o_inter + jnp.einsum('bhts,bhsv->bhtv', scores, v_)
        b_last = b[:, :, -1:, :]
        k_dec = k_ * jnp.exp(b_last - b)
        state = jnp.exp(b_last[:, :, 0, :])[..., None] * state + jnp.einsum('bhsk,bhsv->bhkv', k_dec, v_)
        return state, o

    state0 = jnp.zeros((B, HG_HEADS, HG_HEAD_DIM, HG_HEAD_DIM), f32)
    _, oc = lax.scan(step, state0, (qc, kc, vc, lc))
    o = oc.transpose(1, 0, 3, 2, 4).reshape(B, S, HG_HEADS, HG_HEAD_DIM)
    o = o * lax.rsqrt(jnp.mean(jnp.square(o), axis=-1, keepdims=True) + RMS_EPS)
    o = o.reshape(B, S, HG_WIDTH) * norm_gain.astype(f32)
    o = o * jax.nn.silu(g.astype(f32))
    return o.astype(q.dtype)


def rope(t, pos):
    half = t.shape[-1] // 2
    inv = ROPE_THETA ** (-jnp.arange(half, dtype=jnp.float32) / half)
    ang = pos.astype(jnp.float32)[:, None] * inv[None, :]
    cos = jnp.cos(ang)[None, :, None, :]
    sin = jnp.sin(ang)[None, :, None, :]
    tf = t.astype(jnp.float32)
    t1, t2 = tf[..., :half], tf[..., half:]
    return jnp.concatenate([t1 * cos - t2 * sin, t2 * cos + t1 * sin], axis=-1).astype(t.dtype)


def swa_sink_attention(q, k, v, sinks):
    B, S = q.shape[:2]
    f32 = jnp.float32
    nb = S // ATT_BLOCK
    qb = q.reshape(B, nb, ATT_BLOCK, ATT_KV_HEADS, ATT_GROUP, ATT_HEAD_DIM).astype(f32)

    def with_prev(t):
        tb = t.reshape(B, nb, ATT_BLOCK, ATT_KV_HEADS, ATT_HEAD_DIM).astype(f32)
        prev = jnp.pad(tb, ((0, 0), (1, 0), (0, 0), (0, 0), (0, 0)))[:, :-1]
        return jnp.concatenate([prev, tb], axis=2)

    kb, vb = with_prev(k), with_prev(v)
    scale = ATT_HEAD_DIM ** -0.5
    scores = jnp.einsum('bnqkgd,bnskd->bnkgqs', qb, kb) * scale
    qi = jnp.arange(ATT_BLOCK)[:, None] + ATT_BLOCK
    si = jnp.arange(2 * ATT_BLOCK)[None, :]
    rel = qi - si
    band = (rel >= 0) & (rel < WINDOW)
    key_pos = jnp.arange(nb)[:, None] * ATT_BLOCK - ATT_BLOCK + si
    valid = band[None] & (key_pos >= 0)[:, None, :]
    scores = jnp.where(valid[None, :, None, None], scores, -jnp.inf)
    sink = sinks.astype(f32).reshape(ATT_KV_HEADS, ATT_GROUP)[None, None, :, :, None, None]
    m = jnp.maximum(jnp.max(scores, axis=-1, keepdims=True), sink)
    p = jnp.exp(scores - m)
    probs = p / (jnp.sum(p, axis=-1, keepdims=True) + jnp.exp(sink - m))
    out = jnp.einsum('bnkgqs,bnskd->bnqkgd', probs, vb)
    return out.reshape(B, S, ATT_WIDTH).astype(q.dtype)


def setup_inputs(seed: int = 0) -> dict:
    key = jax.random.key(seed)
    ks = jax.random.split(key, 16)
    f32 = jnp.float32
    x = jax.random.normal(ks[0], (BATCH, SEQ, D_MODEL), f32)
    w_in = jax.random.normal(ks[1], (DEPTH, D_MODEL, D_IN), f32) * D_MODEL ** -0.5
    hg_lb_logits = jax.random.normal(ks[2], (DEPTH + 1, HG_WIDTH), f32) * 0.5
    hg_norm_gain = 1.0 + 0.02 * jax.random.normal(ks[3], (DEPTH, HG_WIDTH), f32)
    attn_sinks = jax.random.normal(ks[4], (DEPTH, ATT_HEADS), f32)
    w_branch_a = jax.random.normal(ks[5], (DEPTH, HG_WIDTH, D_MODEL), f32) * (HG_WIDTH ** -0.5) * BETA
    w_branch_b = jax.random.normal(ks[6], (DEPTH, ATT_WIDTH, D_MODEL), f32) * (ATT_WIDTH ** -0.5) * BETA
    w_out = jax.random.normal(ks[7], (DEPTH, D_MODEL, D_MODEL), f32) * (D_MODEL ** -0.5) * BETA
    ln1_gain = 1.0 + 0.02 * jax.random.normal(ks[8], (DEPTH, D_MODEL), f32)
    ln1_bias = 0.02 * jax.random.normal(ks[9], (DEPTH, D_MODEL), f32)
    w_ff1 = jax.random.normal(ks[10], (DEPTH, D_MODEL, D_FF), f32) * (D_MODEL ** -0.5) * BETA
    w_ff2 = jax.random.normal(ks[11], (DEPTH, D_FF, D_MODEL), f32) * (D_FF ** -0.5) * BETA
    ln2_gain = 1.0 + 0.02 * jax.random.normal(ks[12], (DEPTH, D_MODEL), f32)
    ln2_bias = 0.02 * jax.random.normal(ks[13], (DEPTH, D_MODEL), f32)
    return {"x": x, "w_in": w_in, "hg_lb_logits": hg_lb_logits, "hg_norm_gain": hg_norm_gain,
            "attn_sinks": attn_sinks, "w_branch_a": w_branch_a, "w_branch_b": w_branch_b,
            "w_out": w_out, "ln1_gain": ln1_gain, "ln1_bias": ln1_bias, "w_ff1": w_ff1,
            "w_ff2": w_ff2, "ln2_gain": ln2_gain, "ln2_bias": ln2_bias}


def reference(x, w_in, hg_lb_logits, hg_norm_gain, attn_sinks, w_branch_a, w_branch_b,
              w_out, ln1_gain, ln1_bias, w_ff1, w_ff2, ln2_gain, ln2_bias):
    B, S, _ = x.shape
    pos = jnp.arange(S)
    lb_all = jnp.cumsum(jax.nn.softmax(hg_lb_logits.astype(jnp.float32), axis=0), axis=0)
    splits = _split_points()
    for l in range(DEPTH):
        proj = x @ w_in[l]
        hq, hf, hi, hg, aq, ak, av, ga, gb = jnp.split(proj, splits, axis=-1)
        y_a = hgrn2_mixer(hq, hf, hi, hg, lb_all[l], hg_norm_gain[l]) @ w_branch_a[l]
        q = rope(aq.reshape(B, S, ATT_HEADS, ATT_HEAD_DIM), pos)
        k = rope(ak.reshape(B, S, ATT_KV_HEADS, ATT_HEAD_DIM), pos)
        v = av.reshape(B, S, ATT_KV_HEADS, ATT_HEAD_DIM)
        y_b = swa_sink_attention(q, k, v, attn_sinks[l]) @ w_branch_b[l]
        mixed = jax.nn.sigmoid(ga) * y_a + jax.nn.sigmoid(gb) * y_b
        x = layer_norm(ALPHA * x + mixed @ w_out[l], ln1_gain[l], ln1_bias[l])
        h = jnp.square(jax.nn.relu(x @ w_ff1[l])) @ w_ff2[l]
        x = layer_norm(ALPHA * x + h, ln2_gain[l], ln2_bias[l])
    return x
```

```python
import functools

import jax
import jax.numpy as jnp
from jax import lax
from jax.experimental import pallas as pl
from jax.experimental.pallas import tpu as pltpu

F32 = jnp.float32
BF16 = jnp.bfloat16

D_MODEL = 2048
SEQ = 8192
HG_HEADS = 8
HG_DIM = 128
HG_WIDTH = HG_HEADS * HG_DIM
ATT_HEADS = 16
ATT_KV_HEADS = 4
ATT_DIM = 64
ATT_GROUP = ATT_HEADS // ATT_KV_HEADS
ATT_WIDTH = ATT_HEADS * ATT_DIM
KV_WIDTH = ATT_KV_HEADS * ATT_DIM
WINDOW = 128
ROPE_THETA = 10000.0
D_FF = 4 * D_MODEL
D_IN = 4 * HG_WIDTH + ATT_WIDTH + 2 * KV_WIDTH + 2 * D_MODEL
ALPHA = 2.0 ** 0.25
LN_EPS = 1e-5
RMS_EPS = 1e-6

COL_HQ, COL_HF, COL_HI, COL_HG = 0, HG_WIDTH, 2 * HG_WIDTH, 3 * HG_WIDTH
COL_AQ = 4 * HG_WIDTH
COL_AK = COL_AQ + ATT_WIDTH
COL_GA = COL_AK + 2 * KV_WIDTH
COL_GB = COL_GA + D_MODEL

HG_CHUNK = 64
HG_SUB = 16
NEG_BIG = -1e30

VMEM_LIMIT = 56 * 1024 * 1024


def _sigmoid(x):
    return 1.0 / (1.0 + jnp.exp(-x))


def _dot(a, b):
    return jnp.dot(a, b, preferred_element_type=F32)


def _dot_nt(a, b):
    return lax.dot_general(a, b, (((1,), (1,)), ((), ())), preferred_element_type=F32)


def _dot_tn(a, b):
    return lax.dot_general(a, b, (((0,), (0,)), ((), ())), preferred_element_type=F32)


PROJ_TN = 512
PROJ_SEG = (0, COL_AQ // PROJ_TN, COL_GA // PROJ_TN, D_IN // PROJ_TN)


def _proj_kernel(x_ref, w_ref, oh_ref, oa_ref, og_ref):
    j = pl.program_id(1)
    acc = _dot(x_ref[...], w_ref[...])

    @pl.when(j < PROJ_SEG[1])
    def _():
        oh_ref[...] = acc

    @pl.when((j >= PROJ_SEG[1]) & (j < PROJ_SEG[2]))
    def _():
        oa_ref[...] = acc

    @pl.when(j >= PROJ_SEG[2])
    def _():
        og_ref[...] = acc


def _proj(xb, wb):
    tm, tn = 512, PROJ_TN
    m, k = xb.shape
    n = wb.shape[1]
    widths = [(PROJ_SEG[s + 1] - PROJ_SEG[s]) * tn for s in range(3)]

    def seg_spec(s):
        lo, hi = PROJ_SEG[s], PROJ_SEG[s + 1]
        return pl.BlockSpec((tm, tn), lambda i, j: (i, jnp.clip(j - lo, 0, hi - lo - 1)))

    return pl.pallas_call(
        _proj_kernel,
        out_shape=tuple(jax.ShapeDtypeStruct((m, w), F32) for w in widths),
        grid=(m // tm, n // tn),
        in_specs=[pl.BlockSpec((tm, k), lambda i, j: (i, 0)),
                  pl.BlockSpec((k, tn), lambda i, j: (0, j))],
        out_specs=tuple(seg_spec(s) for s in range(3)),
        compiler_params=pltpu.CompilerParams(
            dimension_semantics=("parallel", "arbitrary"), vmem_limit_bytes=VMEM_LIMIT),
        name="proj",
    )(xb, wb)


def _hgrn_kernel(q_ref, f_ref, i_ref, g_ref, lbl_ref, gain_ref, o_ref, state_ref, *, rows):
    @pl.when(pl.program_id(1) == 0)
    def _():
        state_ref[...] = jnp.zeros_like(state_ref)

    lbl = lbl_ref[...]
    lmax = jnp.max(lbl, axis=0, keepdims=True)
    lexp = jnp.exp(lbl - lmax)
    lb = lexp[0:1, :] / jnp.sum(lexp, axis=0, keepdims=True)
    gain = gain_ref[...]

    C, SB = HG_CHUNK, HG_SUB
    tri = (lax.broadcasted_iota(jnp.int32, (C, C), 0)
           >= lax.broadcasted_iota(jnp.int32, (C, C), 1)).astype(BF16)
    ones_dd = jnp.ones((HG_DIM, HG_DIM), BF16)
    sub_row = lax.broadcasted_iota(jnp.int32, (SB, 1), 0)

    def chunk(c, carry):
        r0 = pl.multiple_of(c * C, C)
        hq = q_ref[pl.ds(r0, C), :]
        hf = f_ref[pl.ds(r0, C), :]
        v = i_ref[pl.ds(r0, C), :]
        hg = g_ref[pl.ds(r0, C), :]

        f = lb + (1.0 - lb) * _sigmoid(hf)
        lf = jnp.log(f)
        kk = 1.0 - f
        qf = hq * _sigmoid(hq)

        h1 = lf.astype(BF16)
        r1 = lf - h1.astype(F32)
        h2 = r1.astype(BF16)
        h3 = (r1 - h2.astype(F32)).astype(BF16)
        b = _dot(tri, h1) + _dot(tri, h2) + _dot(tri, h3)

        st = state_ref[...]
        vb = v.astype(BF16)
        o_inter = _dot_nt((qf * jnp.exp(b)).astype(BF16), st.astype(BF16))

        b_last = b[C - 1:C, :]
        kd = kk * jnp.exp(b_last - b)
        state_ref[...] = st * jnp.exp(b_last) + _dot_tn(vb, kd.astype(BF16))

        outs = []
        for i in range(C // SB):
            lo = i * SB
            bi = b[lo:lo + SB, :]
            qi = qf[lo:lo + SB, :]
            ki = kk[lo:lo + SB, :]
            vi = v[lo:lo + SB, :]
            ps = []
            for j in range(SB):
                d = jnp.where(sub_row >= j, bi - bi[j:j + 1, :], NEG_BIG)
                ps.append(qi * jnp.exp(d) * ki[j:j + 1, :])
            p = jnp.concatenate(ps, axis=0)
            r = _dot(p.astype(BF16), ones_dd)
            acc = o_inter[lo:lo + SB, :]
            for j in range(SB):
                acc = acc + r[j * SB:(j + 1) * SB, :] * vi[j:j + 1, :]
            if i > 0:
                ref = b[lo:lo + 1, :]
                qs = qi * jnp.exp(bi - ref)
                ks = kk[:lo, :] * jnp.exp(ref - b[:lo, :])
                s = _dot_nt(qs.astype(BF16), ks.astype(BF16))
                acc = acc + _dot(s.astype(BF16), vb[:lo, :])
            outs.append(acc)
        o = jnp.concatenate(outs, axis=0)

        ms = jnp.mean(o * o, axis=-1, keepdims=True)
        o = o * lax.rsqrt(ms + RMS_EPS) * gain
        o = o * (hg * _sigmoid(hg))
        o_ref[pl.ds(r0, C), :] = o.astype(o_ref.dtype)
        return carry

    lax.fori_loop(0, rows // C, chunk, 0)


def _hgrn(proj, lb_logits, gain):
    rows = 512
    s = proj.shape[0]
    nlayers = lb_logits.shape[0]

    def col(base):
        return pl.BlockSpec((rows, HG_DIM), lambda h, t, base=base: (t, base // HG_DIM + h))

    return pl.pallas_call(
        functools.partial(_hgrn_kernel, rows=rows),
        out_shape=jax.ShapeDtypeStruct((s, HG_WIDTH), BF16),
        grid=(HG_HEADS, s // rows),
        in_specs=[col(COL_HQ), col(COL_HF), col(COL_HI), col(COL_HG),
                  pl.BlockSpec((nlayers, HG_DIM), lambda h, t: (0, h)),
                  pl.BlockSpec((1, HG_DIM), lambda h, t: (0, h))],
        out_specs=pl.BlockSpec((rows, HG_DIM), lambda h, t: (t, h)),
        scratch_shapes=[pltpu.VMEM((HG_DIM, HG_DIM), F32)],
        compiler_params=pltpu.CompilerParams(
            dimension_semantics=("parallel", "arbitrary"), vmem_limit_bytes=VMEM_LIMIT),
        name="hgrn2",
    )(proj, proj, proj, proj, lb_logits, gain)


def _rope(t, cos, sin_signed):
    w = t.shape[1]
    reps = w // cos.shape[1]
    cos_t = jnp.concatenate([cos] * reps, axis=1) if reps > 1 else cos
    sin_t = jnp.concatenate([sin_signed] * reps, axis=1) if reps > 1 else sin_signed
    lane = lax.broadcasted_iota(jnp.int32, t.shape, 1)
    first_half = (lane % ATT_DIM) < (ATT_DIM // 2)
    rot = jnp.where(first_half, pltpu.roll(t, w - ATT_DIM // 2, 1), pltpu.roll(t, ATT_DIM // 2, 1))
    return t * cos_t + rot * sin_t


def _attn_kernel(q_ref, kvp_ref, kvc_ref, cosp_ref, sinp_ref, cosc_ref, sinc_ref, sink_ref, o_ref):
    n = pl.program_id(0)
    blk = q_ref.shape[0]
    q = _rope(q_ref[...], cosc_ref[...], sinc_ref[...])
    kvp = kvp_ref[...]
    kvc = kvc_ref[...]
    kp = _rope(kvp[:, :KV_WIDTH], cosp_ref[...], sinp_ref[...])
    kc = _rope(kvc[:, :KV_WIDTH], cosc_ref[...], sinc_ref[...])
    k_all = jnp.concatenate([kp, kc], axis=0).astype(BF16)
    v_all = jnp.concatenate([kvp[:, KV_WIDTH:], kvc[:, KV_WIDTH:]], axis=0).astype(BF16)

    qi = lax.broadcasted_iota(jnp.int32, (blk, 2 * blk), 0)
    ci = lax.broadcasted_iota(jnp.int32, (blk, 2 * blk), 1)
    rel = qi + blk - ci
    first_key = jnp.where(n > 0, 0, blk)
    bias = jnp.where(rel >= 0, jnp.where(rel < blk, jnp.where(ci >= first_key, 0.0, NEG_BIG),
                                         NEG_BIG), NEG_BIG)
    scale = ATT_DIM ** -0.5
    qb = q.astype(BF16)
    for g in range(ATT_KV_HEADS):
        kh = k_all[:, g * ATT_DIM:(g + 1) * ATT_DIM]
        vh = v_all[:, g * ATT_DIM:(g + 1) * ATT_DIM]
        for u in range(ATT_GROUP):
            h = g * ATT_GROUP + u
            qh = qb[:, h * ATT_DIM:(h + 1) * ATT_DIM]
            s = _dot_nt(qh, kh) * scale + bias
            sink = sink_ref[h:h + 1, 0:1]
            m = jnp.maximum(jnp.max(s, axis=-1, keepdims=True), sink)
            p = jnp.exp(s - m)
            denom = jnp.sum(p, axis=-1, keepdims=True) + jnp.exp(sink - m)
            out = _dot(p.astype(BF16), vh) / denom
            o_ref[:, h * ATT_DIM:(h + 1) * ATT_DIM] = out.astype(o_ref.dtype)


def _attn(proj, cos, sin_signed, sinks_b):
    blk = WINDOW
    s = proj.shape[0]
    kv_col = ATT_WIDTH // (2 * KV_WIDTH)
    prev = lambda n: (jnp.maximum(n - 1, 0), 0)
    cur = lambda n: (n, 0)
    return pl.pallas_call(
        _attn_kernel,
        out_shape=jax.ShapeDtypeStruct((s, ATT_WIDTH), BF16),
        grid=(s // blk,),
        in_specs=[pl.BlockSpec((blk, ATT_WIDTH), lambda n: (n, 0)),
                  pl.BlockSpec((blk, 2 * KV_WIDTH), lambda n: (jnp.maximum(n - 1, 0), kv_col)),
                  pl.BlockSpec((blk, 2 * KV_WIDTH), lambda n: (n, kv_col)),
                  pl.BlockSpec((blk, 128), prev), pl.BlockSpec((blk, 128), prev),
                  pl.BlockSpec((blk, 128), cur), pl.BlockSpec((blk, 128), cur),
                  pl.BlockSpec((ATT_HEADS, 128), lambda n: (0, 0))],
        out_specs=pl.BlockSpec((blk, ATT_WIDTH), lambda n: (n, 0)),
        compiler_params=pltpu.CompilerParams(
            dimension_semantics=("parallel",), vmem_limit_bytes=VMEM_LIMIT),
        name="swa",
    )(proj, proj, proj, cos, sin_signed, cos, sin_signed, sinks_b)


def _layer_norm(r, gain, bias):
    mu = jnp.mean(r, axis=-1, keepdims=True)
    d = r - mu
    var = jnp.mean(d * d, axis=-1, keepdims=True)
    return d * lax.rsqrt(var + LN_EPS) * gain + bias


def _merge_kernel(oa_ref, ob_ref, ga_ref, gb_ref, x_ref, wa_ref, wb_ref, wo_ref, g_ref, b_ref,
                  x1_ref, x1b_ref):
    ya = _dot(oa_ref[...], wa_ref[...])
    yb = _dot(ob_ref[...], wb_ref[...])
    mixed = _sigmoid(ga_ref[...]) * ya + _sigmoid(gb_ref[...]) * yb
    z = _dot(mixed.astype(BF16), wo_ref[...])
    x1 = _layer_norm(ALPHA * x_ref[...] + z, g_ref[...], b_ref[...])
    x1_ref[...] = x1
    x1b_ref[...] = x1.astype(BF16)


def _merge(oa, ob, proj, x, wa, wb, wo, g1, b1):
    tm = 256
    s = x.shape[0]
    row = lambda i: (i, 0)
    const = lambda i: (0, 0)
    resident = dict(pipeline_mode=pl.Buffered(1))
    return pl.pallas_call(
        _merge_kernel,
        out_shape=(jax.ShapeDtypeStruct((s, D_MODEL), F32), jax.ShapeDtypeStruct((s, D_MODEL), BF16)),
        grid=(s // tm,),
        in_specs=[pl.BlockSpec((tm, HG_WIDTH), row), pl.BlockSpec((tm, ATT_WIDTH), row),
                  pl.BlockSpec((tm, D_MODEL), lambda i: (i, 0)),
                  pl.BlockSpec((tm, D_MODEL), lambda i: (i, 1)),
                  pl.BlockSpec((tm, D_MODEL), row),
                  pl.BlockSpec((HG_WIDTH, D_MODEL), const, **resident),
                  pl.BlockSpec((ATT_WIDTH, D_MODEL), const, **resident),
                  pl.BlockSpec((D_MODEL, D_MODEL), const, **resident),
                  pl.BlockSpec((1, D_MODEL), const), pl.BlockSpec((1, D_MODEL), const)],
        out_specs=(pl.BlockSpec((tm, D_MODEL), row), pl.BlockSpec((tm, D_MODEL), row)),
        compiler_params=pltpu.CompilerParams(
            dimension_semantics=("parallel",), vmem_limit_bytes=VMEM_LIMIT),
        name="merge_ln1",
    )(oa, ob, proj, proj, x, wa, wb, wo, g1, b1)


def _ffn_kernel(xb_ref, x_ref, w1_ref, w2_ref, g_ref, b_ref, o_ref, acc_ref):
    j = pl.program_id(1)

    @pl.when(j == 0)
    def _():
        acc_ref[...] = jnp.zeros_like(acc_ref)

    h = jnp.maximum(_dot(xb_ref[...], w1_ref[...]), 0.0)
    acc_ref[...] += _dot((h * h).astype(BF16), w2_ref[...])

    @pl.when(j == pl.num_programs(1) - 1)
    def _():
        o_ref[...] = _layer_norm(ALPHA * x_ref[...] + acc_ref[...], g_ref[...], b_ref[...])


def _ffn(x1b, x1, w1, w2, g2, b2):
    tm, tf = 512, 512
    s = x1.shape[0]
    return pl.pallas_call(
        _ffn_kernel,
        out_shape=jax.ShapeDtypeStruct((s, D_MODEL), F32),
        grid=(s // tm, D_FF // tf),
        in_specs=[pl.BlockSpec((tm, D_MODEL), lambda i, j: (i, 0)),
                  pl.BlockSpec((tm, D_MODEL), lambda i, j: (i, 0)),
                  pl.BlockSpec((D_MODEL, tf), lambda i, j: (0, j)),
                  pl.BlockSpec((tf, D_MODEL), lambda i, j: (j, 0)),
                  pl.BlockSpec((1, D_MODEL), lambda i, j: (0, 0)),
                  pl.BlockSpec((1, D_MODEL), lambda i, j: (0, 0))],
        out_specs=pl.BlockSpec((tm, D_MODEL), lambda i, j: (i, 0)),
        scratch_shapes=[pltpu.VMEM((tm, D_MODEL), F32)],
        compiler_params=pltpu.CompilerParams(
            dimension_semantics=("parallel", "arbitrary"), vmem_limit_bytes=VMEM_LIMIT),
        name="ffn_ln2",
    )(x1b, x1, w1, w2, g2, b2)


def _rope_tables(seq):
    half = ATT_DIM // 2
    inv = ROPE_THETA ** (-jnp.arange(half, dtype=F32) / half)
    ang = jnp.arange(seq, dtype=F32)[:, None] * inv[None, :]
    cos, sin = jnp.cos(ang), jnp.sin(ang)
    cos_t = jnp.concatenate([cos, cos, cos, cos], axis=1)
    sin_t = jnp.concatenate([-sin, sin, -sin, sin], axis=1)
    return cos_t, sin_t


def kernel(x, w_in, hg_lb_logits, hg_norm_gain, attn_sinks, w_branch_a, w_branch_b, w_out,
           ln1_gain, ln1_bias, w_ff1, w_ff2, ln2_gain, ln2_bias):
    b, s, d = x.shape
    assert (b, s, d) == (1, SEQ, D_MODEL) and w_in.shape == (1, D_MODEL, D_IN)
    x2 = x.reshape(s, d)
    proj_h, proj_a, proj_g = _proj(x2.astype(BF16), w_in[0].astype(BF16))
    oa = _hgrn(proj_h, hg_lb_logits, hg_norm_gain)
    cos_t, sin_t = _rope_tables(s)
    sinks_b = jnp.broadcast_to(attn_sinks[0][:, None], (ATT_HEADS, 128))
    ob = _attn(proj_a, cos_t, sin_t, sinks_b)
    x1, x1b = _merge(oa, ob, proj_g, x2, w_branch_a[0].astype(BF16), w_branch_b[0].astype(BF16),
                     w_out[0].astype(BF16), ln1_gain, ln1_bias)
    out = _ffn(x1b, x1, w_ff1[0].astype(BF16), w_ff2[0].astype(BF16), ln2_gain, ln2_bias)
    return out.reshape(b, s, d)
```

```python
import functools
import math

import jax
import jax.numpy as jnp
from jax import lax
from jax.experimental import pallas as pl
from jax.experimental.pallas import tpu as pltpu

F32 = jnp.float32
BF16 = jnp.bfloat16

D_MODEL = 2048
SEQ = 8192
HG_HEADS = 8
HG_DIM = 128
HG_WIDTH = HG_HEADS * HG_DIM
ATT_HEADS = 16
ATT_KV_HEADS = 4
ATT_DIM = 64
ATT_GROUP = ATT_HEADS // ATT_KV_HEADS
ATT_WIDTH = ATT_HEADS * ATT_DIM
KV_WIDTH = ATT_KV_HEADS * ATT_DIM
WINDOW = 128
ROPE_THETA = 10000.0
D_FF = 4 * D_MODEL
D_IN = 4 * HG_WIDTH + ATT_WIDTH + 2 * KV_WIDTH + 2 * D_MODEL
ALPHA = 2.0 ** 0.25
LN_EPS = 1e-5
RMS_EPS = 1e-6
LOG2E = math.log2(math.e)

LANES = 128
HG_CHUNK = 64
HG_SUB = 8
NEG_BIG = -1e30

VMEM_LIMIT = 56 * 1024 * 1024


def _dot(a, b):
    return jnp.dot(a, b, preferred_element_type=F32)


def _dot_nt(a, b):
    return lax.dot_general(a, b, (((1,), (1,)), ((), ())), preferred_element_type=F32)


def _dot_tn(a, b):
    return lax.dot_general(a, b, (((0,), (0,)), ((), ())), preferred_element_type=F32)


PROJ_TM = 1024
PROJ_TN = 512
PROJ_EDGES = tuple(c // PROJ_TN for c in (
    0, HG_WIDTH, 2 * HG_WIDTH, 3 * HG_WIDTH, 4 * HG_WIDTH, 4 * HG_WIDTH + ATT_WIDTH,
    4 * HG_WIDTH + ATT_WIDTH + 2 * KV_WIDTH, 4 * HG_WIDTH + ATT_WIDTH + 2 * KV_WIDTH + D_MODEL, D_IN))
Q_SCALE = ATT_DIM ** -0.5 * LOG2E


def _rope(t, cos, sin_signed):
    w = t.shape[1]
    reps = w // cos.shape[1]
    cos_t = jnp.concatenate([cos] * reps, axis=1) if reps > 1 else cos
    sin_t = jnp.concatenate([sin_signed] * reps, axis=1) if reps > 1 else sin_signed
    lane = lax.broadcasted_iota(jnp.int32, t.shape, 1)
    first_half = (lane % ATT_DIM) < (ATT_DIM // 2)
    rot = jnp.where(first_half, pltpu.roll(t, w - ATT_DIM // 2, 1), pltpu.roll(t, ATT_DIM // 2, 1))
    return t * cos_t + rot * sin_t


def _dup_heads(t):
    lane = lax.broadcasted_iota(jnp.int32, (t.shape[0], LANES), 1)
    low = lane < ATT_DIM
    out = []
    for g in range(ATT_KV_HEADS):
        src = t[:, LANES * (g // 2):LANES * (g // 2 + 1)]
        swapped = pltpu.roll(src, ATT_DIM, 1)
        out.append(jnp.where(low, src, swapped) if g % 2 == 0 else jnp.where(low, swapped, src))
    return jnp.concatenate(out, axis=1)


def _proj_kernel(x_ref, w_ref, lbl_ref, cos_ref, sin_ref,
                 oq_ref, olf_ref, okk_ref, ov_ref, og_ref, oaq_ref, okv_ref, oga_ref, ogb_ref):
    j = pl.program_id(1)
    acc = _dot(x_ref[...], w_ref[...])
    e = PROJ_EDGES

    def seg(s):
        return (j >= e[s]) & (j < e[s + 1])

    @pl.when(seg(0))
    def _():
        oq_ref[...] = acc * jax.nn.sigmoid(acc)

    @pl.when(seg(1))
    def _():
        lbl = lbl_ref[...]
        lexp = jnp.exp(lbl - jnp.max(lbl, axis=0, keepdims=True))
        lb = lexp[0:1, :] / jnp.sum(lexp, axis=0, keepdims=True)
        f = lb + (1.0 - lb) * jax.nn.sigmoid(acc)
        olf_ref[...] = jnp.log(f)
        okk_ref[...] = 1.0 - f

    @pl.when(seg(2))
    def _():
        ov_ref[...] = acc.astype(BF16)

    @pl.when(seg(3))
    def _():
        og_ref[...] = (acc * jax.nn.sigmoid(acc)).astype(BF16)

    @pl.when(seg(4))
    def _():
        oaq_ref[...] = (_rope(acc, cos_ref[...], sin_ref[...]) * Q_SCALE).astype(BF16)

    @pl.when(seg(5))
    def _():
        k = _rope(acc[:, :KV_WIDTH], cos_ref[...], sin_ref[...])
        okv_ref[...] = jnp.concatenate(
            [_dup_heads(k), _dup_heads(acc[:, KV_WIDTH:])], axis=1).astype(BF16)

    @pl.when(seg(6))
    def _():
        oga_ref[...] = jax.nn.sigmoid(acc).astype(BF16)

    @pl.when(seg(7))
    def _():
        ogb_ref[...] = jax.nn.sigmoid(acc).astype(BF16)


def _proj(xb, wb, lb_logits, cos, sin_signed):
    tm, tn = PROJ_TM, PROJ_TN
    m, k = xb.shape
    e = PROJ_EDGES
    assert e[6] - e[5] == 1 and 2 * KV_WIDTH == tn

    def seg_spec(s, width=tn):
        lo, hi = e[s], e[s + 1]
        return pl.BlockSpec((tm, width), lambda i, j: (i, jnp.clip(j - lo, 0, hi - lo - 1)))

    def sds(width, dtype):
        return jax.ShapeDtypeStruct((m, width), dtype)

    return pl.pallas_call(
        _proj_kernel,
        out_shape=(sds(HG_WIDTH, F32), sds(HG_WIDTH, F32), sds(HG_WIDTH, F32), sds(HG_WIDTH, BF16),
                   sds(HG_WIDTH, BF16), sds(ATT_WIDTH, BF16), sds(4 * KV_WIDTH, BF16),
                   sds(D_MODEL, BF16), sds(D_MODEL, BF16)),
        grid=(m // tm, D_IN // tn),
        in_specs=[pl.BlockSpec((tm, k), lambda i, j: (i, 0)),
                  pl.BlockSpec((k, tn), lambda i, j: (0, j)),
                  pl.BlockSpec((lb_logits.shape[0], tn),
                               lambda i, j: (0, jnp.clip(j - e[1], 0, e[2] - e[1] - 1))),
                  pl.BlockSpec((tm, LANES), lambda i, j: (i, 0)),
                  pl.BlockSpec((tm, LANES), lambda i, j: (i, 0))],
        out_specs=(seg_spec(0), seg_spec(1), seg_spec(1), seg_spec(2), seg_spec(3), seg_spec(4),
                   pl.BlockSpec((tm, 4 * KV_WIDTH), lambda i, j: (i, 0)),
                   seg_spec(6), seg_spec(7)),
        compiler_params=pltpu.CompilerParams(
            dimension_semantics=("parallel", "arbitrary"), vmem_limit_bytes=VMEM_LIMIT),
        name="proj",
    )(xb, wb, lb_logits, cos, sin_signed)


def _hgrn_kernel(q_ref, lf_ref, kk_ref, v_ref, g_ref, gain_ref, o_ref, state_ref, *, rows, heads):
    @pl.when(pl.program_id(1) == 0)
    def _():
        state_ref[...] = jnp.zeros_like(state_ref)

    C, SB = HG_CHUNK, HG_SUB
    ri = lax.broadcasted_iota(jnp.int32, (C, C), 0)
    ci = lax.broadcasted_iota(jnp.int32, (C, C), 1)
    tri = (ri >= ci).astype(BF16)
    row = lax.broadcasted_iota(jnp.int32, (C, 1), 0)
    sub_row = lax.broadcasted_iota(jnp.int32, (SB, 1), 0)
    levels = []
    half = C // 2
    while half >= SB:
        pair = ((ri // (2 * half)) == (ci // (2 * half))) & ((ri % (2 * half)) >= half) \
            & ((ci % (2 * half)) < half)
        levels.append((half, (row % (2 * half)) >= half, jnp.where(pair, 1.0, 0.0)))
        half //= 2
    oi = lax.broadcasted_iota(jnp.int32, (2 * HG_DIM, 2 * HG_DIM), 0)
    oj = lax.broadcasted_iota(jnp.int32, (2 * HG_DIM, 2 * HG_DIM), 1)
    ones_bd = ((oi // HG_DIM) == (oj // HG_DIM)).astype(BF16)
    n_units = (C // SB) * SB
    assert n_units * SB == 2 * 2 * HG_DIM

    def chunk(c, carry):
        r0 = pl.multiple_of(c * C, C)
        for h in range(heads):
            sl = slice(h * HG_DIM, (h + 1) * HG_DIM)
            qf = q_ref[pl.ds(r0, C), sl]
            lf = lf_ref[pl.ds(r0, C), sl]
            kk = kk_ref[pl.ds(r0, C), sl]
            vb = v_ref[pl.ds(r0, C), sl]
            vf = vb.astype(F32)

            hi = lf.astype(BF16)
            lo = (lf - hi.astype(F32)).astype(BF16)
            b = (_dot(tri, hi) + _dot(tri, lo)) * LOG2E

            st = state_ref[h]
            o = _dot_nt((qf * jnp.exp2(b)).astype(BF16), st.astype(BF16))
            b_last = b[C - 1:C, :]
            kd = kk * jnp.exp2(b_last - b)
            state_ref[h] = st * jnp.exp2(b_last) + _dot_tn(vb, kd.astype(BF16))

            s_off = None
            for lvl, second_half, pair in levels:
                ref = jnp.concatenate(
                    [jnp.broadcast_to(b[base + lvl:base + lvl + 1, :], (2 * lvl, HG_DIM))
                     for base in range(0, C, 2 * lvl)], axis=0)
                e = jnp.exp2(-jnp.abs(b - ref))
                qs = jnp.where(second_half, qf * e, 0.0)
                ks = jnp.where(second_half, 0.0, kk * e)
                s = _dot_nt(qs.astype(BF16), ks.astype(BF16)) * pair
                s_off = s if s_off is None else s_off + s
            o = o + _dot(s_off.astype(BF16), vb)

            ps = []
            for i in range(C // SB):
                base = i * SB
                bi = b[base:base + SB, :]
                qi = qf[base:base + SB, :]
                for j in range(SB):
                    d = jnp.where(sub_row >= j, bi - b[base + j:base + j + 1, :], NEG_BIG)
                    ps.append(qi * jnp.exp2(d) * kk[base + j:base + j + 1, :])
            nh = n_units // 2
            p = jnp.concatenate([jnp.concatenate(ps[:nh], axis=0),
                                 jnp.concatenate(ps[nh:], axis=0)], axis=1)
            r = _dot(p.astype(BF16), ones_bd)
            diag = []
            for i in range(C // SB):
                acc = None
                for j in range(SB):
                    u = i * SB + j
                    ru = r[(u % nh) * SB:(u % nh + 1) * SB, (u // nh) * HG_DIM:(u // nh + 1) * HG_DIM]
                    term = ru * vf[i * SB + j:i * SB + j + 1, :]
                    acc = term if acc is None else acc + term
                diag.append(acc)
            o = o + jnp.concatenate(diag, axis=0)

            ms = jnp.mean(o * o, axis=-1, keepdims=True)
            o = o * lax.rsqrt(ms + RMS_EPS) * gain_ref[:, sl]
            o = o * g_ref[pl.ds(r0, C), sl].astype(F32)
            o_ref[pl.ds(r0, C), sl] = o.astype(o_ref.dtype)
        return carry

    lax.fori_loop(0, rows // C, chunk, 0)


def _hgrn(q, lf, kk, v, g, gain, *, rows=256, heads=4):
    s = q.shape[0]
    blk = pl.BlockSpec((rows, heads * HG_DIM), lambda hb, t: (t, hb))
    return pl.pallas_call(
        functools.partial(_hgrn_kernel, rows=rows, heads=heads),
        out_shape=jax.ShapeDtypeStruct((s, HG_WIDTH), BF16),
        grid=(HG_HEADS // heads, s // rows),
        in_specs=[blk, blk, blk, blk, blk,
                  pl.BlockSpec((1, heads * HG_DIM), lambda hb, t: (0, hb))],
        out_specs=blk,
        scratch_shapes=[pltpu.VMEM((heads, HG_DIM, HG_DIM), F32)],
        compiler_params=pltpu.CompilerParams(
            dimension_semantics=("parallel", "arbitrary"), vmem_limit_bytes=VMEM_LIMIT),
        name="hgrn2",
    )(q, lf, kk, v, g, gain)


def _attn_kernel(q_ref, kvp_ref, kvc_ref, sink_ref, o_ref):
    n = pl.program_id(0)
    blk = q_ref.shape[0]
    qi = lax.broadcasted_iota(jnp.int32, (blk, 2 * blk), 0)
    ci = lax.broadcasted_iota(jnp.int32, (blk, 2 * blk), 1)
    rel = qi + blk - ci
    first_key = jnp.where(n > 0, 0, blk)
    bias1 = jnp.where(rel >= 0, jnp.where(rel < blk, jnp.where(ci >= first_key, 0.0, NEG_BIG),
                                          NEG_BIG), NEG_BIG)
    bias = jnp.concatenate([bias1] * ATT_GROUP, axis=0)
    lane = lax.broadcasted_iota(jnp.int32, (blk, LANES), 1)
    low = lane < ATT_DIM
    zero = jnp.zeros((), BF16)

    for g in range(ATT_KV_HEADS):
        ksl = slice(g * LANES, (g + 1) * LANES)
        vsl = slice(4 * KV_WIDTH // 2 + g * LANES, 4 * KV_WIDTH // 2 + (g + 1) * LANES)
        kd = jnp.concatenate([kvp_ref[:, ksl], kvc_ref[:, ksl]], axis=0)
        vd = jnp.concatenate([kvp_ref[:, vsl], kvc_ref[:, vsl]], axis=0)
        qs = []
        for pr in range(ATT_GROUP // 2):
            qp = q_ref[:, (g * ATT_GROUP // 2 + pr) * LANES:(g * ATT_GROUP // 2 + pr + 1) * LANES]
            qs += [jnp.where(low, qp, zero), jnp.where(low, zero, qp)]
        s = _dot_nt(jnp.concatenate(qs, axis=0), kd) + bias
        sink = jnp.concatenate(
            [jnp.broadcast_to(sink_ref[g * ATT_GROUP + u:g * ATT_GROUP + u + 1, 0:1], (blk, 1))
             for u in range(ATT_GROUP)], axis=0) * LOG2E
        m = jnp.maximum(jnp.max(s, axis=-1, keepdims=True), sink)
        p = jnp.exp2(s - m)
        denom = jnp.sum(p, axis=-1, keepdims=True) + jnp.exp2(sink - m)
        t = _dot(p.astype(BF16), vd) / denom
        for pr in range(ATT_GROUP // 2):
            out = jnp.where(low, t[2 * pr * blk:(2 * pr + 1) * blk], t[(2 * pr + 1) * blk:(2 * pr + 2) * blk])
            col = (g * ATT_GROUP // 2 + pr) * LANES
            o_ref[:, col:col + LANES] = out.astype(o_ref.dtype)


def _attn(aq, akv, sinks_b):
    blk = WINDOW
    s = aq.shape[0]
    return pl.pallas_call(
        _attn_kernel,
        out_shape=jax.ShapeDtypeStruct((s, ATT_WIDTH), BF16),
        grid=(s // blk,),
        in_specs=[pl.BlockSpec((blk, ATT_WIDTH), lambda n: (n, 0)),
                  pl.BlockSpec((blk, 4 * KV_WIDTH), lambda n: (jnp.maximum(n - 1, 0), 0)),
                  pl.BlockSpec((blk, 4 * KV_WIDTH), lambda n: (n, 0)),
                  pl.BlockSpec((ATT_HEADS, LANES), lambda n: (0, 0))],
        out_specs=pl.BlockSpec((blk, ATT_WIDTH), lambda n: (n, 0)),
        compiler_params=pltpu.CompilerParams(
            dimension_semantics=("parallel",), vmem_limit_bytes=VMEM_LIMIT),
        name="swa",
    )(aq, akv, akv, sinks_b)


def _layer_norm(r, gain, bias):
    mu = jnp.mean(r, axis=-1, keepdims=True)
    d = r - mu
    var = jnp.mean(d * d, axis=-1, keepdims=True)
    return d * lax.rsqrt(var + LN_EPS) * gain + bias


def _merge_kernel(oa_ref, ob_ref, ga_ref, gb_ref, x_ref, wa_ref, wb_ref, wo_ref, g_ref, b_ref,
                  x1_ref, x1b_ref):
    ya = _dot(oa_ref[...], wa_ref[...])
    yb = _dot(ob_ref[...], wb_ref[...])
    mixed = ga_ref[...].astype(F32) * ya + gb_ref[...].astype(F32) * yb
    z = _dot(mixed.astype(BF16), wo_ref[...])
    x1 = _layer_norm(ALPHA * x_ref[...] + z, g_ref[...], b_ref[...])
    x1_ref[...] = x1
    x1b_ref[...] = x1.astype(BF16)


def _merge(oa, ob, ga, gb, x, wa, wb, wo, g1, b1):
    tm = 256
    s = x.shape[0]
    row = lambda i: (i, 0)
    const = lambda i: (0, 0)
    resident = dict(pipeline_mode=pl.Buffered(1))
    return pl.pallas_call(
        _merge_kernel,
        out_shape=(jax.ShapeDtypeStruct((s, D_MODEL), F32), jax.ShapeDtypeStruct((s, D_MODEL), BF16)),
        grid=(s // tm,),
        in_specs=[pl.BlockSpec((tm, HG_WIDTH), row), pl.BlockSpec((tm, ATT_WIDTH), row),
                  pl.BlockSpec((tm, D_MODEL), row), pl.BlockSpec((tm, D_MODEL), row),
                  pl.BlockSpec((tm, D_MODEL), row),
                  pl.BlockSpec((HG_WIDTH, D_MODEL), const, **resident),
                  pl.BlockSpec((ATT_WIDTH, D_MODEL), const, **resident),
                  pl.BlockSpec((D_MODEL, D_MODEL), const, **resident),
                  pl.BlockSpec((1, D_MODEL), const), pl.BlockSpec((1, D_MODEL), const)],
        out_specs=(pl.BlockSpec((tm, D_MODEL), row), pl.BlockSpec((tm, D_MODEL), row)),
        compiler_params=pltpu.CompilerParams(
            dimension_semantics=("parallel",), vmem_limit_bytes=VMEM_LIMIT),
        name="merge_ln1",
    )(oa, ob, ga, gb, x, wa, wb, wo, g1, b1)


def _ffn_kernel(xb_ref, x_ref, w1_ref, w2_ref, g_ref, b_ref, o_ref, acc_ref):
    j = pl.program_id(1)

    @pl.when(j == 0)
    def _():
        acc_ref[...] = jnp.zeros_like(acc_ref)

    h = jnp.maximum(_dot(xb_ref[...], w1_ref[...]), 0.0)
    acc_ref[...] += _dot((h * h).astype(BF16), w2_ref[...])

    @pl.when(j == pl.num_programs(1) - 1)
    def _():
        o_ref[...] = _layer_norm(ALPHA * x_ref[...] + acc_ref[...], g_ref[...], b_ref[...])


def _ffn(x1b, x1, w1, w2, g2, b2):
    tm, tf = 512, 512
    s = x1.shape[0]
    return pl.pallas_call(
        _ffn_kernel,
        out_shape=jax.ShapeDtypeStruct((s, D_MODEL), F32),
        grid=(s // tm, D_FF // tf),
        in_specs=[pl.BlockSpec((tm, D_MODEL), lambda i, j: (i, 0)),
                  pl.BlockSpec((tm, D_MODEL), lambda i, j: (i, 0)),
                  pl.BlockSpec((D_MODEL, tf), lambda i, j: (0, j)),
                  pl.BlockSpec((tf, D_MODEL), lambda i, j: (j, 0)),
                  pl.BlockSpec((1, D_MODEL), lambda i, j: (0, 0)),
                  pl.BlockSpec((1, D_MODEL), lambda i, j: (0, 0))],
        out_specs=pl.BlockSpec((tm, D_MODEL), lambda i, j: (i, 0)),
        scratch_shapes=[pltpu.VMEM((tm, D_MODEL), F32)],
        compiler_params=pltpu.CompilerParams(
            dimension_semantics=("parallel", "arbitrary"), vmem_limit_bytes=VMEM_LIMIT),
        name="ffn_ln2",
    )(x1b, x1, w1, w2, g2, b2)


def _rope_tables(seq):
    half = ATT_DIM // 2
    inv = ROPE_THETA ** (-jnp.arange(half, dtype=F32) / half)
    ang = jnp.arange(seq, dtype=F32)[:, None] * inv[None, :]
    cos, sin = jnp.cos(ang), jnp.sin(ang)
    cos_t = jnp.concatenate([cos, cos, cos, cos], axis=1)
    sin_t = jnp.concatenate([-sin, sin, -sin, sin], axis=1)
    return cos_t, sin_t


def kernel(x, w_in, hg_lb_logits, hg_norm_gain, attn_sinks, w_branch_a, w_branch_b, w_out,
           ln1_gain, ln1_bias, w_ff1, w_ff2, ln2_gain, ln2_bias):
    b, s, d = x.shape
    assert (b, s, d) == (1, SEQ, D_MODEL) and w_in.shape == (1, D_MODEL, D_IN)
    x2 = x.reshape(s, d)
    cos_t, sin_t = _rope_tables(s)
    hq, lf, kk, hv, hg, aq, akv, ga, gb = _proj(
        x2.astype(BF16), w_in[0].astype(BF16), hg_lb_logits, cos_t, sin_t)
    oa = _hgrn(hq, lf, kk, hv, hg, hg_norm_gain)
    sinks_b = jnp.broadcast_to(attn_sinks[0][:, None], (ATT_HEADS, LANES))
    ob = _attn(aq, akv, sinks_b)
    x1, x1b = _merge(oa, ob, ga, gb, x2, w_branch_a[0].astype(BF16), w_branch_b[0].astype(BF16),
                     w_out[0].astype(BF16), ln1_gain, ln1_bias)
    out = _ffn(x1b, x1, w_ff1[0].astype(BF16), w_ff2[0].astype(BF16), ln2_gain, ln2_bias)
    return out.reshape(b, s, d)
```

```python
import functools
import math

import jax
import jax.numpy as jnp
from jax import lax
from jax.experimental import pallas as pl
from jax.experimental.pallas import tpu as pltpu

F32 = jnp.float32
BF16 = jnp.bfloat16

D_MODEL = 2048
SEQ = 8192
HG_HEADS = 8
HG_DIM = 128
HG_WIDTH = HG_HEADS * HG_DIM
ATT_HEADS = 16
ATT_KV_HEADS = 4
ATT_DIM = 64
ATT_GROUP = ATT_HEADS // ATT_KV_HEADS
ATT_WIDTH = ATT_HEADS * ATT_DIM
KV_WIDTH = ATT_KV_HEADS * ATT_DIM
WINDOW = 128
ROPE_THETA = 10000.0
D_FF = 4 * D_MODEL
D_IN = 4 * HG_WIDTH + ATT_WIDTH + 2 * KV_WIDTH + 2 * D_MODEL
ALPHA = 2.0 ** 0.25
LN_EPS = 1e-5
RMS_EPS = 1e-6
LOG2E = math.log2(math.e)

LANES = 128
HG_CHUNK = 64
HG_SUB = 8
NEG_BIG = -1e30

VMEM_LIMIT = 56 * 1024 * 1024


def _dot(a, b):
    return jnp.dot(a, b, preferred_element_type=F32)


def _dot_nt(a, b):
    return lax.dot_general(a, b, (((1,), (1,)), ((), ())), preferred_element_type=F32)


def _dot_tn(a, b):
    return lax.dot_general(a, b, (((0,), (0,)), ((), ())), preferred_element_type=F32)


PROJ_TM = 1024
PROJ_TN = 512
PROJ_RS = 256
PROJ_EDGES = tuple(c // PROJ_TN for c in (
    0, HG_WIDTH, 2 * HG_WIDTH, 3 * HG_WIDTH, 4 * HG_WIDTH, 4 * HG_WIDTH + ATT_WIDTH,
    4 * HG_WIDTH + ATT_WIDTH + 2 * KV_WIDTH, 4 * HG_WIDTH + ATT_WIDTH + 2 * KV_WIDTH + D_MODEL, D_IN))
Q_SCALE = ATT_DIM ** -0.5 * LOG2E


def _rope(t, cos, sin_signed):
    w = t.shape[1]
    reps = w // cos.shape[1]
    cos_t = jnp.concatenate([cos] * reps, axis=1) if reps > 1 else cos
    sin_t = jnp.concatenate([sin_signed] * reps, axis=1) if reps > 1 else sin_signed
    lane = lax.broadcasted_iota(jnp.int32, t.shape, 1)
    first_half = (lane % ATT_DIM) < (ATT_DIM // 2)
    rot = jnp.where(first_half, pltpu.roll(t, w - ATT_DIM // 2, 1), pltpu.roll(t, ATT_DIM // 2, 1))
    return t * cos_t + rot * sin_t


def _dup_heads(t):
    lane = lax.broadcasted_iota(jnp.int32, (t.shape[0], LANES), 1)
    low = lane < ATT_DIM
    out = []
    for g in range(ATT_KV_HEADS):
        src = t[:, LANES * (g // 2):LANES * (g // 2 + 1)]
        swapped = pltpu.roll(src, ATT_DIM, 1)
        out.append(jnp.where(low, src, swapped) if g % 2 == 0 else jnp.where(low, swapped, src))
    return jnp.concatenate(out, axis=1)


def _proj_kernel(x_ref, w_ref, lbl_ref, cos_ref, sin_ref,
                 oq_ref, olf_ref, okk_ref, ov_ref, og_ref, oaq_ref, okv_ref, oga_ref, ogb_ref):
    j = pl.program_id(1)
    e = PROJ_EDGES

    def segment(s):
        def deco(epilogue):
            @pl.when((j >= e[s]) & (j < e[s + 1]))
            def _():
                for r in range(PROJ_TM // PROJ_RS):
                    rows = slice(r * PROJ_RS, (r + 1) * PROJ_RS)
                    epilogue(rows, _dot(x_ref[rows, :], w_ref[...]))
        return deco

    @segment(0)
    def _(rows, acc):
        oq_ref[rows, :] = acc * jax.nn.sigmoid(acc)

    @segment(1)
    def _(rows, acc):
        lbl = lbl_ref[...]
        lexp = jnp.exp(lbl - jnp.max(lbl, axis=0, keepdims=True))
        lb = lexp[0:1, :] / jnp.sum(lexp, axis=0, keepdims=True)
        f = lb + (1.0 - lb) * jax.nn.sigmoid(acc)
        olf_ref[rows, :] = jnp.log(f)
        okk_ref[rows, :] = 1.0 - f

    @segment(2)
    def _(rows, acc):
        ov_ref[rows, :] = acc.astype(BF16)

    @segment(3)
    def _(rows, acc):
        og_ref[rows, :] = (acc * jax.nn.sigmoid(acc)).astype(BF16)

    @segment(4)
    def _(rows, acc):
        oaq_ref[rows, :] = (_rope(acc, cos_ref[rows, :], sin_ref[rows, :]) * Q_SCALE).astype(BF16)

    @segment(5)
    def _(rows, acc):
        k = _rope(acc[:, :KV_WIDTH], cos_ref[rows, :], sin_ref[rows, :])
        okv_ref[rows, :] = jnp.concatenate(
            [_dup_heads(k), _dup_heads(acc[:, KV_WIDTH:])], axis=1).astype(BF16)

    @segment(6)
    def _(rows, acc):
        oga_ref[rows, :] = jax.nn.sigmoid(acc).astype(BF16)

    @segment(7)
    def _(rows, acc):
        ogb_ref[rows, :] = jax.nn.sigmoid(acc).astype(BF16)


def _proj(xb, wb, lb_logits, cos, sin_signed):
    tm, tn = PROJ_TM, PROJ_TN
    m, k = xb.shape
    e = PROJ_EDGES
    assert e[6] - e[5] == 1 and 2 * KV_WIDTH == tn

    def seg_spec(s, width=tn):
        lo, hi = e[s], e[s + 1]
        return pl.BlockSpec((tm, width), lambda i, j: (i, jnp.clip(j - lo, 0, hi - lo - 1)))

    def sds(width, dtype):
        return jax.ShapeDtypeStruct((m, width), dtype)

    return pl.pallas_call(
        _proj_kernel,
        out_shape=(sds(HG_WIDTH, F32), sds(HG_WIDTH, F32), sds(HG_WIDTH, F32), sds(HG_WIDTH, BF16),
                   sds(HG_WIDTH, BF16), sds(ATT_WIDTH, BF16), sds(4 * KV_WIDTH, BF16),
                   sds(D_MODEL, BF16), sds(D_MODEL, BF16)),
        grid=(m // tm, D_IN // tn),
        in_specs=[pl.BlockSpec((tm, k), lambda i, j: (i, 0)),
                  pl.BlockSpec((k, tn), lambda i, j: (0, j)),
                  pl.BlockSpec((lb_logits.shape[0], tn),
                               lambda i, j: (0, jnp.clip(j - e[1], 0, e[2] - e[1] - 1))),
                  pl.BlockSpec((tm, LANES), lambda i, j: (i, 0)),
                  pl.BlockSpec((tm, LANES), lambda i, j: (i, 0))],
        out_specs=(seg_spec(0), seg_spec(1), seg_spec(1), seg_spec(2), seg_spec(3), seg_spec(4),
                   pl.BlockSpec((tm, 4 * KV_WIDTH), lambda i, j: (i, 0)),
                   seg_spec(6), seg_spec(7)),
        compiler_params=pltpu.CompilerParams(
            dimension_semantics=("parallel", "arbitrary"), vmem_limit_bytes=VMEM_LIMIT),
        name="proj",
    )(xb, wb, lb_logits, cos, sin_signed)


def _hgrn_kernel(q_ref, lf_ref, kk_ref, v_ref, g_ref, gain_ref, o_ref, state_ref, c_scr, v_scr,
                 *, rows, heads):
    @pl.when(pl.program_id(1) == 0)
    def _():
        state_ref[...] = jnp.zeros_like(state_ref)

    C, SB = HG_CHUNK, HG_SUB
    ri = lax.broadcasted_iota(jnp.int32, (C, C), 0)
    ci = lax.broadcasted_iota(jnp.int32, (C, C), 1)
    tri = (ri >= ci).astype(BF16)
    row = lax.broadcasted_iota(jnp.int32, (C, 1), 0)
    sub_row = lax.broadcasted_iota(jnp.int32, (SB, 1), 0)
    levels = []
    half = C // 2
    while half >= SB:
        pair = ((ri // (2 * half)) == (ci // (2 * half))) & ((ri % (2 * half)) >= half) \
            & ((ci % (2 * half)) < half)
        levels.append((half, (row % (2 * half)) >= half, jnp.where(pair, 1.0, 0.0)))
        half //= 2
    oi = lax.broadcasted_iota(jnp.int32, (2 * HG_DIM, 2 * HG_DIM), 0)
    oj = lax.broadcasted_iota(jnp.int32, (2 * HG_DIM, 2 * HG_DIM), 1)
    ones_bd = ((oi // HG_DIM) == (oj // HG_DIM)).astype(BF16)
    n_units = (C // SB) * SB
    assert n_units * SB == 2 * 2 * HG_DIM

    def chunk(c, carry):
        r0 = pl.multiple_of(c * C, C)
        hs = range(heads)
        sls = [slice(h * HG_DIM, (h + 1) * HG_DIM) for h in hs]
        qf = [q_ref[pl.ds(r0, C), sl] for sl in sls]
        lf = [lf_ref[pl.ds(r0, C), sl] for sl in sls]
        kk = [kk_ref[pl.ds(r0, C), sl] for sl in sls]
        vb = [v_ref[pl.ds(r0, C), sl] for sl in sls]

        b = []
        for h in hs:
            hi = lf[h].astype(BF16)
            lo = (lf[h] - hi.astype(F32)).astype(BF16)
            b.append((_dot(tri, hi) + _dot(tri, lo)) * LOG2E)

        o = []
        for h in hs:
            st = state_ref[h]
            o.append(_dot_nt((qf[h] * jnp.exp2(b[h])).astype(BF16), st.astype(BF16)))
            b_last = b[h][C - 1:C, :]
            kd = kk[h] * jnp.exp2(b_last - b[h])
            state_ref[h] = st * jnp.exp2(b_last) + _dot_tn(vb[h], kd.astype(BF16))

        s_off = [None] * heads
        for lvl, second_half, pair in levels:
            for h in hs:
                ref = jnp.concatenate(
                    [jnp.broadcast_to(b[h][base + lvl:base + lvl + 1, :], (2 * lvl, HG_DIM))
                     for base in range(0, C, 2 * lvl)], axis=0)
                e = jnp.exp2(-jnp.abs(b[h] - ref))
                qs = jnp.where(second_half, qf[h] * e, 0.0)
                ks = jnp.where(second_half, 0.0, kk[h] * e)
                s = _dot_nt(qs.astype(BF16), ks.astype(BF16)) * pair
                s_off[h] = s if s_off[h] is None else s_off[h] + s
        for h in hs:
            o[h] = o[h] + _dot(s_off[h].astype(BF16), vb[h])

        nh = n_units // 2
        for h in hs:
            c_scr[h] = b[h] - jnp.log2(jnp.maximum(kk[h], 0.0))
            v_scr[h] = vb[h].astype(F32)
        r = []
        for h in hs:
            ps = []
            for i in range(C // SB):
                base = i * SB
                bi = b[h][base:base + SB, :]
                qi = qf[h][base:base + SB, :]
                for j in range(SB):
                    cj = c_scr[h, base + j:base + j + 1, :]
                    ps.append(qi * jnp.exp2(jnp.where(sub_row >= j, bi - cj, NEG_BIG)))
            p = jnp.concatenate([jnp.concatenate(ps[:nh], axis=0),
                                 jnp.concatenate(ps[nh:], axis=0)], axis=1)
            r.append(_dot(p.astype(BF16), ones_bd))

        for h in hs:
            diag = []
            for i in range(C // SB):
                acc = None
                for j in range(SB):
                    u = i * SB + j
                    ru = r[h][(u % nh) * SB:(u % nh + 1) * SB,
                              (u // nh) * HG_DIM:(u // nh + 1) * HG_DIM]
                    term = ru * v_scr[h, i * SB + j:i * SB + j + 1, :]
                    acc = term if acc is None else acc + term
                diag.append(acc)
            out = o[h] + jnp.concatenate(diag, axis=0)
            ms = jnp.mean(out * out, axis=-1, keepdims=True)
            out = out * lax.rsqrt(ms + RMS_EPS) * gain_ref[:, sls[h]]
            out = out * g_ref[pl.ds(r0, C), sls[h]].astype(F32)
            o_ref[pl.ds(r0, C), sls[h]] = out.astype(o_ref.dtype)
        return carry

    lax.fori_loop(0, rows // C, chunk, 0)


def _hgrn(q, lf, kk, v, g, gain, *, rows=256, heads=8):
    s = q.shape[0]
    blk = pl.BlockSpec((rows, heads * HG_DIM), lambda hb, t: (t, hb))
    return pl.pallas_call(
        functools.partial(_hgrn_kernel, rows=rows, heads=heads),
        out_shape=jax.ShapeDtypeStruct((s, HG_WIDTH), BF16),
        grid=(HG_HEADS // heads, s // rows),
        in_specs=[blk, blk, blk, blk, blk,
                  pl.BlockSpec((1, heads * HG_DIM), lambda hb, t: (0, hb))],
        out_specs=blk,
        scratch_shapes=[pltpu.VMEM((heads, HG_DIM, HG_DIM), F32),
                        pltpu.VMEM((heads, HG_CHUNK, HG_DIM), F32),
                        pltpu.VMEM((heads, HG_CHUNK, HG_DIM), F32)],
        compiler_params=pltpu.CompilerParams(
            dimension_semantics=("parallel", "arbitrary"), vmem_limit_bytes=VMEM_LIMIT),
        name="hgrn2",
    )(q, lf, kk, v, g, gain)


def _attn_kernel(q_ref, kvp_ref, kvc_ref, sink_ref, o_ref):
    n = pl.program_id(0)
    blk = q_ref.shape[0]
    qi = lax.broadcasted_iota(jnp.int32, (blk, 2 * blk), 0)
    ci = lax.broadcasted_iota(jnp.int32, (blk, 2 * blk), 1)
    rel = qi + blk - ci
    first_key = jnp.where(n > 0, 0, blk)
    bias1 = jnp.where(rel >= 0, jnp.where(rel < blk, jnp.where(ci >= first_key, 0.0, NEG_BIG),
                                          NEG_BIG), NEG_BIG)
    bias = jnp.concatenate([bias1] * ATT_GROUP, axis=0)
    lane = lax.broadcasted_iota(jnp.int32, (blk, LANES), 1)
    low = lane < ATT_DIM
    zero = jnp.zeros((), BF16)

    groups = range(ATT_KV_HEADS)
    pairs = range(ATT_GROUP // 2)
    s = []
    for g in groups:
        ksl = slice(g * LANES, (g + 1) * LANES)
        kd = jnp.concatenate([kvp_ref[:, ksl], kvc_ref[:, ksl]], axis=0)
        qs = []
        for pr in pairs:
            col = (g * len(pairs) + pr) * LANES
            qp = q_ref[:, col:col + LANES]
            qs += [jnp.where(low, qp, zero), jnp.where(low, zero, qp)]
        s.append(_dot_nt(jnp.concatenate(qs, axis=0), kd) + bias)

    p, sink_w = [], []
    for g in groups:
        sink = jnp.concatenate(
            [jnp.broadcast_to(sink_ref[g * ATT_GROUP + u:g * ATT_GROUP + u + 1, :], (blk, LANES))
             for u in range(ATT_GROUP)], axis=0) * LOG2E
        m = jnp.max(jnp.maximum(s[g][:, :blk], s[g][:, blk:]), axis=-1, keepdims=True)
        m = jnp.maximum(jnp.broadcast_to(m, (ATT_GROUP * blk, LANES)), sink)
        p.append(jnp.exp2(s[g] - jnp.concatenate([m, m], axis=1)).astype(BF16))
        sink_w.append(jnp.exp2(sink - m))

    ones_kl = jnp.ones((2 * blk, LANES), BF16)
    t = []
    for g in groups:
        vsl = slice(2 * KV_WIDTH + g * LANES, 2 * KV_WIDTH + (g + 1) * LANES)
        vd = jnp.concatenate([kvp_ref[:, vsl], kvc_ref[:, vsl]], axis=0)
        denom = _dot(p[g], ones_kl) + sink_w[g]
        t.append(_dot(p[g], vd) / denom)

    for g in groups:
        for pr in pairs:
            out = jnp.where(low, t[g][2 * pr * blk:(2 * pr + 1) * blk],
                            t[g][(2 * pr + 1) * blk:(2 * pr + 2) * blk])
            col = (g * len(pairs) + pr) * LANES
            o_ref[:, col:col + LANES] = out.astype(o_ref.dtype)


def _attn(aq, akv, sinks_b):
    blk = WINDOW
    s = aq.shape[0]
    return pl.pallas_call(
        _attn_kernel,
        out_shape=jax.ShapeDtypeStruct((s, ATT_WIDTH), BF16),
        grid=(s // blk,),
        in_specs=[pl.BlockSpec((blk, ATT_WIDTH), lambda n: (n, 0)),
                  pl.BlockSpec((blk, 4 * KV_WIDTH), lambda n: (jnp.maximum(n - 1, 0), 0)),
                  pl.BlockSpec((blk, 4 * KV_WIDTH), lambda n: (n, 0)),
                  pl.BlockSpec((ATT_HEADS, LANES), lambda n: (0, 0))],
        out_specs=pl.BlockSpec((blk, ATT_WIDTH), lambda n: (n, 0)),
        compiler_params=pltpu.CompilerParams(
            dimension_semantics=("parallel",), vmem_limit_bytes=VMEM_LIMIT),
        name="swa",
    )(aq, akv, akv, sinks_b)


def _layer_norm(r, gain, bias):
    mu = jnp.mean(r, axis=-1, keepdims=True)
    d = r - mu
    var = jnp.mean(d * d, axis=-1, keepdims=True)
    return d * lax.rsqrt(var + LN_EPS) * gain + bias


def _merge_kernel(oa_ref, ob_ref, ga_ref, gb_ref, x_ref, wa_ref, wb_ref, wo_ref, g_ref, b_ref,
                  x1_ref, x1b_ref):
    for r in range(MERGE_TM // MERGE_RS):
        rows = slice(r * MERGE_RS, (r + 1) * MERGE_RS)
        ya = _dot(oa_ref[rows, :], wa_ref[...])
        yb = _dot(ob_ref[rows, :], wb_ref[...])
        mixed = ga_ref[rows, :].astype(F32) * ya + gb_ref[rows, :].astype(F32) * yb
        z = _dot(mixed.astype(BF16), wo_ref[...])
        x1 = _layer_norm(ALPHA * x_ref[rows, :] + z, g_ref[...], b_ref[...])
        x1_ref[rows, :] = x1
        x1b_ref[rows, :] = x1.astype(BF16)


MERGE_TM = 256
MERGE_RS = 128


def _merge(oa, ob, ga, gb, x, wa, wb, wo, g1, b1):
    tm = MERGE_TM
    s = x.shape[0]
    row = lambda i: (i, 0)
    const = lambda i: (0, 0)
    resident = dict(pipeline_mode=pl.Buffered(1))
    return pl.pallas_call(
        _merge_kernel,
        out_shape=(jax.ShapeDtypeStruct((s, D_MODEL), F32), jax.ShapeDtypeStruct((s, D_MODEL), BF16)),
        grid=(s // tm,),
        in_specs=[pl.BlockSpec((tm, HG_WIDTH), row), pl.BlockSpec((tm, ATT_WIDTH), row),
                  pl.BlockSpec((tm, D_MODEL), row), pl.BlockSpec((tm, D_MODEL), row),
                  pl.BlockSpec((tm, D_MODEL), row),
                  pl.BlockSpec((HG_WIDTH, D_MODEL), const, **resident),
                  pl.BlockSpec((ATT_WIDTH, D_MODEL), const, **resident),
                  pl.BlockSpec((D_MODEL, D_MODEL), const, **resident),
                  pl.BlockSpec((1, D_MODEL), const), pl.BlockSpec((1, D_MODEL), const)],
        out_specs=(pl.BlockSpec((tm, D_MODEL), row), pl.BlockSpec((tm, D_MODEL), row)),
        compiler_params=pltpu.CompilerParams(
            dimension_semantics=("parallel",), vmem_limit_bytes=VMEM_LIMIT),
        name="merge_ln1",
    )(oa, ob, ga, gb, x, wa, wb, wo, g1, b1)


def _ffn_kernel(xb_ref, x_ref, w1_ref, w2_ref, g_ref, b_ref, o_ref, acc_ref):
    j = pl.program_id(1)

    @pl.when(j == 0)
    def _():
        acc_ref[...] = jnp.zeros_like(acc_ref)

    h = jnp.maximum(_dot(xb_ref[...], w1_ref[...]), 0.0)
    acc_ref[...] += _dot((h * h).astype(BF16), w2_ref[...])

    @pl.when(j == pl.num_programs(1) - 1)
    def _():
        o_ref[...] = _layer_norm(ALPHA * x_ref[...] + acc_ref[...], g_ref[...], b_ref[...])


def _ffn(x1b, x1, w1, w2, g2, b2):
    tm, tf = 512, 512
    s = x1.shape[0]
    return pl.pallas_call(
        _ffn_kernel,
        out_shape=jax.ShapeDtypeStruct((s, D_MODEL), F32),
        grid=(s // tm, D_FF // tf),
        in_specs=[pl.BlockSpec((tm, D_MODEL), lambda i, j: (i, 0)),
                  pl.BlockSpec((tm, D_MODEL), lambda i, j: (i, 0)),
                  pl.BlockSpec((D_MODEL, tf), lambda i, j: (0, j)),
                  pl.BlockSpec((tf, D_MODEL), lambda i, j: (j, 0)),
                  pl.BlockSpec((1, D_MODEL), lambda i, j: (0, 0)),
                  pl.BlockSpec((1, D_MODEL), lambda i, j: (0, 0))],
        out_specs=pl.BlockSpec((tm, D_MODEL), lambda i, j: (i, 0)),
        scratch_shapes=[pltpu.VMEM((tm, D_MODEL), F32)],
        compiler_params=pltpu.CompilerParams(
            dimension_semantics=("parallel", "arbitrary"), vmem_limit_bytes=VMEM_LIMIT),
        name="ffn_ln2",
    )(x1b, x1, w1, w2, g2, b2)


def _rope_tables(seq):
    half = ATT_DIM // 2
    inv = ROPE_THETA ** (-jnp.arange(half, dtype=F32) / half)
    ang = jnp.arange(seq, dtype=F32)[:, None] * inv[None, :]
    cos, sin = jnp.cos(ang), jnp.sin(ang)
    cos_t = jnp.concatenate([cos, cos, cos, cos], axis=1)
    sin_t = jnp.concatenate([-sin, sin, -sin, sin], axis=1)
    return cos_t, sin_t


def kernel(x, w_in, hg_lb_logits, hg_norm_gain, attn_sinks, w_branch_a, w_branch_b, w_out,
           ln1_gain, ln1_bias, w_ff1, w_ff2, ln2_gain, ln2_bias):
    b, s, d = x.shape
    assert (b, s, d) == (1, SEQ, D_MODEL) and w_in.shape == (1, D_MODEL, D_IN)
    x2 = x.reshape(s, d)
    cos_t, sin_t = _rope_tables(s)
    hq, lf, kk, hv, hg, aq, akv, ga, gb = _proj(
        x2.astype(BF16), w_in[0].astype(BF16), hg_lb_logits, cos_t, sin_t)
    oa = _hgrn(hq, lf, kk, hv, hg, hg_norm_gain)
    sinks_b = jnp.broadcast_to(attn_sinks[0][:, None], (ATT_HEADS, LANES))
    ob = _attn(aq, akv, sinks_b)
    x1, x1b = _merge(oa, ob, ga, gb, x2, w_branch_a[0].astype(BF16), w_branch_b[0].astype(BF16),
                     w_out[0].astype(BF16), ln1_gain, ln1_bias)
    out = _ffn(x1b, x1, w_ff1[0].astype(BF16), w_ff2[0].astype(BF16), ln2_gain, ln2_bias)
    return out.reshape(b, s, d)
```

```python
import functools
import math

import jax
import jax.numpy as jnp
from jax import lax
from jax.experimental import pallas as pl
from jax.experimental.pallas import tpu as pltpu

F32 = jnp.float32
BF16 = jnp.bfloat16

D_MODEL = 2048
SEQ = 8192
HG_HEADS = 8
HG_DIM = 128
HG_WIDTH = HG_HEADS * HG_DIM
ATT_HEADS = 16
ATT_KV_HEADS = 4
ATT_DIM = 64
ATT_GROUP = ATT_HEADS // ATT_KV_HEADS
ATT_WIDTH = ATT_HEADS * ATT_DIM
KV_WIDTH = ATT_KV_HEADS * ATT_DIM
WINDOW = 128
ROPE_THETA = 10000.0
D_FF = 4 * D_MODEL
D_IN = 4 * HG_WIDTH + ATT_WIDTH + 2 * KV_WIDTH + 2 * D_MODEL
ALPHA = 2.0 ** 0.25
LN_EPS = 1e-5
RMS_EPS = 1e-6
LOG2E = math.log2(math.e)

LANES = 128
HG_CHUNK = 64
HG_SUB = 8
NEG_BIG = -1e30

VMEM_LIMIT = 56 * 1024 * 1024


def _dot(a, b):
    return jnp.dot(a, b, preferred_element_type=F32)


def _dot_nt(a, b):
    return lax.dot_general(a, b, (((1,), (1,)), ((), ())), preferred_element_type=F32)


def _dot_tn(a, b):
    return lax.dot_general(a, b, (((0,), (0,)), ((), ())), preferred_element_type=F32)


PROJ_TM = 1024
PROJ_TN = 512
PROJ_RS = 256
PROJ_EDGES = tuple(c // PROJ_TN for c in (
    0, HG_WIDTH, 2 * HG_WIDTH, 3 * HG_WIDTH, 4 * HG_WIDTH, 4 * HG_WIDTH + ATT_WIDTH,
    4 * HG_WIDTH + ATT_WIDTH + 2 * KV_WIDTH, 4 * HG_WIDTH + ATT_WIDTH + 2 * KV_WIDTH + D_MODEL, D_IN))
Q_SCALE = ATT_DIM ** -0.5 * LOG2E


def _rope(t, cos, sin_signed):
    w = t.shape[1]
    reps = w // cos.shape[1]
    cos_t = jnp.concatenate([cos] * reps, axis=1) if reps > 1 else cos
    sin_t = jnp.concatenate([sin_signed] * reps, axis=1) if reps > 1 else sin_signed
    lane = lax.broadcasted_iota(jnp.int32, t.shape, 1)
    first_half = (lane % ATT_DIM) < (ATT_DIM // 2)
    rot = jnp.where(first_half, pltpu.roll(t, w - ATT_DIM // 2, 1), pltpu.roll(t, ATT_DIM // 2, 1))
    return t * cos_t + rot * sin_t


def _dup_heads(t):
    lane = lax.broadcasted_iota(jnp.int32, (t.shape[0], LANES), 1)
    low = lane < ATT_DIM
    out = []
    for g in range(ATT_KV_HEADS):
        src = t[:, LANES * (g // 2):LANES * (g // 2 + 1)]
        swapped = pltpu.roll(src, ATT_DIM, 1)
        out.append(jnp.where(low, src, swapped) if g % 2 == 0 else jnp.where(low, swapped, src))
    return jnp.concatenate(out, axis=1)


def _proj_kernel(x_ref, w_ref, lbl_ref, cos_ref, sin_ref, *refs):
    side_in = refs[:len(SIDE_CASTS)]
    (oq_ref, olf_ref, okk_ref, ov_ref, og_ref, oaq_ref, okv_ref, oga_ref, ogb_ref) = \
        refs[len(SIDE_CASTS):len(SIDE_CASTS) + 9]
    side_out = refs[len(SIDE_CASTS) + 9:-1]
    xb_ref = refs[-1]
    j = pl.program_id(1)
    e = PROJ_EDGES

    @pl.when(j == 0)
    def _():
        xb_ref[...] = x_ref[...].astype(BF16)

    def segment(s):
        def deco(epilogue):
            @pl.when((j >= e[s]) & (j < e[s + 1]))
            def _():
                for src, dst in zip(side_in, side_out):
                    dst[...] = src[...].astype(BF16)
                for r in range(PROJ_TM // PROJ_RS):
                    rows = slice(r * PROJ_RS, (r + 1) * PROJ_RS)
                    epilogue(rows, _dot(xb_ref[rows, :], w_ref[...]))
        return deco

    @segment(0)
    def _(rows, acc):
        oq_ref[rows, :] = (acc * jax.nn.sigmoid(acc)).astype(BF16)

    @segment(1)
    def _(rows, acc):
        lbl = lbl_ref[...]
        lexp = jnp.exp(lbl - jnp.max(lbl, axis=0, keepdims=True))
        lb = lexp[0:1, :] / jnp.sum(lexp, axis=0, keepdims=True)
        f = lb + (1.0 - lb) * jax.nn.sigmoid(acc)
        olf_ref[rows, :] = jnp.log(f)
        okk_ref[rows, :] = (1.0 - f).astype(BF16)

    @segment(2)
    def _(rows, acc):
        ov_ref[rows, :] = acc.astype(BF16)

    @segment(3)
    def _(rows, acc):
        og_ref[rows, :] = (acc * jax.nn.sigmoid(acc)).astype(BF16)

    @segment(4)
    def _(rows, acc):
        oaq_ref[rows, :] = (_rope(acc, cos_ref[rows, :], sin_ref[rows, :]) * Q_SCALE).astype(BF16)

    @segment(5)
    def _(rows, acc):
        k = _rope(acc[:, :KV_WIDTH], cos_ref[rows, :], sin_ref[rows, :])
        okv_ref[rows, :] = jnp.concatenate(
            [_dup_heads(k), _dup_heads(acc[:, KV_WIDTH:])], axis=1).astype(BF16)

    @segment(6)
    def _(rows, acc):
        oga_ref[rows, :] = jax.nn.sigmoid(acc).astype(BF16)

    @segment(7)
    def _(rows, acc):
        ogb_ref[rows, :] = jax.nn.sigmoid(acc).astype(BF16)


SIDE_CASTS = ((16, HG_WIDTH // 16), (16, ATT_WIDTH // 16), (16, D_MODEL // 16),
              (16, D_MODEL // 16), (D_FF // 128, 128))


def _proj(x, wb, lb_logits, cos, sin_signed, side_weights):
    tm, tn = PROJ_TM, PROJ_TN
    m, k = x.shape
    e = PROJ_EDGES
    nj = D_IN // tn
    assert e[6] - e[5] == 1 and 2 * KV_WIDTH == tn

    def seg_spec(s, width=tn):
        lo, hi = e[s], e[s + 1]
        return pl.BlockSpec((tm, width), lambda i, j: (i, jnp.clip(j - lo, 0, hi - lo - 1)))

    def sds(width, dtype):
        return jax.ShapeDtypeStruct((m, width), dtype)

    side_specs, side_shapes = [], []
    for w, (rows, steps) in zip(side_weights, SIDE_CASTS):
        assert w.shape[0] == rows * steps and steps <= (m // tm) * nj
        side_specs.append(pl.BlockSpec(
            (rows, w.shape[1]), lambda i, j, steps=steps: (jnp.minimum(i * nj + j, steps - 1), 0)))
        side_shapes.append(jax.ShapeDtypeStruct(w.shape, BF16))

    return pl.pallas_call(
        _proj_kernel,
        out_shape=(sds(HG_WIDTH, BF16), sds(HG_WIDTH, F32), sds(HG_WIDTH, BF16), sds(HG_WIDTH, BF16),
                   sds(HG_WIDTH, BF16), sds(ATT_WIDTH, BF16), sds(4 * KV_WIDTH, BF16),
                   sds(D_MODEL, BF16), sds(D_MODEL, BF16), *side_shapes),
        grid=(m // tm, nj),
        in_specs=[pl.BlockSpec((tm, k), lambda i, j: (i, 0)),
                  pl.BlockSpec((k, tn), lambda i, j: (0, j)),
                  pl.BlockSpec((lb_logits.shape[0], tn),
                               lambda i, j: (0, jnp.clip(j - e[1], 0, e[2] - e[1] - 1))),
                  pl.BlockSpec((tm, LANES), lambda i, j: (i, 0)),
                  pl.BlockSpec((tm, LANES), lambda i, j: (i, 0)),
                  *side_specs],
        out_specs=(seg_spec(0), seg_spec(1), seg_spec(1), seg_spec(2), seg_spec(3), seg_spec(4),
                   pl.BlockSpec((tm, 4 * KV_WIDTH), lambda i, j: (i, 0)),
                   seg_spec(6), seg_spec(7), *side_specs),
        scratch_shapes=[pltpu.VMEM((tm, k), BF16)],
        compiler_params=pltpu.CompilerParams(
            dimension_semantics=("arbitrary", "arbitrary"), vmem_limit_bytes=VMEM_LIMIT),
        name="proj",
    )(x, wb, lb_logits, cos, sin_signed, *side_weights)


def _hgrn_kernel(q_ref, lf_ref, kk_ref, v_ref, g_ref, gain_ref, o_ref, state_ref, c_scr, v_scr,
                 *, rows, heads):
    @pl.when(pl.program_id(1) == 0)
    def _():
        state_ref[...] = jnp.zeros_like(state_ref)

    C, SB = HG_CHUNK, HG_SUB
    ri = lax.broadcasted_iota(jnp.int32, (C, C), 0)
    ci = lax.broadcasted_iota(jnp.int32, (C, C), 1)
    tri = (ri >= ci).astype(BF16)
    row = lax.broadcasted_iota(jnp.int32, (C, 1), 0)
    sub_row = lax.broadcasted_iota(jnp.int32, (SB, 1), 0)
    levels = []
    half = C // 2
    while half >= SB:
        pair = ((ri // (2 * half)) == (ci // (2 * half))) & ((ri % (2 * half)) >= half) \
            & ((ci % (2 * half)) < half)
        levels.append((half, (row % (2 * half)) >= half, jnp.where(pair, 1.0, 0.0)))
        half //= 2
    oi = lax.broadcasted_iota(jnp.int32, (2 * HG_DIM, 2 * HG_DIM), 0)
    oj = lax.broadcasted_iota(jnp.int32, (2 * HG_DIM, 2 * HG_DIM), 1)
    ones_bd = ((oi // HG_DIM) == (oj // HG_DIM)).astype(BF16)
    n_units = (C // SB) * SB
    assert n_units * SB == 2 * 2 * HG_DIM

    def chunk(c, carry):
        r0 = pl.multiple_of(c * C, C)
        hs = range(heads)
        sls = [slice(h * HG_DIM, (h + 1) * HG_DIM) for h in hs]
        qf = [q_ref[pl.ds(r0, C), sl].astype(F32) for sl in sls]
        lf = [lf_ref[pl.ds(r0, C), sl] for sl in sls]
        kk = [kk_ref[pl.ds(r0, C), sl].astype(F32) for sl in sls]
        vb = [v_ref[pl.ds(r0, C), sl] for sl in sls]

        b = []
        for h in hs:
            hi = lf[h].astype(BF16)
            lo = (lf[h] - hi.astype(F32)).astype(BF16)
            b.append((_dot(tri, hi) + _dot(tri, lo)) * LOG2E)

        o = []
        for h in hs:
            st = state_ref[h]
            o.append(_dot_nt((qf[h] * jnp.exp2(b[h])).astype(BF16), st.astype(BF16)))
            b_last = b[h][C - 1:C, :]
            kd = kk[h] * jnp.exp2(b_last - b[h])
            state_ref[h] = st * jnp.exp2(b_last) + _dot_tn(vb[h], kd.astype(BF16))

        s_off = [None] * heads
        for lvl, second_half, pair in levels:
            for h in hs:
                ref = jnp.concatenate(
                    [jnp.broadcast_to(b[h][base + lvl:base + lvl + 1, :], (2 * lvl, HG_DIM))
                     for base in range(0, C, 2 * lvl)], axis=0)
                e = jnp.exp2(-jnp.abs(b[h] - ref))
                qs = jnp.where(second_half, qf[h] * e, 0.0)
                ks = jnp.where(second_half, 0.0, kk[h] * e)
                s = _dot_nt(qs.astype(BF16), ks.astype(BF16)) * pair
                s_off[h] = s if s_off[h] is None else s_off[h] + s
        for h in hs:
            o[h] = o[h] + _dot(s_off[h].astype(BF16), vb[h])

        nh = n_units // 2
        for h in hs:
            c_scr[h] = b[h] - jnp.log2(jnp.maximum(kk[h], 0.0))
            v_scr[h] = vb[h].astype(F32)
        r = []
        for h in hs:
            ps = []
            for i in range(C // SB):
                base = i * SB
                bi = b[h][base:base + SB, :]
                qi = qf[h][base:base + SB, :]
                for j in range(SB):
                    cj = c_scr[h, base + j:base + j + 1, :]
                    ps.append(qi * jnp.exp2(jnp.where(sub_row >= j, bi - cj, NEG_BIG)))
            p = jnp.concatenate([jnp.concatenate(ps[:nh], axis=0),
                                 jnp.concatenate(ps[nh:], axis=0)], axis=1)
            r.append(_dot(p.astype(BF16), ones_bd))

        for h in hs:
            diag = []
            for i in range(C // SB):
                acc = None
                for j in range(SB):
                    u = i * SB + j
                    ru = r[h][(u % nh) * SB:(u % nh + 1) * SB,
                              (u // nh) * HG_DIM:(u // nh + 1) * HG_DIM]
                    term = ru * v_scr[h, i * SB + j:i * SB + j + 1, :]
                    acc = term if acc is None else acc + term
                diag.append(acc)
            out = o[h] + jnp.concatenate(diag, axis=0)
            ms = jnp.mean(out * out, axis=-1, keepdims=True)
            out = out * lax.rsqrt(ms + RMS_EPS) * gain_ref[:, sls[h]]
            out = out * g_ref[pl.ds(r0, C), sls[h]].astype(F32)
            o_ref[pl.ds(r0, C), sls[h]] = out.astype(o_ref.dtype)
        return carry

    lax.fori_loop(0, rows // C, chunk, 0)


def _hgrn(q, lf, kk, v, g, gain, *, rows=256, heads=8):
    s = q.shape[0]
    blk = pl.BlockSpec((rows, heads * HG_DIM), lambda hb, t: (t, hb))
    return pl.pallas_call(
        functools.partial(_hgrn_kernel, rows=rows, heads=heads),
        out_shape=jax.ShapeDtypeStruct((s, HG_WIDTH), BF16),
        grid=(HG_HEADS // heads, s // rows),
        in_specs=[blk, blk, blk, blk, blk,
                  pl.BlockSpec((1, heads * HG_DIM), lambda hb, t: (0, hb))],
        out_specs=blk,
        scratch_shapes=[pltpu.VMEM((heads, HG_DIM, HG_DIM), F32),
                        pltpu.VMEM((heads, HG_CHUNK, HG_DIM), F32),
                        pltpu.VMEM((heads, HG_CHUNK, HG_DIM), F32)],
        compiler_params=pltpu.CompilerParams(
            dimension_semantics=("parallel", "arbitrary"), vmem_limit_bytes=VMEM_LIMIT),
        name="hgrn2",
    )(q, lf, kk, v, g, gain)


def _attn_kernel(q_ref, kvp_ref, kvc_ref, sink_ref, o_ref):
    n = pl.program_id(0)
    blk = q_ref.shape[0]
    qi = lax.broadcasted_iota(jnp.int32, (blk, 2 * blk), 0)
    ci = lax.broadcasted_iota(jnp.int32, (blk, 2 * blk), 1)
    rel = qi + blk - ci
    first_key = jnp.where(n > 0, 0, blk)
    bias1 = jnp.where(rel >= 0, jnp.where(rel < blk, jnp.where(ci >= first_key, 0.0, NEG_BIG),
                                          NEG_BIG), NEG_BIG)
    bias = jnp.concatenate([bias1] * ATT_GROUP, axis=0)
    lane = lax.broadcasted_iota(jnp.int32, (blk, LANES), 1)
    low = lane < ATT_DIM
    zero = jnp.zeros((), BF16)

    groups = range(ATT_KV_HEADS)
    pairs = range(ATT_GROUP // 2)
    s = []
    for g in groups:
        ksl = slice(g * LANES, (g + 1) * LANES)
        kd = jnp.concatenate([kvp_ref[:, ksl], kvc_ref[:, ksl]], axis=0)
        qs = []
        for pr in pairs:
            col = (g * len(pairs) + pr) * LANES
            qp = q_ref[:, col:col + LANES]
            qs += [jnp.where(low, qp, zero), jnp.where(low, zero, qp)]
        s.append(_dot_nt(jnp.concatenate(qs, axis=0), kd) + bias)

    p, sink_w = [], []
    for g in groups:
        sink = jnp.concatenate(
            [jnp.broadcast_to(sink_ref[g * ATT_GROUP + u:g * ATT_GROUP + u + 1, :], (blk, LANES))
             for u in range(ATT_GROUP)], axis=0) * LOG2E
        m = jnp.max(jnp.maximum(s[g][:, :blk], s[g][:, blk:]), axis=-1, keepdims=True)
        m = jnp.maximum(jnp.broadcast_to(m, (ATT_GROUP * blk, LANES)), sink)
        p.append(jnp.exp2(s[g] - jnp.concatenate([m, m], axis=1)).astype(BF16))
        sink_w.append(jnp.exp2(sink - m))

    ones_kl = jnp.ones((2 * blk, LANES), BF16)
    t = []
    for g in groups:
        vsl = slice(2 * KV_WIDTH + g * LANES, 2 * KV_WIDTH + (g + 1) * LANES)
        vd = jnp.concatenate([kvp_ref[:, vsl], kvc_ref[:, vsl]], axis=0)
        denom = _dot(p[g], ones_kl) + sink_w[g]
        t.append(_dot(p[g], vd) / denom)

    for g in groups:
        for pr in pairs:
            out = jnp.where(low, t[g][2 * pr * blk:(2 * pr + 1) * blk],
                            t[g][(2 * pr + 1) * blk:(2 * pr + 2) * blk])
            col = (g * len(pairs) + pr) * LANES
            o_ref[:, col:col + LANES] = out.astype(o_ref.dtype)


def _attn(aq, akv, sinks_b):
    blk = WINDOW
    s = aq.shape[0]
    return pl.pallas_call(
        _attn_kernel,
        out_shape=jax.ShapeDtypeStruct((s, ATT_WIDTH), BF16),
        grid=(s // blk,),
        in_specs=[pl.BlockSpec((blk, ATT_WIDTH), lambda n: (n, 0)),
                  pl.BlockSpec((blk, 4 * KV_WIDTH), lambda n: (jnp.maximum(n - 1, 0), 0)),
                  pl.BlockSpec((blk, 4 * KV_WIDTH), lambda n: (n, 0)),
                  pl.BlockSpec((ATT_HEADS, LANES), lambda n: (0, 0))],
        out_specs=pl.BlockSpec((blk, ATT_WIDTH), lambda n: (n, 0)),
        compiler_params=pltpu.CompilerParams(
            dimension_semantics=("parallel",), vmem_limit_bytes=VMEM_LIMIT),
        name="swa",
    )(aq, akv, akv, sinks_b)


def _layer_norm(r, gain, bias):
    mu = jnp.mean(r, axis=-1, keepdims=True)
    d = r - mu
    var = jnp.mean(d * d, axis=-1, keepdims=True)
    return d * lax.rsqrt(var + LN_EPS) * gain + bias


def _merge_kernel(oa_ref, ob_ref, ga_ref, gb_ref, x_ref, wa_ref, wb_ref, wo_ref, g_ref, b_ref,
                  x1_ref, x1b_ref):
    for r in range(MERGE_TM // MERGE_RS):
        rows = slice(r * MERGE_RS, (r + 1) * MERGE_RS)
        ya = _dot(oa_ref[rows, :], wa_ref[...])
        yb = _dot(ob_ref[rows, :], wb_ref[...])
        mixed = ga_ref[rows, :].astype(F32) * ya + gb_ref[rows, :].astype(F32) * yb
        z = _dot(mixed.astype(BF16), wo_ref[...])
        x1 = _layer_norm(ALPHA * x_ref[rows, :] + z, g_ref[...], b_ref[...])
        x1_ref[rows, :] = x1
        x1b_ref[rows, :] = x1.astype(BF16)


MERGE_TM = 256
MERGE_RS = 128


def _merge(oa, ob, ga, gb, x, wa, wb, wo, g1, b1):
    tm = MERGE_TM
    s = x.shape[0]
    row = lambda i: (i, 0)
    const = lambda i: (0, 0)
    resident = dict(pipeline_mode=pl.Buffered(1))
    return pl.pallas_call(
        _merge_kernel,
        out_shape=(jax.ShapeDtypeStruct((s, D_MODEL), F32), jax.ShapeDtypeStruct((s, D_MODEL), BF16)),
        grid=(s // tm,),
        in_specs=[pl.BlockSpec((tm, HG_WIDTH), row), pl.BlockSpec((tm, ATT_WIDTH), row),
                  pl.BlockSpec((tm, D_MODEL), row), pl.BlockSpec((tm, D_MODEL), row),
                  pl.BlockSpec((tm, D_MODEL), row),
                  pl.BlockSpec((HG_WIDTH, D_MODEL), const, **resident),
                  pl.BlockSpec((ATT_WIDTH, D_MODEL), const, **resident),
                  pl.BlockSpec((D_MODEL, D_MODEL), const, **resident),
                  pl.BlockSpec((1, D_MODEL), const), pl.BlockSpec((1, D_MODEL), const)],
        out_specs=(pl.BlockSpec((tm, D_MODEL), row), pl.BlockSpec((tm, D_MODEL), row)),
        compiler_params=pltpu.CompilerParams(
            dimension_semantics=("parallel",), vmem_limit_bytes=VMEM_LIMIT),
        name="merge_ln1",
    )(oa, ob, ga, gb, x, wa, wb, wo, g1, b1)


def _ffn_kernel(xb_ref, x_ref, w1_ref, w2_ref, g_ref, b_ref, o_ref, acc_ref):
    j = pl.program_id(1)

    @pl.when(j == 0)
    def _():
        acc_ref[...] = jnp.zeros_like(acc_ref)

    h = jnp.maximum(_dot(xb_ref[...], w1_ref[...]), 0.0)
    acc_ref[...] += _dot((h * h).astype(BF16), w2_ref[...])

    @pl.when(j == pl.num_programs(1) - 1)
    def _():
        o_ref[...] = _layer_norm(ALPHA * x_ref[...] + acc_ref[...], g_ref[...], b_ref[...])


def _ffn(x1b, x1, w1, w2, g2, b2):
    tm, tf = 512, 512
    s = x1.shape[0]
    return pl.pallas_call(
        _ffn_kernel,
        out_shape=jax.ShapeDtypeStruct((s, D_MODEL), F32),
        grid=(s // tm, D_FF // tf),
        in_specs=[pl.BlockSpec((tm, D_MODEL), lambda i, j: (i, 0)),
                  pl.BlockSpec((tm, D_MODEL), lambda i, j: (i, 0)),
                  pl.BlockSpec((D_MODEL, tf), lambda i, j: (0, j)),
                  pl.BlockSpec((tf, D_MODEL), lambda i, j: (j, 0)),
                  pl.BlockSpec((1, D_MODEL), lambda i, j: (0, 0)),
                  pl.BlockSpec((1, D_MODEL), lambda i, j: (0, 0))],
        out_specs=pl.BlockSpec((tm, D_MODEL), lambda i, j: (i, 0)),
        scratch_shapes=[pltpu.VMEM((tm, D_MODEL), F32)],
        compiler_params=pltpu.CompilerParams(
            dimension_semantics=("parallel", "arbitrary"), vmem_limit_bytes=VMEM_LIMIT),
        name="ffn_ln2",
    )(x1b, x1, w1, w2, g2, b2)


def _rope_tables(seq):
    half = ATT_DIM // 2
    inv = ROPE_THETA ** (-jnp.arange(half, dtype=F32) / half)
    ang = jnp.arange(seq, dtype=F32)[:, None] * inv[None, :]
    cos, sin = jnp.cos(ang), jnp.sin(ang)
    cos_t = jnp.concatenate([cos, cos, cos, cos], axis=1)
    sin_t = jnp.concatenate([-sin, sin, -sin, sin], axis=1)
    return cos_t, sin_t


def kernel(x, w_in, hg_lb_logits, hg_norm_gain, attn_sinks, w_branch_a, w_branch_b, w_out,
           ln1_gain, ln1_bias, w_ff1, w_ff2, ln2_gain, ln2_bias):
    b, s, d = x.shape
    assert (b, s, d) == (1, SEQ, D_MODEL) and w_in.shape == (1, D_MODEL, D_IN)
    x2 = x.reshape(s, d)
    cos_t, sin_t = _rope_tables(s)
    hq, lf, kk, hv, hg, aq, akv, ga, gb, wa, wb, wo, w1, w2 = _proj(
        x2, w_in[0].astype(BF16), hg_lb_logits, cos_t, sin_t,
        (w_branch_a[0], w_branch_b[0], w_out[0], w_ff1[0], w_ff2[0]))
    oa = _hgrn(hq, lf, kk, hv, hg, hg_norm_gain)
    sinks_b = jnp.broadcast_to(attn_sinks[0][:, None], (ATT_HEADS, LANES))
    ob = _attn(aq, akv, sinks_b)
    x1, x1b = _merge(oa, ob, ga, gb, x2, wa, wb, wo, ln1_gain, ln1_bias)
    out = _ffn(x1b, x1, w1, w2, ln2_gain, ln2_bias)
    return out.reshape(b, s, d)
```

```python
import functools
import math

import jax
import jax.numpy as jnp
from jax import lax
from jax.experimental import pallas as pl
from jax.experimental.pallas import tpu as pltpu

F32 = jnp.float32
BF16 = jnp.bfloat16

D_MODEL = 2048
SEQ = 8192
HG_HEADS = 8
HG_DIM = 128
HG_WIDTH = HG_HEADS * HG_DIM
ATT_HEADS = 16
ATT_KV_HEADS = 4
ATT_DIM = 64
ATT_GROUP = ATT_HEADS // ATT_KV_HEADS
ATT_WIDTH = ATT_HEADS * ATT_DIM
KV_WIDTH = ATT_KV_HEADS * ATT_DIM
WINDOW = 128
ROPE_THETA = 10000.0
D_FF = 4 * D_MODEL
D_IN = 4 * HG_WIDTH + ATT_WIDTH + 2 * KV_WIDTH + 2 * D_MODEL
ALPHA = 2.0 ** 0.25
LN_EPS = 1e-5
RMS_EPS = 1e-6
LOG2E = math.log2(math.e)

LANES = 128
HG_CHUNK = 64
HG_SUB = 8
NEG_BIG = -1e30

VMEM_LIMIT = 56 * 1024 * 1024


def _dot(a, b):
    return jnp.dot(a, b, preferred_element_type=F32)


def _dot_nt(a, b):
    return lax.dot_general(a, b, (((1,), (1,)), ((), ())), preferred_element_type=F32)


def _dot_tn(a, b):
    return lax.dot_general(a, b, (((0,), (0,)), ((), ())), preferred_element_type=F32)


PROJ_TM = 1024
PROJ_TN = 512
PROJ_RS = 128
PROJ_EDGES = tuple(c // PROJ_TN for c in (
    0, HG_WIDTH, 2 * HG_WIDTH, 3 * HG_WIDTH, 4 * HG_WIDTH, 4 * HG_WIDTH + ATT_WIDTH,
    4 * HG_WIDTH + ATT_WIDTH + 2 * KV_WIDTH, 4 * HG_WIDTH + ATT_WIDTH + 2 * KV_WIDTH + D_MODEL, D_IN))
Q_SCALE = ATT_DIM ** -0.5 * LOG2E


def _rope(t, cos, sin_signed):
    w = t.shape[1]
    reps = w // cos.shape[1]
    cos_t = jnp.concatenate([cos] * reps, axis=1) if reps > 1 else cos
    sin_t = jnp.concatenate([sin_signed] * reps, axis=1) if reps > 1 else sin_signed
    lane = lax.broadcasted_iota(jnp.int32, t.shape, 1)
    first_half = (lane % ATT_DIM) < (ATT_DIM // 2)
    rot = jnp.where(first_half, pltpu.roll(t, w - ATT_DIM // 2, 1), pltpu.roll(t, ATT_DIM // 2, 1))
    return t * cos_t + rot * sin_t


def _dup_heads(t):
    lane = lax.broadcasted_iota(jnp.int32, (t.shape[0], LANES), 1)
    low = lane < ATT_DIM
    out = []
    for g in range(ATT_KV_HEADS):
        src = t[:, LANES * (g // 2):LANES * (g // 2 + 1)]
        swapped = pltpu.roll(src, ATT_DIM, 1)
        out.append(jnp.where(low, src, swapped) if g % 2 == 0 else jnp.where(low, swapped, src))
    return jnp.concatenate(out, axis=1)


def _proj_kernel(x_ref, w_ref, lbl_ref, cos_ref, sin_ref,
                 oh_ref, olf_ref, oaq_ref, okv_ref, og_ref, xb_ref):
    j = pl.program_id(1)
    e = PROJ_EDGES

    @pl.when(j == 0)
    def _():
        xb_ref[...] = x_ref[...].astype(BF16)

    def segment(s):
        def deco(epilogue):
            @pl.when((j >= e[s]) & (j < e[s + 1]))
            def _():
                for r in range(PROJ_TM // PROJ_RS):
                    rows = slice(r * PROJ_RS, (r + 1) * PROJ_RS)
                    epilogue(rows, _dot(xb_ref[rows, :], w_ref[...]))
        return deco

    @segment(0)
    def _(rows, acc):
        oh_ref[rows, :] = (acc * jax.nn.sigmoid(acc)).astype(BF16)

    @segment(1)
    def _(rows, acc):
        lbl = lbl_ref[...]
        lexp = jnp.exp(lbl - jnp.max(lbl, axis=0, keepdims=True))
        lb = lexp[0:1, :] / jnp.sum(lexp, axis=0, keepdims=True)
        f = lb + (1.0 - lb) * jax.nn.sigmoid(acc)
        olf_ref[rows, :] = jnp.log(f)
        oh_ref[rows, :] = (1.0 - f).astype(BF16)

    @segment(2)
    def _(rows, acc):
        oh_ref[rows, :] = acc.astype(BF16)

    @segment(3)
    def _(rows, acc):
        oh_ref[rows, :] = (acc * jax.nn.sigmoid(acc)).astype(BF16)

    @segment(4)
    def _(rows, acc):
        oaq_ref[rows, :] = (_rope(acc, cos_ref[rows, :], sin_ref[rows, :]) * Q_SCALE).astype(BF16)

    @segment(5)
    def _(rows, acc):
        k = _rope(acc[:, :KV_WIDTH], cos_ref[rows, :], sin_ref[rows, :])
        okv_ref[rows, :] = jnp.concatenate(
            [_dup_heads(k), _dup_heads(acc[:, KV_WIDTH:])], axis=1).astype(BF16)

    @segment(6)
    def _(rows, acc):
        og_ref[rows, :] = jax.nn.sigmoid(acc).astype(BF16)

    @segment(7)
    def _(rows, acc):
        og_ref[rows, :] = jax.nn.sigmoid(acc).astype(BF16)


def _proj(x, wb, lb_logits, cos, sin_signed):
    tm, tn = PROJ_TM, PROJ_TN
    m, k = x.shape
    e = PROJ_EDGES
    assert e[6] - e[5] == 1 and 2 * KV_WIDTH == tn

    def span_spec(first, last, width=tn):
        lo, hi = e[first], e[last + 1]
        return pl.BlockSpec((tm, width), lambda i, j: (i, jnp.clip(j - lo, 0, hi - lo - 1)))

    def sds(width, dtype):
        return jax.ShapeDtypeStruct((m, width), dtype)

    return pl.pallas_call(
        _proj_kernel,
        out_shape=(sds(4 * HG_WIDTH, BF16), sds(HG_WIDTH, F32), sds(ATT_WIDTH, BF16),
                   sds(4 * KV_WIDTH, BF16), sds(2 * D_MODEL, BF16)),
        grid=(m // tm, D_IN // tn),
        in_specs=[pl.BlockSpec((tm, k), lambda i, j: (i, 0)),
                  pl.BlockSpec((k, tn), lambda i, j: (0, j)),
                  pl.BlockSpec((lb_logits.shape[0], tn),
                               lambda i, j: (0, jnp.clip(j - e[1], 0, e[2] - e[1] - 1))),
                  pl.BlockSpec((tm, LANES), lambda i, j: (i, 0)),
                  pl.BlockSpec((tm, LANES), lambda i, j: (i, 0))],
        out_specs=(span_spec(0, 3), span_spec(1, 1), span_spec(4, 4),
                   pl.BlockSpec((tm, 4 * KV_WIDTH), lambda i, j: (i, 0)),
                   span_spec(6, 7)),
        scratch_shapes=[pltpu.VMEM((tm, k), BF16)],
        compiler_params=pltpu.CompilerParams(
            dimension_semantics=("arbitrary", "arbitrary"), vmem_limit_bytes=VMEM_LIMIT),
        name="proj",
    )(x, wb, lb_logits, cos, sin_signed)


def _hgrn_kernel(q_ref, kk_ref, v_ref, g_ref, lf_ref, gain_ref, *refs, rows, heads):
    n_side = len(SIDE_CASTS)
    side_in, o_ref, side_out = refs[:n_side], refs[n_side], refs[n_side + 1:2 * n_side + 1]
    state_ref, c_scr, v_scr = refs[2 * n_side + 1:]

    @pl.when(pl.program_id(1) == 0)
    def _():
        state_ref[...] = jnp.zeros_like(state_ref)

    C, SB = HG_CHUNK, HG_SUB
    ri = lax.broadcasted_iota(jnp.int32, (C, C), 0)
    ci = lax.broadcasted_iota(jnp.int32, (C, C), 1)
    tri = (ri >= ci).astype(BF16)
    row = lax.broadcasted_iota(jnp.int32, (C, 1), 0)
    sub_row = lax.broadcasted_iota(jnp.int32, (SB, 1), 0)
    levels = []
    half = C // 2
    while half >= SB:
        pair = ((ri // (2 * half)) == (ci // (2 * half))) & ((ri % (2 * half)) >= half) \
            & ((ci % (2 * half)) < half)
        levels.append((half, (row % (2 * half)) >= half, jnp.where(pair, 1.0, 0.0)))
        half //= 2
    oi = lax.broadcasted_iota(jnp.int32, (2 * HG_DIM, 2 * HG_DIM), 0)
    oj = lax.broadcasted_iota(jnp.int32, (2 * HG_DIM, 2 * HG_DIM), 1)
    ones_bd = ((oi // HG_DIM) == (oj // HG_DIM)).astype(BF16)
    n_units = (C // SB) * SB
    assert n_units * SB == 2 * 2 * HG_DIM

    def chunk(c, carry):
        r0 = pl.multiple_of(c * C, C)
        for src, dst in zip(side_in, side_out):
            slab = src.shape[0] // (rows // C)
            s0 = pl.multiple_of(c * slab, slab)
            dst[pl.ds(s0, slab), :] = src[pl.ds(s0, slab), :].astype(BF16)
        hs = range(heads)
        sls = [slice(h * HG_DIM, (h + 1) * HG_DIM) for h in hs]
        qf = [q_ref[pl.ds(r0, C), sl].astype(F32) for sl in sls]
        lf = [lf_ref[pl.ds(r0, C), sl] for sl in sls]
        kk = [kk_ref[pl.ds(r0, C), sl].astype(F32) for sl in sls]
        vb = [v_ref[pl.ds(r0, C), sl] for sl in sls]

        b = []
        for h in hs:
            hi = lf[h].astype(BF16)
            lo = (lf[h] - hi.astype(F32)).astype(BF16)
            b.append((_dot(tri, hi) + _dot(tri, lo)) * LOG2E)

        o = []
        for h in hs:
            st = state_ref[h]
            o.append(_dot_nt((qf[h] * jnp.exp2(b[h])).astype(BF16), st.astype(BF16)))
            b_last = b[h][C - 1:C, :]
            kd = kk[h] * jnp.exp2(b_last - b[h])
            state_ref[h] = st * jnp.exp2(b_last) + _dot_tn(vb[h], kd.astype(BF16))

        s_off = [None] * heads
        for lvl, second_half, pair in levels:
            for h in hs:
                ref = jnp.concatenate(
                    [jnp.broadcast_to(b[h][base + lvl:base + lvl + 1, :], (2 * lvl, HG_DIM))
                     for base in range(0, C, 2 * lvl)], axis=0)
                e = jnp.exp2(-jnp.abs(b[h] - ref))
                qs = jnp.where(second_half, qf[h] * e, 0.0)
                ks = jnp.where(second_half, 0.0, kk[h] * e)
                s = _dot_nt(qs.astype(BF16), ks.astype(BF16)) * pair
                s_off[h] = s if s_off[h] is None else s_off[h] + s
        for h in hs:
            o[h] = o[h] + _dot(s_off[h].astype(BF16), vb[h])

        nh = n_units // 2
        for h in hs:
            c_scr[h] = b[h] - jnp.log2(jnp.maximum(kk[h], 0.0))
            v_scr[h] = vb[h].astype(F32)
        r = []
        for h in hs:
            ps = []
            for i in range(C // SB):
                base = i * SB
                bi = b[h][base:base + SB, :]
                qi = qf[h][base:base + SB, :]
                for j in range(SB):
                    cj = c_scr[h, base + j:base + j + 1, :]
                    ps.append(qi * jnp.exp2(jnp.where(sub_row >= j, bi - cj, NEG_BIG)))
            p = jnp.concatenate([jnp.concatenate(ps[:nh], axis=0),
                                 jnp.concatenate(ps[nh:], axis=0)], axis=1)
            r.append(_dot(p.astype(BF16), ones_bd))

        for h in hs:
            diag = []
            for i in range(C // SB):
                acc = None
                for j in range(SB):
                    u = i * SB + j
                    ru = r[h][(u % nh) * SB:(u % nh + 1) * SB,
                              (u // nh) * HG_DIM:(u // nh + 1) * HG_DIM]
                    term = ru * v_scr[h, i * SB + j:i * SB + j + 1, :]
                    acc = term if acc is None else acc + term
                diag.append(acc)
            out = o[h] + jnp.concatenate(diag, axis=0)
            ms = jnp.mean(out * out, axis=-1, keepdims=True)
            out = out * lax.rsqrt(ms + RMS_EPS) * gain_ref[:, sls[h]]
            out = out * g_ref[pl.ds(r0, C), sls[h]].astype(F32)
            o_ref[pl.ds(r0, C), sls[h]] = out.astype(o_ref.dtype)
        return carry

    lax.fori_loop(0, rows // C, chunk, 0)


SIDE_CASTS = (16, 16, 32, 32, 32)


def _hgrn(packed, lf, gain, side_weights, *, rows=256, heads=HG_HEADS):
    s = lf.shape[0]
    nhb, nt = HG_HEADS // heads, s // rows
    assert nhb == 1

    def section(k):
        return pl.BlockSpec((rows, heads * HG_DIM), lambda hb, t: (t, k * nhb + hb))

    side_specs, side_shapes = [], []
    for w, steps in zip(side_weights, SIDE_CASTS):
        slab = w.shape[0] // steps
        assert slab * steps == w.shape[0] and steps <= nt and slab % (16 * (rows // HG_CHUNK)) == 0
        side_specs.append(pl.BlockSpec(
            (slab, w.shape[1]), lambda hb, t, steps=steps: (jnp.minimum(t, steps - 1), 0)))
        side_shapes.append(jax.ShapeDtypeStruct(w.shape, BF16))

    outs = pl.pallas_call(
        functools.partial(_hgrn_kernel, rows=rows, heads=heads),
        out_shape=(jax.ShapeDtypeStruct((s, HG_WIDTH), BF16), *side_shapes),
        grid=(nhb, nt),
        in_specs=[section(0), section(1), section(2), section(3), section(0),
                  pl.BlockSpec((1, heads * HG_DIM), lambda hb, t: (0, hb)), *side_specs],
        out_specs=(section(0), *side_specs),
        scratch_shapes=[pltpu.VMEM((heads, HG_DIM, HG_DIM), F32),
                        pltpu.VMEM((heads, HG_CHUNK, HG_DIM), F32),
                        pltpu.VMEM((heads, HG_CHUNK, HG_DIM), F32)],
        compiler_params=pltpu.CompilerParams(
            dimension_semantics=("arbitrary", "arbitrary"), vmem_limit_bytes=VMEM_LIMIT),
        name="hgrn2",
    )(packed, packed, packed, packed, lf, gain, *side_weights)
    return outs[0], outs[1:]


def _attn_kernel(q_ref, kvp_ref, kvc_ref, sink_ref, o_ref):
    n = pl.program_id(0)
    blk = q_ref.shape[0]
    qi = lax.broadcasted_iota(jnp.int32, (blk, 2 * blk), 0)
    ci = lax.broadcasted_iota(jnp.int32, (blk, 2 * blk), 1)
    rel = qi + blk - ci
    first_key = jnp.where(n > 0, 0, blk)
    bias1 = jnp.where(rel >= 0, jnp.where(rel < blk, jnp.where(ci >= first_key, 0.0, NEG_BIG),
                                          NEG_BIG), NEG_BIG)
    bias = jnp.concatenate([bias1] * ATT_GROUP, axis=0)
    lane = lax.broadcasted_iota(jnp.int32, (blk, LANES), 1)
    low = lane < ATT_DIM
    zero = jnp.zeros((), BF16)

    groups = range(ATT_KV_HEADS)
    pairs = range(ATT_GROUP // 2)
    s = []
    for g in groups:
        ksl = slice(g * LANES, (g + 1) * LANES)
        kd = jnp.concatenate([kvp_ref[:, ksl], kvc_ref[:, ksl]], axis=0)
        qs = []
        for pr in pairs:
            col = (g * len(pairs) + pr) * LANES
            qp = q_ref[:, col:col + LANES]
            qs += [jnp.where(low, qp, zero), jnp.where(low, zero, qp)]
        s.append(_dot_nt(jnp.concatenate(qs, axis=0), kd) + bias)

    p, sink_w = [], []
    for g in groups:
        sink = jnp.concatenate(
            [jnp.broadcast_to(sink_ref[g * ATT_GROUP + u:g * ATT_GROUP + u + 1, :], (blk, LANES))
             for u in range(ATT_GROUP)], axis=0) * LOG2E
        m = jnp.max(jnp.maximum(s[g][:, :blk], s[g][:, blk:]), axis=-1, keepdims=True)
        m = jnp.maximum(jnp.broadcast_to(m, (ATT_GROUP * blk, LANES)), sink)
        p.append(jnp.exp2(s[g] - jnp.concatenate([m, m], axis=1)).astype(BF16))
        sink_w.append(jnp.exp2(sink - m))

    ones_kl = jnp.ones((2 * blk, LANES), BF16)
    t = []
    for g in groups:
        vsl = slice(2 * KV_WIDTH + g * LANES, 2 * KV_WIDTH + (g + 1) * LANES)
        vd = jnp.concatenate([kvp_ref[:, vsl], kvc_ref[:, vsl]], axis=0)
        denom = _dot(p[g], ones_kl) + sink_w[g]
        t.append(_dot(p[g], vd) / denom)

    for g in groups:
        for pr in pairs:
            out = jnp.where(low, t[g][2 * pr * blk:(2 * pr + 1) * blk],
                            t[g][(2 * pr + 1) * blk:(2 * pr + 2) * blk])
            col = (g * len(pairs) + pr) * LANES
            o_ref[:, col:col + LANES] = out.astype(o_ref.dtype)


def _attn(aq, akv, sinks_b):
    blk = WINDOW
    s = aq.shape[0]
    return pl.pallas_call(
        _attn_kernel,
        out_shape=jax.ShapeDtypeStruct((s, ATT_WIDTH), BF16),
        grid=(s // blk,),
        in_specs=[pl.BlockSpec((blk, ATT_WIDTH), lambda n: (n, 0)),
                  pl.BlockSpec((blk, 4 * KV_WIDTH), lambda n: (jnp.maximum(n - 1, 0), 0)),
                  pl.BlockSpec((blk, 4 * KV_WIDTH), lambda n: (n, 0)),
                  pl.BlockSpec((ATT_HEADS, LANES), lambda n: (0, 0))],
        out_specs=pl.BlockSpec((blk, ATT_WIDTH), lambda n: (n, 0)),
        compiler_params=pltpu.CompilerParams(
            dimension_semantics=("parallel",), vmem_limit_bytes=VMEM_LIMIT),
        name="swa",
    )(aq, akv, akv, sinks_b)


def _layer_norm(r, gain, bias):
    mu = jnp.mean(r, axis=-1, keepdims=True)
    d = r - mu
    var = jnp.mean(d * d, axis=-1, keepdims=True)
    return d * lax.rsqrt(var + LN_EPS) * gain + bias


def _merge_kernel(oa_ref, ob_ref, ga_ref, gb_ref, x_ref, wa_ref, wb_ref, wo_ref, g_ref, b_ref,
                  x1_ref, x1b_ref):
    for r in range(MERGE_TM // MERGE_RS):
        rows = slice(r * MERGE_RS, (r + 1) * MERGE_RS)
        ya = _dot(oa_ref[rows, :], wa_ref[...])
        yb = _dot(ob_ref[rows, :], wb_ref[...])
        mixed = ga_ref[rows, :].astype(F32) * ya + gb_ref[rows, :].astype(F32) * yb
        z = _dot(mixed.astype(BF16), wo_ref[...])
        x1 = _layer_norm(ALPHA * x_ref[rows, :] + z, g_ref[...], b_ref[...])
        x1_ref[rows, :] = x1
        x1b_ref[rows, :] = x1.astype(BF16)


MERGE_TM = 256
MERGE_RS = 128


def _merge(oa, ob, gates, x, wa, wb, wo, g1, b1):
    tm = MERGE_TM
    s = x.shape[0]
    row = lambda i: (i, 0)
    const = lambda i: (0, 0)
    resident = dict(pipeline_mode=pl.Buffered(1))
    return pl.pallas_call(
        _merge_kernel,
        out_shape=(jax.ShapeDtypeStruct((s, D_MODEL), F32), jax.ShapeDtypeStruct((s, D_MODEL), BF16)),
        grid=(s // tm,),
        in_specs=[pl.BlockSpec((tm, HG_WIDTH), row), pl.BlockSpec((tm, ATT_WIDTH), row),
                  pl.BlockSpec((tm, D_MODEL), row), pl.BlockSpec((tm, D_MODEL), lambda i: (i, 1)),
                  pl.BlockSpec((tm, D_MODEL), row),
                  pl.BlockSpec((HG_WIDTH, D_MODEL), const, **resident),
                  pl.BlockSpec((ATT_WIDTH, D_MODEL), const, **resident),
                  pl.BlockSpec((D_MODEL, D_MODEL), const, **resident),
                  pl.BlockSpec((1, D_MODEL), const), pl.BlockSpec((1, D_MODEL), const)],
        out_specs=(pl.BlockSpec((tm, D_MODEL), row), pl.BlockSpec((tm, D_MODEL), row)),
        compiler_params=pltpu.CompilerParams(
            dimension_semantics=("parallel",), vmem_limit_bytes=VMEM_LIMIT),
        name="merge_ln1",
    )(oa, ob, gates, gates, x, wa, wb, wo, g1, b1)


def _ffn_kernel(xb_ref, x_ref, w1_ref, w2_ref, g_ref, b_ref, o_ref, acc_ref):
    j = pl.program_id(1)

    @pl.when(j == 0)
    def _():
        acc_ref[...] = jnp.zeros_like(acc_ref)

    h = jnp.maximum(_dot(xb_ref[...], w1_ref[...]), 0.0)
    acc_ref[...] += _dot((h * h).astype(BF16), w2_ref[...])

    @pl.when(j == pl.num_programs(1) - 1)
    def _():
        o_ref[...] = _layer_norm(ALPHA * x_ref[...] + acc_ref[...], g_ref[...], b_ref[...])


def _ffn(x1b, x1, w1, w2, g2, b2):
    tm, tf = 512, 512
    s = x1.shape[0]
    return pl.pallas_call(
        _ffn_kernel,
        out_shape=jax.ShapeDtypeStruct((s, D_MODEL), F32),
        grid=(s // tm, D_FF // tf),
        in_specs=[pl.BlockSpec((tm, D_MODEL), lambda i, j: (i, 0)),
                  pl.BlockSpec((tm, D_MODEL), lambda i, j: (i, 0)),
                  pl.BlockSpec((D_MODEL, tf), lambda i, j: (0, j)),
                  pl.BlockSpec((tf, D_MODEL), lambda i, j: (j, 0)),
                  pl.BlockSpec((1, D_MODEL), lambda i, j: (0, 0)),
                  pl.BlockSpec((1, D_MODEL), lambda i, j: (0, 0))],
        out_specs=pl.BlockSpec((tm, D_MODEL), lambda i, j: (i, 0)),
        scratch_shapes=[pltpu.VMEM((tm, D_MODEL), F32)],
        compiler_params=pltpu.CompilerParams(
            dimension_semantics=("parallel", "arbitrary"), vmem_limit_bytes=VMEM_LIMIT),
        name="ffn_ln2",
    )(x1b, x1, w1, w2, g2, b2)


def _rope_tables(seq):
    half = ATT_DIM // 2
    inv = ROPE_THETA ** (-jnp.arange(half, dtype=F32) / half)
    ang = jnp.arange(seq, dtype=F32)[:, None] * inv[None, :]
    cos, sin = jnp.cos(ang), jnp.sin(ang)
    cos_t = jnp.concatenate([cos, cos, cos, cos], axis=1)
    sin_t = jnp.concatenate([-sin, sin, -sin, sin], axis=1)
    return cos_t, sin_t


def kernel(x, w_in, hg_lb_logits, hg_norm_gain, attn_sinks, w_branch_a, w_branch_b, w_out,
           ln1_gain, ln1_bias, w_ff1, w_ff2, ln2_gain, ln2_bias):
    b, s, d = x.shape
    assert (b, s, d) == (1, SEQ, D_MODEL) and w_in.shape == (1, D_MODEL, D_IN)
    x2 = x.reshape(s, d)
    cos_t, sin_t = _rope_tables(s)
    hpack, lf, aq, akv, gates = _proj(x2, w_in[0].astype(BF16), hg_lb_logits, cos_t, sin_t)
    oa, (wa, wb, wo, w1, w2) = _hgrn(
        hpack, lf, hg_norm_gain, (w_branch_a[0], w_branch_b[0], w_out[0], w_ff1[0], w_ff2[0]))
    sinks_b = jnp.broadcast_to(attn_sinks[0][:, None], (ATT_HEADS, LANES))
    ob = _attn(aq, akv, sinks_b)
    x1, x1b = _merge(oa, ob, gates, x2, wa, wb, wo, ln1_gain, ln1_bias)
    out = _ffn(x1b, x1, w1, w2, ln2_gain, ln2_bias)
    return out.reshape(b, s, d)
```

```python
import math

import jax
import jax.numpy as jnp
import numpy as np
from jax import lax
from jax.experimental import pallas as pl
from jax.experimental.pallas import tpu as pltpu

F32 = jnp.float32
BF16 = jnp.bfloat16

D_MODEL = 2048
SEQ = 8192
HG_HEADS = 8
HG_DIM = 128
HG_WIDTH = HG_HEADS * HG_DIM
ATT_HEADS = 16
ATT_KV_HEADS = 4
ATT_DIM = 64
ATT_GROUP = ATT_HEADS // ATT_KV_HEADS
ATT_WIDTH = ATT_HEADS * ATT_DIM
KV_WIDTH = ATT_KV_HEADS * ATT_DIM
WINDOW = 128
ROPE_THETA = 10000.0
D_FF = 4 * D_MODEL
D_IN = 4 * HG_WIDTH + ATT_WIDTH + 2 * KV_WIDTH + 2 * D_MODEL
ALPHA = 2.0 ** 0.25
LN_EPS = 1e-5
RMS_EPS = 1e-6
LOG2E = math.log2(math.e)

LANES = 128
HG_CHUNK = 64
HG_SUB = 8
NEG_BIG = -1e30

VMEM_LIMIT = 56 * 1024 * 1024
PROJ_VMEM_LIMIT = 60 * 1024 * 1024


def _dot(a, b):
    return jnp.dot(a, b, preferred_element_type=F32)


def _dot_nt(a, b):
    return lax.dot_general(a, b, (((1,), (1,)), ((), ())), preferred_element_type=F32)


def _dot_tn(a, b):
    return lax.dot_general(a, b, (((0,), (0,)), ((), ())), preferred_element_type=F32)


def _hgrn_consts():
    C, SB = HG_CHUNK, HG_SUB
    ri = lax.broadcasted_iota(jnp.int32, (C, C), 0)
    ci = lax.broadcasted_iota(jnp.int32, (C, C), 1)
    tri = (ri >= ci).astype(BF16)
    row = lax.broadcasted_iota(jnp.int32, (C, 1), 0)
    sub_row = lax.broadcasted_iota(jnp.int32, (SB, 1), 0)
    levels = []
    half = C // 2
    while half >= SB:
        pair = ((ri // (2 * half)) == (ci // (2 * half))) & ((ri % (2 * half)) >= half) \
            & ((ci % (2 * half)) < half)
        levels.append((half, (row % (2 * half)) >= half, jnp.where(pair, 1.0, 0.0)))
        half //= 2
    oi = lax.broadcasted_iota(jnp.int32, (2 * HG_DIM, 2 * HG_DIM), 0)
    oj = lax.broadcasted_iota(jnp.int32, (2 * HG_DIM, 2 * HG_DIM), 1)
    ones_bd = ((oi // HG_DIM) == (oj // HG_DIM)).astype(BF16)
    return tri, sub_row, levels, ones_bd


class _HgrnChunk:
    WEIGHTS = dict(cumsum=8, diag=95, state=20, level=15, off_values=3, finish=40)
    TOTAL_WEIGHT = HG_HEADS * (8 + 95 + 20 + 3 * 15 + 3 + 40)

    def __init__(self, consts, load, gain_ref, o_ref, r0, state_ref, c_scr, v_scr):
        self.tri, self.sub_row, self.levels, self.ones_bd = consts
        assert len(self.levels) == 3
        self.load, self.gain_ref, self.o_ref, self.r0 = load, gain_ref, o_ref, r0
        self.state_ref, self.c_scr, self.v_scr = state_ref, c_scr, v_scr
        hs = range(HG_HEADS)
        self.qf = [load('q', h).astype(F32) for h in hs]
        self.kk = [load('k', h).astype(F32) for h in hs]
        self.vb = [load('v', h) for h in hs]
        self.b, self.o, self.r = [], [], []
        self.s_off = [None] * HG_HEADS

    def cumsum(self):
        for h in range(HG_HEADS):
            lf = self.load('lf', h)
            hi = lf.astype(BF16)
            lo = (lf - hi.astype(F32)).astype(BF16)
            self.b.append((_dot(self.tri, hi) + _dot(self.tri, lo)) * LOG2E)
            yield self.WEIGHTS['cumsum']

    def state(self):
        C = HG_CHUNK
        for h in range(HG_HEADS):
            b, st = self.b[h], self.state_ref[h]
            self.o.append(_dot_nt((self.qf[h] * jnp.exp2(b)).astype(BF16), st.astype(BF16)))
            b_last = b[C - 1:C, :]
            kd = self.kk[h] * jnp.exp2(b_last - b)
            self.state_ref[h] = st * jnp.exp2(b_last) + _dot_tn(self.vb[h], kd.astype(BF16))
            yield self.WEIGHTS['state']

    def level(self, k):
        lvl, second_half, pair = self.levels[k]
        for h in range(HG_HEADS):
            b = self.b[h]
            ref = jnp.concatenate(
                [jnp.broadcast_to(b[base + lvl:base + lvl + 1, :], (2 * lvl, HG_DIM))
                 for base in range(0, HG_CHUNK, 2 * lvl)], axis=0)
            e = jnp.exp2(-jnp.abs(b - ref))
            qs = jnp.where(second_half, self.qf[h] * e, 0.0)
            ks = jnp.where(second_half, 0.0, self.kk[h] * e)
            s = _dot_nt(qs.astype(BF16), ks.astype(BF16)) * pair
            self.s_off[h] = s if self.s_off[h] is None else self.s_off[h] + s
            yield self.WEIGHTS['level']

    def off_values(self):
        for h in range(HG_HEADS):
            self.o[h] = self.o[h] + _dot(self.s_off[h].astype(BF16), self.vb[h])
            yield self.WEIGHTS['off_values']

    def diag(self):
        C, SB = HG_CHUNK, HG_SUB
        n_units = (C // SB) * SB
        assert n_units * SB == 2 * 2 * HG_DIM
        nh = n_units // 2
        for h in range(HG_HEADS):
            b = self.b[h]
            self.c_scr[h] = b - jnp.log2(jnp.maximum(self.kk[h], 0.0))
            self.v_scr[h] = self.vb[h].astype(F32)
            ps = []
            for i in range(C // SB):
                base = i * SB
                bi = b[base:base + SB, :]
                qi = self.qf[h][base:base + SB, :]
                for j in range(SB):
                    cj = self.c_scr[h, base + j:base + j + 1, :]
                    ps.append(qi * jnp.exp2(jnp.where(self.sub_row >= j, bi - cj, NEG_BIG)))
            p = jnp.concatenate([jnp.concatenate(ps[:nh], axis=0),
                                 jnp.concatenate(ps[nh:], axis=0)], axis=1)
            self.r.append(_dot(p.astype(BF16), self.ones_bd))
            yield self.WEIGHTS['diag']

    def finish(self):
        C, SB = HG_CHUNK, HG_SUB
        nh = (C // SB) * SB // 2
        for h in range(HG_HEADS):
            sl = slice(h * HG_DIM, (h + 1) * HG_DIM)
            diag = []
            for i in range(C // SB):
                acc = None
                for j in range(SB):
                    u = i * SB + j
                    ru = self.r[h][(u % nh) * SB:(u % nh + 1) * SB,
                                   (u // nh) * HG_DIM:(u // nh + 1) * HG_DIM]
                    term = ru * self.v_scr[h, i * SB + j:i * SB + j + 1, :]
                    acc = term if acc is None else acc + term
                diag.append(acc)
            out = self.o[h] + jnp.concatenate(diag, axis=0)
            ms = jnp.mean(out * out, axis=-1, keepdims=True)
            out = out * lax.rsqrt(ms + RMS_EPS) * self.gain_ref[:, sl]
            out = out * self.load('g', h).astype(F32)
            self.o_ref[pl.ds(self.r0, C), sl] = out.astype(self.o_ref.dtype)
            yield self.WEIGHTS['finish']


PROJ_TM = 1024
PROJ_TN = 512
PROJ_RS = 128
PROJ_EDGES = tuple(c // PROJ_TN for c in (
    0, HG_WIDTH, 2 * HG_WIDTH, 3 * HG_WIDTH, 4 * HG_WIDTH, 4 * HG_WIDTH + ATT_WIDTH,
    4 * HG_WIDTH + ATT_WIDTH + 2 * KV_WIDTH, D_IN))
HGRN_SECTION = {'q': 0, 'k': 1, 'v': 2, 'g': 3}
HGRN_CHUNKS_PER_STEP = PROJ_TM // HG_CHUNK // (PROJ_EDGES[7] - PROJ_EDGES[6])
Q_SCALE = ATT_DIM ** -0.5 * LOG2E
SIDE_CASTS = ((16, HG_WIDTH // 16), (16, ATT_WIDTH // 16), (16, D_MODEL // 16),
              (16, D_MODEL // 16), (D_FF // 128, 128))


def _rope(t, cos, sin_signed):
    w = t.shape[1]
    reps = w // cos.shape[1]
    cos_t = jnp.concatenate([cos] * reps, axis=1) if reps > 1 else cos
    sin_t = jnp.concatenate([sin_signed] * reps, axis=1) if reps > 1 else sin_signed
    lane = lax.broadcasted_iota(jnp.int32, t.shape, 1)
    first_half = (lane % ATT_DIM) < (ATT_DIM // 2)
    rot = jnp.where(first_half, pltpu.roll(t, w - ATT_DIM // 2, 1), pltpu.roll(t, ATT_DIM // 2, 1))
    return t * cos_t + rot * sin_t


def _dup_heads(t):
    lane = lax.broadcasted_iota(jnp.int32, (t.shape[0], LANES), 1)
    low = lane < ATT_DIM
    out = []
    for g in range(ATT_KV_HEADS):
        src = t[:, LANES * (g // 2):LANES * (g // 2 + 1)]
        swapped = pltpu.roll(src, ATT_DIM, 1)
        out.append(jnp.where(low, src, swapped) if g % 2 == 0 else jnp.where(low, swapped, src))
    return jnp.concatenate(out, axis=1)


def _proj_kernel(x_ref, w_ref, lbl_ref, cos_ref, sin_ref, gain_ref, *refs):
    n_side = len(SIDE_CASTS)
    side_in = refs[:n_side]
    oa_ref, oaq_ref, okv_ref, og_ref = refs[n_side:n_side + 4]
    side_out = refs[n_side + 4:2 * n_side + 4]
    xb_ref, hp_ref, lf_ref, state_ref, c_scr, v_scr = refs[2 * n_side + 4:]
    i = pl.program_id(0)
    j = pl.program_id(1)
    e = PROJ_EDGES
    tiles_per_section = HG_WIDTH // PROJ_TN
    heads_per_tile = PROJ_TN // HG_DIM

    @pl.when(j == 0)
    def _():
        xb_ref[...] = x_ref[...].astype(BF16)

    @pl.when((i == 0) & (j == 0))
    def _():
        state_ref[...] = jnp.zeros_like(state_ref)

    def segment(s, interleave=None):
        def deco(epilogue):
            @pl.when((j >= e[s]) & (j < e[s + 1]))
            def _():
                for src, dst in zip(side_in, side_out):
                    dst[...] = src[...].astype(BF16)
                n_sub = PROJ_TM // PROJ_RS
                stages, total = interleave() if interleave is not None else (iter(()), 0)
                budget = 0.0
                for r in range(n_sub):
                    rows = slice(r * PROJ_RS, (r + 1) * PROJ_RS)
                    epilogue(rows, _dot(xb_ref[rows, :], w_ref[...]))
                    budget += total / n_sub
                    while budget > 0:
                        weight = next(stages, None)
                        if weight is None:
                            break
                        budget -= weight
                for _ in stages:
                    pass
        return deco

    @segment(0)
    def _(rows, acc):
        hp_ref[j, rows, :] = (acc * jax.nn.sigmoid(acc)).astype(BF16)

    @segment(1)
    def _(rows, acc):
        lbl = lbl_ref[...]
        lexp = jnp.exp(lbl - jnp.max(lbl, axis=0, keepdims=True))
        lb = lexp[0:1, :] / jnp.sum(lexp, axis=0, keepdims=True)
        f = lb + (1.0 - lb) * jax.nn.sigmoid(acc)
        lf_ref[j - e[1], rows, :] = jnp.log(f)
        hp_ref[j, rows, :] = (1.0 - f).astype(BF16)

    @segment(2)
    def _(rows, acc):
        hp_ref[j, rows, :] = acc.astype(BF16)

    @segment(3)
    def _(rows, acc):
        hp_ref[j, rows, :] = (acc * jax.nn.sigmoid(acc)).astype(BF16)

    @segment(4)
    def _(rows, acc):
        oaq_ref[rows, :] = (_rope(acc, cos_ref[rows, :], sin_ref[rows, :]) * Q_SCALE).astype(BF16)

    @segment(5)
    def _(rows, acc):
        k = _rope(acc[:, :KV_WIDTH], cos_ref[rows, :], sin_ref[rows, :])
        okv_ref[rows, :] = jnp.concatenate(
            [_dup_heads(k), _dup_heads(acc[:, KV_WIDTH:])], axis=1).astype(BF16)

    def hgrn_chunks():
        consts = _hgrn_consts()
        chunks = []
        for u in range(HGRN_CHUNKS_PER_STEP):
            r0 = pl.multiple_of(((j - e[6]) * HGRN_CHUNKS_PER_STEP + u) * HG_CHUNK, HG_CHUNK)

            def load(name, h, r0=r0):
                lanes = slice((h % heads_per_tile) * HG_DIM, (h % heads_per_tile + 1) * HG_DIM)
                if name == 'lf':
                    return lf_ref[h // heads_per_tile, pl.ds(r0, HG_CHUNK), lanes]
                tile = HGRN_SECTION[name] * tiles_per_section + h // heads_per_tile
                return hp_ref[tile, pl.ds(r0, HG_CHUNK), lanes]

            chunks.append(_HgrnChunk(consts, load, gain_ref, oa_ref, r0, state_ref,
                                     c_scr.at[u], v_scr.at[u]))

        def stages():
            for c in chunks:
                yield from c.cumsum()
            for c in chunks:
                yield from c.diag()
            for c in chunks:
                yield from c.state()
                for k in range(3):
                    yield from c.level(k)
                yield from c.off_values()
                yield from c.finish()

        return stages(), len(chunks) * _HgrnChunk.TOTAL_WEIGHT

    @segment(6, interleave=hgrn_chunks)
    def _(rows, acc):
        og_ref[rows, :] = jax.nn.sigmoid(acc).astype(BF16)


def _proj(x, wb, lb_logits, cos, sin_signed, gain, side_weights):
    tm, tn = PROJ_TM, PROJ_TN
    m, k = x.shape
    e = PROJ_EDGES
    nj = D_IN // tn
    assert e[6] - e[5] == 1 and 2 * KV_WIDTH == tn
    assert (tm // HG_CHUNK) % (e[7] - e[6]) == 0

    def span_spec(s, width=tn):
        lo, hi = e[s], e[s + 1]
        return pl.BlockSpec((tm, width), lambda i, j: (i, jnp.clip(j - lo, 0, hi - lo - 1)))

    def sds(width, dtype):
        return jax.ShapeDtypeStruct((m, width), dtype)

    side_specs, side_shapes = [], []
    for w, (rows, steps) in zip(side_weights, SIDE_CASTS):
        assert w.shape[0] == rows * steps and steps <= (m // tm) * nj
        side_specs.append(pl.BlockSpec(
            (rows, w.shape[1]), lambda i, j, steps=steps: (jnp.minimum(i * nj + j, steps - 1), 0)))
        side_shapes.append(jax.ShapeDtypeStruct(w.shape, BF16))

    row_tile = lambda i, j: (i, 0)
    outs = pl.pallas_call(
        _proj_kernel,
        out_shape=(sds(HG_WIDTH, BF16), sds(ATT_WIDTH, BF16), sds(4 * KV_WIDTH, BF16),
                   sds(2 * D_MODEL, BF16), *side_shapes),
        grid=(m // tm, nj),
        in_specs=[pl.BlockSpec((tm, k), row_tile),
                  pl.BlockSpec((k, tn), lambda i, j: (0, j)),
                  pl.BlockSpec((lb_logits.shape[0], tn),
                               lambda i, j: (0, jnp.clip(j - e[1], 0, e[2] - e[1] - 1))),
                  pl.BlockSpec((tm, LANES), row_tile),
                  pl.BlockSpec((tm, LANES), row_tile),
                  pl.BlockSpec((1, HG_WIDTH), lambda i, j: (0, 0)),
                  *side_specs],
        out_specs=(pl.BlockSpec((tm, HG_WIDTH), row_tile), span_spec(4),
                   pl.BlockSpec((tm, 4 * KV_WIDTH), row_tile), span_spec(6), *side_specs),
        scratch_shapes=[pltpu.VMEM((tm, k), BF16),
                        pltpu.VMEM((4 * HG_WIDTH // tn, tm, tn), BF16),
                        pltpu.VMEM((HG_WIDTH // tn, tm, tn), F32),
                        pltpu.VMEM((HG_HEADS, HG_DIM, HG_DIM), F32),
                        pltpu.VMEM((HGRN_CHUNKS_PER_STEP, HG_HEADS, HG_CHUNK, HG_DIM), F32),
                        pltpu.VMEM((HGRN_CHUNKS_PER_STEP, HG_HEADS, HG_CHUNK, HG_DIM), F32)],
        compiler_params=pltpu.CompilerParams(
            dimension_semantics=("arbitrary", "arbitrary"), vmem_limit_bytes=PROJ_VMEM_LIMIT),
        name="proj_hgrn2",
    )(x, wb, lb_logits, cos, sin_signed, gain, *side_weights)
    return outs[0], outs[1], outs[2], outs[3], outs[4:]


def _attn_kernel(q_ref, kvp_ref, kvc_ref, sink_ref, o_ref):
    n = pl.program_id(0)
    blk = q_ref.shape[0]
    qi = lax.broadcasted_iota(jnp.int32, (blk, 2 * blk), 0)
    ci = lax.broadcasted_iota(jnp.int32, (blk, 2 * blk), 1)
    rel = qi + blk - ci
    first_key = jnp.where(n > 0, 0, blk)
    bias1 = jnp.where(rel >= 0, jnp.where(rel < blk, jnp.where(ci >= first_key, 0.0, NEG_BIG),
                                          NEG_BIG), NEG_BIG)
    bias = jnp.concatenate([bias1] * ATT_GROUP, axis=0)
    lane = lax.broadcasted_iota(jnp.int32, (blk, LANES), 1)
    low = lane < ATT_DIM
    zero = jnp.zeros((), BF16)

    groups = range(ATT_KV_HEADS)
    pairs = range(ATT_GROUP // 2)
    s = []
    for g in groups:
        ksl = slice(g * LANES, (g + 1) * LANES)
        kd = jnp.concatenate([kvp_ref[:, ksl], kvc_ref[:, ksl]], axis=0)
        qs = []
        for pr in pairs:
            col = (g * len(pairs) + pr) * LANES
            qp = q_ref[:, col:col + LANES]
            qs += [jnp.where(low, qp, zero), jnp.where(low, zero, qp)]
        s.append(_dot_nt(jnp.concatenate(qs, axis=0), kd) + bias)

    p, sink_w = [], []
    for g in groups:
        sink = jnp.concatenate(
            [jnp.broadcast_to(sink_ref[g * ATT_GROUP + u:g * ATT_GROUP + u + 1, :], (blk, LANES))
             for u in range(ATT_GROUP)], axis=0) * LOG2E
        m = jnp.max(jnp.maximum(s[g][:, :blk], s[g][:, blk:]), axis=-1, keepdims=True)
        m = jnp.maximum(jnp.broadcast_to(m, (ATT_GROUP * blk, LANES)), sink)
        p.append(jnp.exp2(s[g] - jnp.concatenate([m, m], axis=1)).astype(BF16))
        sink_w.append(jnp.exp2(sink - m))

    ones_kl = jnp.ones((2 * blk, LANES), BF16)
    t = []
    for g in groups:
        vsl = slice(2 * KV_WIDTH + g * LANES, 2 * KV_WIDTH + (g + 1) * LANES)
        vd = jnp.concatenate([kvp_ref[:, vsl], kvc_ref[:, vsl]], axis=0)
        denom = _dot(p[g], ones_kl) + sink_w[g]
        t.append(_dot(p[g], vd) / denom)

    for g in groups:
        for pr in pairs:
            out = jnp.where(low, t[g][2 * pr * blk:(2 * pr + 1) * blk],
                            t[g][(2 * pr + 1) * blk:(2 * pr + 2) * blk])
            col = (g * len(pairs) + pr) * LANES
            o_ref[:, col:col + LANES] = out.astype(o_ref.dtype)


def _attn(aq, akv, sinks_b):
    blk = WINDOW
    s = aq.shape[0]
    return pl.pallas_call(
        _attn_kernel,
        out_shape=jax.ShapeDtypeStruct((s, ATT_WIDTH), BF16),
        grid=(s // blk,),
        in_specs=[pl.BlockSpec((blk, ATT_WIDTH), lambda n: (n, 0)),
                  pl.BlockSpec((blk, 4 * KV_WIDTH), lambda n: (jnp.maximum(n - 1, 0), 0)),
                  pl.BlockSpec((blk, 4 * KV_WIDTH), lambda n: (n, 0)),
                  pl.BlockSpec((ATT_HEADS, LANES), lambda n: (0, 0))],
        out_specs=pl.BlockSpec((blk, ATT_WIDTH), lambda n: (n, 0)),
        compiler_params=pltpu.CompilerParams(
            dimension_semantics=("parallel",), vmem_limit_bytes=VMEM_LIMIT),
        name="swa",
    )(aq, akv, akv, sinks_b)


def _layer_norm(r, gain, bias):
    mu = jnp.mean(r, axis=-1, keepdims=True)
    d = r - mu
    var = jnp.mean(d * d, axis=-1, keepdims=True)
    return d * lax.rsqrt(var + LN_EPS) * gain + bias


MERGE_TM = 256
MERGE_RS = 128


def _merge_kernel(oa_ref, ob_ref, ga_ref, gb_ref, x_ref, wa_ref, wb_ref, wo_ref, g_ref, b_ref,
                  x1_ref, x1b_ref):
    for r in range(MERGE_TM // MERGE_RS):
        rows = slice(r * MERGE_RS, (r + 1) * MERGE_RS)
        ya = _dot(oa_ref[rows, :], wa_ref[...])
        yb = _dot(ob_ref[rows, :], wb_ref[...])
        mixed = ga_ref[rows, :].astype(F32) * ya + gb_ref[rows, :].astype(F32) * yb
        z = _dot(mixed.astype(BF16), wo_ref[...])
        x1 = _layer_norm(ALPHA * x_ref[rows, :] + z, g_ref[...], b_ref[...])
        x1_ref[rows, :] = x1
        x1b_ref[rows, :] = x1.astype(BF16)


def _merge(oa, ob, gates, x, wa, wb, wo, g1, b1):
    tm = MERGE_TM
    s = x.shape[0]
    row = lambda i: (i, 0)
    const = lambda i: (0, 0)
    resident = dict(pipeline_mode=pl.Buffered(1))
    return pl.pallas_call(
        _merge_kernel,
        out_shape=(jax.ShapeDtypeStruct((s, D_MODEL), F32), jax.ShapeDtypeStruct((s, D_MODEL), BF16)),
        grid=(s // tm,),
        in_specs=[pl.BlockSpec((tm, HG_WIDTH), row), pl.BlockSpec((tm, ATT_WIDTH), row),
                  pl.BlockSpec((tm, D_MODEL), row), pl.BlockSpec((tm, D_MODEL), lambda i: (i, 1)),
                  pl.BlockSpec((tm, D_MODEL), row),
                  pl.BlockSpec((HG_WIDTH, D_MODEL), const, **resident),
                  pl.BlockSpec((ATT_WIDTH, D_MODEL), const, **resident),
                  pl.BlockSpec((D_MODEL, D_MODEL), const, **resident),
                  pl.BlockSpec((1, D_MODEL), const), pl.BlockSpec((1, D_MODEL), const)],
        out_specs=(pl.BlockSpec((tm, D_MODEL), row), pl.BlockSpec((tm, D_MODEL), row)),
        compiler_params=pltpu.CompilerParams(
            dimension_semantics=("parallel",), vmem_limit_bytes=VMEM_LIMIT),
        name="merge_ln1",
    )(oa, ob, gates, gates, x, wa, wb, wo, g1, b1)


def _ffn_kernel(xb_ref, x_ref, w1_ref, w2_ref, g_ref, b_ref, o_ref, acc_ref):
    j = pl.program_id(1)

    @pl.when(j == 0)
    def _():
        acc_ref[...] = jnp.zeros_like(acc_ref)

    h = jnp.maximum(_dot(xb_ref[...], w1_ref[...]), 0.0)
    acc_ref[...] += _dot((h * h).astype(BF16), w2_ref[...])

    @pl.when(j == pl.num_programs(1) - 1)
    def _():
        o_ref[...] = _layer_norm(ALPHA * x_ref[...] + acc_ref[...], g_ref[...], b_ref[...])


def _ffn(x1b, x1, w1, w2, g2, b2):
    tm, tf = 512, 512
    s = x1.shape[0]
    return pl.pallas_call(
        _ffn_kernel,
        out_shape=jax.ShapeDtypeStruct((s, D_MODEL), F32),
        grid=(s // tm, D_FF // tf),
        in_specs=[pl.BlockSpec((tm, D_MODEL), lambda i, j: (i, 0)),
                  pl.BlockSpec((tm, D_MODEL), lambda i, j: (i, 0)),
                  pl.BlockSpec((D_MODEL, tf), lambda i, j: (0, j)),
                  pl.BlockSpec((tf, D_MODEL), lambda i, j: (j, 0)),
                  pl.BlockSpec((1, D_MODEL), lambda i, j: (0, 0)),
                  pl.BlockSpec((1, D_MODEL), lambda i, j: (0, 0))],
        out_specs=pl.BlockSpec((tm, D_MODEL), lambda i, j: (i, 0)),
        scratch_shapes=[pltpu.VMEM((tm, D_MODEL), F32)],
        compiler_params=pltpu.CompilerParams(
            dimension_semantics=("parallel", "arbitrary"), vmem_limit_bytes=VMEM_LIMIT),
        name="ffn_ln2",
    )(x1b, x1, w1, w2, g2, b2)


def _rope_tables(seq):
    half = ATT_DIM // 2
    inv = ROPE_THETA ** (-np.arange(half, dtype=np.float64) / half)
    ang = np.arange(seq, dtype=np.float64)[:, None] * inv[None, :]
    cos, sin = np.cos(ang), np.sin(ang)
    cos_t = np.concatenate([cos, cos, cos, cos], axis=1).astype(np.float32)
    sin_t = np.concatenate([-sin, sin, -sin, sin], axis=1).astype(np.float32)
    return jnp.asarray(cos_t), jnp.asarray(sin_t)


def kernel(x, w_in, hg_lb_logits, hg_norm_gain, attn_sinks, w_branch_a, w_branch_b, w_out,
           ln1_gain, ln1_bias, w_ff1, w_ff2, ln2_gain, ln2_bias):
    b, s, d = x.shape
    assert (b, s, d) == (1, SEQ, D_MODEL) and w_in.shape == (1, D_MODEL, D_IN)
    x2 = x.reshape(s, d)
    cos_t, sin_t = _rope_tables(s)
    oa, aq, akv, gates, (wa, wb, wo, w1, w2) = _proj(
        x2, w_in[0].astype(BF16), hg_lb_logits, cos_t, sin_t, hg_norm_gain,
        (w_branch_a[0], w_branch_b[0], w_out[0], w_ff1[0], w_ff2[0]))
    sinks_b = jnp.broadcast_to(attn_sinks[0][:, None], (ATT_HEADS, LANES))
    ob = _attn(aq, akv, sinks_b)
    x1, x1b = _merge(oa, ob, gates, x2, wa, wb, wo, ln1_gain, ln1_bias)
    out = _ffn(x1b, x1, w1, w2, ln2_gain, ln2_bias)
    return out.reshape(b, s, d)
```

```python
import math

import jax
import jax.numpy as jnp
import numpy as np
from jax import lax
from jax.experimental import pallas as pl
from jax.experimental.pallas import tpu as pltpu

F32 = jnp.float32
BF16 = jnp.bfloat16

D_MODEL = 2048
SEQ = 8192
HG_HEADS = 8
HG_DIM = 128
HG_WIDTH = HG_HEADS * HG_DIM
ATT_HEADS = 16
ATT_KV_HEADS = 4
ATT_DIM = 64
ATT_GROUP = ATT_HEADS // ATT_KV_HEADS
ATT_WIDTH = ATT_HEADS * ATT_DIM
KV_WIDTH = ATT_KV_HEADS * ATT_DIM
WINDOW = 128
ROPE_THETA = 10000.0
D_FF = 4 * D_MODEL
D_IN = 4 * HG_WIDTH + ATT_WIDTH + 2 * KV_WIDTH + 2 * D_MODEL
ALPHA = 2.0 ** 0.25
LN_EPS = 1e-5
RMS_EPS = 1e-6
LOG2E = math.log2(math.e)

LANES = 128
HG_CHUNK = 64
HG_SUB = 8
NEG_BIG = -1e30

VMEM_LIMIT = 56 * 1024 * 1024
PROJ_VMEM_LIMIT = 60 * 1024 * 1024


def _dot(a, b):
    return jnp.dot(a, b, preferred_element_type=F32)


def _dot_nt(a, b):
    return lax.dot_general(a, b, (((1,), (1,)), ((), ())), preferred_element_type=F32)


def _dot_tn(a, b):
    return lax.dot_general(a, b, (((0,), (0,)), ((), ())), preferred_element_type=F32)


def _hgrn_consts():
    C, SB = HG_CHUNK, HG_SUB
    ri = lax.broadcasted_iota(jnp.int32, (C, C), 0)
    ci = lax.broadcasted_iota(jnp.int32, (C, C), 1)
    tri = (ri >= ci).astype(BF16)
    row = lax.broadcasted_iota(jnp.int32, (C, 1), 0)
    sub_row = lax.broadcasted_iota(jnp.int32, (SB, 1), 0)
    levels = []
    half = C // 2
    while half >= SB:
        pair = ((ri // (2 * half)) == (ci // (2 * half))) & ((ri % (2 * half)) >= half) \
            & ((ci % (2 * half)) < half)
        levels.append((half, (row % (2 * half)) >= half, jnp.where(pair, 1.0, 0.0)))
        half //= 2
    return tri, sub_row, levels


class _HgrnChunk:
    WEIGHTS = dict(cumsum=8, diag=95, state=20, level=15, off_values=3, finish=40)
    TOTAL_WEIGHT = HG_HEADS * (8 + 95 + 20 + 3 * 15 + 3 + 40)

    def __init__(self, consts, load, gain_ref, o_ref, r0, state_ref, c_scr, v_scr):
        self.tri, self.sub_row, self.levels = consts
        assert len(self.levels) == 3
        self.load, self.gain_ref, self.o_ref, self.r0 = load, gain_ref, o_ref, r0
        self.state_ref, self.c_scr, self.v_scr = state_ref, c_scr, v_scr
        hs = range(HG_HEADS)
        self.qf = [load('q', h).astype(F32) for h in hs]
        self.kk = [load('k', h).astype(F32) for h in hs]
        self.vb = [load('v', h) for h in hs]
        self.b, self.o, self.r = [], [], []
        self.s_off = [None] * HG_HEADS

    def cumsum(self):
        for h in range(HG_HEADS):
            lf = self.load('lf', h)
            hi = lf.astype(BF16)
            lo = (lf - hi.astype(F32)).astype(BF16)
            both = _dot(self.tri, jnp.concatenate([hi, lo], axis=1))
            self.b.append((both[:, :HG_DIM] + both[:, HG_DIM:]) * LOG2E)
            yield self.WEIGHTS['cumsum']

    def state(self):
        C = HG_CHUNK
        for h in range(HG_HEADS):
            b, st = self.b[h], self.state_ref[h]
            self.o.append(_dot_nt((self.qf[h] * jnp.exp2(b)).astype(BF16), st.astype(BF16)))
            b_last = b[C - 1:C, :]
            kd = self.kk[h] * jnp.exp2(b_last - b)
            self.state_ref[h] = st * jnp.exp2(b_last) + _dot_tn(self.vb[h], kd.astype(BF16))
            yield self.WEIGHTS['state']

    def level(self, k):
        lvl, second_half, pair = self.levels[k]
        for h in range(HG_HEADS):
            b = self.b[h]
            ref = jnp.concatenate(
                [jnp.broadcast_to(b[base + lvl:base + lvl + 1, :], (2 * lvl, HG_DIM))
                 for base in range(0, HG_CHUNK, 2 * lvl)], axis=0)
            e = jnp.exp2(-jnp.abs(b - ref))
            qs = jnp.where(second_half, self.qf[h] * e, 0.0)
            ks = jnp.where(second_half, 0.0, self.kk[h] * e)
            s = _dot_nt(qs.astype(BF16), ks.astype(BF16)) * pair
            self.s_off[h] = s if self.s_off[h] is None else self.s_off[h] + s
            yield self.WEIGHTS['level']

    def off_values(self):
        for h in range(HG_HEADS):
            self.o[h] = self.o[h] + _dot(self.s_off[h].astype(BF16), self.vb[h])
            yield self.WEIGHTS['off_values']

    def diag(self):
        C, SB = HG_CHUNK, HG_SUB
        for h in range(HG_HEADS):
            b = self.b[h]
            self.c_scr[h] = b - jnp.log2(jnp.maximum(self.kk[h], 0.0))
            self.v_scr[h] = self.vb[h].astype(F32)
            scores = []
            for i in range(C // SB):
                base = i * SB
                bi = b[base:base + SB, :]
                qi = self.qf[h][base:base + SB, :]
                for j in range(SB):
                    cj = self.c_scr[h, base + j:base + j + 1, :]
                    p = qi * jnp.exp2(jnp.where(self.sub_row >= j, bi - cj, NEG_BIG))
                    scores.append(jnp.sum(p, axis=-1, keepdims=True))
            self.r.append(scores)
            yield self.WEIGHTS['diag']

    def finish(self):
        C, SB = HG_CHUNK, HG_SUB
        for h in range(HG_HEADS):
            sl = slice(h * HG_DIM, (h + 1) * HG_DIM)
            diag = []
            for i in range(C // SB):
                acc = None
                for j in range(SB):
                    term = self.r[h][i * SB + j] * self.v_scr[h, i * SB + j:i * SB + j + 1, :]
                    acc = term if acc is None else acc + term
                diag.append(acc)
            out = self.o[h] + jnp.concatenate(diag, axis=0)
            ms = jnp.mean(out * out, axis=-1, keepdims=True)
            out = out * lax.rsqrt(ms + RMS_EPS) * self.gain_ref[:, sl]
            out = out * self.load('g', h).astype(F32)
            self.o_ref[pl.ds(self.r0, C), sl] = out.astype(self.o_ref.dtype)
            yield self.WEIGHTS['finish']


PROJ_TM = 1024
PROJ_TN = 512
PROJ_RS = 128
PROJ_EDGES = tuple(c // PROJ_TN for c in (
    0, HG_WIDTH, 2 * HG_WIDTH, 3 * HG_WIDTH, 4 * HG_WIDTH, 4 * HG_WIDTH + ATT_WIDTH,
    4 * HG_WIDTH + ATT_WIDTH + 2 * KV_WIDTH, D_IN))
HGRN_SECTION = {'q': 0, 'k': 1, 'v': 2, 'g': 3}
HGRN_CHUNKS_PER_STEP = PROJ_TM // HG_CHUNK // (PROJ_EDGES[7] - PROJ_EDGES[6])
Q_SCALE = ATT_DIM ** -0.5 * LOG2E
SIDE_CASTS = ((16, HG_WIDTH // 16), (16, ATT_WIDTH // 16), (16, D_MODEL // 16),
              (16, D_MODEL // 16), (D_FF // 128, 128))


def _rope(t, cos, sin_signed):
    w = t.shape[1]
    reps = w // cos.shape[1]
    cos_t = jnp.concatenate([cos] * reps, axis=1) if reps > 1 else cos
    sin_t = jnp.concatenate([sin_signed] * reps, axis=1) if reps > 1 else sin_signed
    lane = lax.broadcasted_iota(jnp.int32, t.shape, 1)
    first_half = (lane % ATT_DIM) < (ATT_DIM // 2)
    rot = jnp.where(first_half, pltpu.roll(t, w - ATT_DIM // 2, 1), pltpu.roll(t, ATT_DIM // 2, 1))
    return t * cos_t + rot * sin_t


def _dup_heads(t):
    lane = lax.broadcasted_iota(jnp.int32, (t.shape[0], LANES), 1)
    low = lane < ATT_DIM
    out = []
    for g in range(ATT_KV_HEADS):
        src = t[:, LANES * (g // 2):LANES * (g // 2 + 1)]
        swapped = pltpu.roll(src, ATT_DIM, 1)
        out.append(jnp.where(low, src, swapped) if g % 2 == 0 else jnp.where(low, swapped, src))
    return jnp.concatenate(out, axis=1)


def _proj_kernel(x_ref, w_ref, lbl_ref, cos_ref, sin_ref, gain_ref, *refs):
    n_side = len(SIDE_CASTS)
    side_in = refs[:n_side]
    oa_ref, oaq_ref, okv_ref, og_ref = refs[n_side:n_side + 4]
    side_out = refs[n_side + 4:2 * n_side + 4]
    xb_ref, hp_ref, lf_ref, state_ref, c_scr, v_scr = refs[2 * n_side + 4:]
    i = pl.program_id(0)
    j = pl.program_id(1)
    e = PROJ_EDGES
    tiles_per_section = HG_WIDTH // PROJ_TN
    heads_per_tile = PROJ_TN // HG_DIM

    @pl.when(j == 0)
    def _():
        xb_ref[...] = x_ref[...].astype(BF16)

    @pl.when((i == 0) & (j == 0))
    def _():
        state_ref[...] = jnp.zeros_like(state_ref)

    def segment(s, interleave=None):
        def deco(epilogue):
            @pl.when((j >= e[s]) & (j < e[s + 1]))
            def _():
                for src, dst in zip(side_in, side_out):
                    dst[...] = src[...].astype(BF16)
                n_sub = PROJ_TM // PROJ_RS
                stages, total = interleave() if interleave is not None else (iter(()), 0)
                budget = 0.0
                for r in range(n_sub):
                    rows = slice(r * PROJ_RS, (r + 1) * PROJ_RS)
                    epilogue(rows, _dot(xb_ref[rows, :], w_ref[...]))
                    budget += total / n_sub
                    while budget > 0:
                        weight = next(stages, None)
                        if weight is None:
                            break
                        budget -= weight
                for _ in stages:
                    pass
        return deco

    @segment(0)
    def _(rows, acc):
        hp_ref[j, rows, :] = (acc * jax.nn.sigmoid(acc)).astype(BF16)

    @segment(1)
    def _(rows, acc):
        lbl = lbl_ref[...]
        lexp = jnp.exp(lbl - jnp.max(lbl, axis=0, keepdims=True))
        lb = lexp[0:1, :] / jnp.sum(lexp, axis=0, keepdims=True)
        f = lb + (1.0 - lb) * jax.nn.sigmoid(acc)
        lf_ref[j - e[1], rows, :] = jnp.log(f)
        hp_ref[j, rows, :] = (1.0 - f).astype(BF16)

    @segment(2)
    def _(rows, acc):
        hp_ref[j, rows, :] = acc.astype(BF16)

    @segment(3)
    def _(rows, acc):
        hp_ref[j, rows, :] = (acc * jax.nn.sigmoid(acc)).astype(BF16)

    @segment(4)
    def _(rows, acc):
        oaq_ref[rows, :] = (_rope(acc, cos_ref[rows, :], sin_ref[rows, :]) * Q_SCALE).astype(BF16)

    @segment(5)
    def _(rows, acc):
        k = _rope(acc[:, :KV_WIDTH], cos_ref[rows, :], sin_ref[rows, :])
        okv_ref[rows, :] = jnp.concatenate(
            [_dup_heads(k), _dup_heads(acc[:, KV_WIDTH:])], axis=1).astype(BF16)

    def hgrn_chunks():
        consts = _hgrn_consts()
        chunks = []
        for u in range(HGRN_CHUNKS_PER_STEP):
            r0 = pl.multiple_of(((j - e[6]) * HGRN_CHUNKS_PER_STEP + u) * HG_CHUNK, HG_CHUNK)

            def load(name, h, r0=r0):
                lanes = slice((h % heads_per_tile) * HG_DIM, (h % heads_per_tile + 1) * HG_DIM)
                if name == 'lf':
                    return lf_ref[h // heads_per_tile, pl.ds(r0, HG_CHUNK), lanes]
                tile = HGRN_SECTION[name] * tiles_per_section + h // heads_per_tile
                return hp_ref[tile, pl.ds(r0, HG_CHUNK), lanes]

            chunks.append(_HgrnChunk(consts, load, gain_ref, oa_ref, r0, state_ref,
                                     c_scr.at[u], v_scr.at[u]))

        def stages():
            for c in chunks:
                yield from c.cumsum()
            for c in chunks:
                yield from c.diag()
            for c in chunks:
                yield from c.state()
                for k in range(3):
                    yield from c.level(k)
                yield from c.off_values()
                yield from c.finish()

        return stages(), len(chunks) * _HgrnChunk.TOTAL_WEIGHT

    @segment(6, interleave=hgrn_chunks)
    def _(rows, acc):
        og_ref[rows, :] = jax.nn.sigmoid(acc).astype(BF16)


def _proj(x, wb, lb_logits, cos, sin_signed, gain, side_weights):
    tm, tn = PROJ_TM, PROJ_TN
    m, k = x.shape
    e = PROJ_EDGES
    nj = D_IN // tn
    assert e[6] - e[5] == 1 and 2 * KV_WIDTH == tn
    assert (tm // HG_CHUNK) % (e[7] - e[6]) == 0

    def span_spec(s, width=tn):
        lo, hi = e[s], e[s + 1]
        return pl.BlockSpec((tm, width), lambda i, j: (i, jnp.clip(j - lo, 0, hi - lo - 1)))

    def sds(width, dtype):
        return jax.ShapeDtypeStruct((m, width), dtype)

    side_specs, side_shapes = [], []
    for w, (rows, steps) in zip(side_weights, SIDE_CASTS):
        assert w.shape[0] == rows * steps and steps <= (m // tm) * nj
        side_specs.append(pl.BlockSpec(
            (rows, w.shape[1]), lambda i, j, steps=steps: (jnp.minimum(i * nj + j, steps - 1), 0)))
        side_shapes.append(jax.ShapeDtypeStruct(w.shape, BF16))

    row_tile = lambda i, j: (i, 0)
    outs = pl.pallas_call(
        _proj_kernel,
        out_shape=(sds(HG_WIDTH, BF16), sds(ATT_WIDTH, BF16), sds(4 * KV_WIDTH, BF16),
                   sds(2 * D_MODEL, BF16), *side_shapes),
        grid=(m // tm, nj),
        in_specs=[pl.BlockSpec((tm, k), row_tile),
                  pl.BlockSpec((k, tn), lambda i, j: (0, j)),
                  pl.BlockSpec((lb_logits.shape[0], tn),
                               lambda i, j: (0, jnp.clip(j - e[1], 0, e[2] - e[1] - 1))),
                  pl.BlockSpec((tm, LANES), row_tile),
                  pl.BlockSpec((tm, LANES), row_tile),
                  pl.BlockSpec((1, HG_WIDTH), lambda i, j: (0, 0)),
                  *side_specs],
        out_specs=(pl.BlockSpec((tm, HG_WIDTH), row_tile), span_spec(4),
                   pl.BlockSpec((tm, 4 * KV_WIDTH), row_tile), span_spec(6), *side_specs),
        scratch_shapes=[pltpu.VMEM((tm, k), BF16),
                        pltpu.VMEM((4 * HG_WIDTH // tn, tm, tn), BF16),
                        pltpu.VMEM((HG_WIDTH // tn, tm, tn), F32),
                        pltpu.VMEM((HG_HEADS, HG_DIM, HG_DIM), F32),
                        pltpu.VMEM((HGRN_CHUNKS_PER_STEP, HG_HEADS, HG_CHUNK, HG_DIM), F32),
                        pltpu.VMEM((HGRN_CHUNKS_PER_STEP, HG_HEADS, HG_CHUNK, HG_DIM), F32)],
        compiler_params=pltpu.CompilerParams(
            dimension_semantics=("arbitrary", "arbitrary"), vmem_limit_bytes=PROJ_VMEM_LIMIT),
        name="proj_hgrn2",
    )(x, wb, lb_logits, cos, sin_signed, gain, *side_weights)
    return outs[0], outs[1], outs[2], outs[3], outs[4:]


def _attn_kernel(q_ref, kvp_ref, kvc_ref, sink_ref, o_ref):
    n = pl.program_id(0)
    blk = q_ref.shape[0]
    qi = lax.broadcasted_iota(jnp.int32, (blk, 2 * blk), 0)
    ci = lax.broadcasted_iota(jnp.int32, (blk, 2 * blk), 1)
    rel = qi + blk - ci
    first_key = jnp.where(n > 0, 0, blk)
    bias1 = jnp.where(rel >= 0, jnp.where(rel < blk, jnp.where(ci >= first_key, 0.0, NEG_BIG),
                                          NEG_BIG), NEG_BIG)
    bias = jnp.concatenate([bias1] * ATT_GROUP, axis=0)
    lane = lax.broadcasted_iota(jnp.int32, (blk, LANES), 1)
    low = lane < ATT_DIM
    zero = jnp.zeros((), BF16)

    groups = range(ATT_KV_HEADS)
    pairs = range(ATT_GROUP // 2)
    s = []
    for g in groups:
        ksl = slice(g * LANES, (g + 1) * LANES)
        kd = jnp.concatenate([kvp_ref[:, ksl], kvc_ref[:, ksl]], axis=0)
        qs = []
        for pr in pairs:
            col = (g * len(pairs) + pr) * LANES
            qp = q_ref[:, col:col + LANES]
            qs += [jnp.where(low, qp, zero), jnp.where(low, zero, qp)]
        s.append(_dot_nt(jnp.concatenate(qs, axis=0), kd) + bias)

    p, sink_w = [], []
    for g in groups:
        sink = jnp.concatenate(
            [jnp.broadcast_to(sink_ref[g * ATT_GROUP + u:g * ATT_GROUP + u + 1, :], (blk, LANES))
             for u in range(ATT_GROUP)], axis=0) * LOG2E
        m = jnp.max(jnp.maximum(s[g][:, :blk], s[g][:, blk:]), axis=-1, keepdims=True)
        m = jnp.maximum(jnp.broadcast_to(m, (ATT_GROUP * blk, LANES)), sink)
        p.append(jnp.exp2(s[g] - jnp.concatenate([m, m], axis=1)).astype(BF16))
        sink_w.append(jnp.exp2(sink - m))

    ones_kl = jnp.ones((2 * blk, LANES), BF16)
    t = []
    for g in groups:
        vsl = slice(2 * KV_WIDTH + g * LANES, 2 * KV_WIDTH + (g + 1) * LANES)
        vd = jnp.concatenate([kvp_ref[:, vsl], kvc_ref[:, vsl]], axis=0)
        denom = _dot(p[g], ones_kl) + sink_w[g]
        t.append(_dot(p[g], vd) / denom)

    for g in groups:
        for pr in pairs:
            out = jnp.where(low, t[g][2 * pr * blk:(2 * pr + 1) * blk],
                            t[g][(2 * pr + 1) * blk:(2 * pr + 2) * blk])
            col = (g * len(pairs) + pr) * LANES
            o_ref[:, col:col + LANES] = out.astype(o_ref.dtype)


def _attn(aq, akv, sinks_b):
    blk = WINDOW
    s = aq.shape[0]
    return pl.pallas_call(
        _attn_kernel,
        out_shape=jax.ShapeDtypeStruct((s, ATT_WIDTH), BF16),
        grid=(s // blk,),
        in_specs=[pl.BlockSpec((blk, ATT_WIDTH), lambda n: (n, 0)),
                  pl.BlockSpec((blk, 4 * KV_WIDTH), lambda n: (jnp.maximum(n - 1, 0), 0)),
                  pl.BlockSpec((blk, 4 * KV_WIDTH), lambda n: (n, 0)),
                  pl.BlockSpec((ATT_HEADS, LANES), lambda n: (0, 0))],
        out_specs=pl.BlockSpec((blk, ATT_WIDTH), lambda n: (n, 0)),
        compiler_params=pltpu.CompilerParams(
            dimension_semantics=("parallel",), vmem_limit_bytes=VMEM_LIMIT),
        name="swa",
    )(aq, akv, akv, sinks_b)


def _layer_norm(r, gain, bias):
    mu = jnp.mean(r, axis=-1, keepdims=True)
    d = r - mu
    var = jnp.mean(d * d, axis=-1, keepdims=True)
    return d * lax.rsqrt(var + LN_EPS) * gain + bias


MERGE_TM = 256
MERGE_RS = 128


def _merge_kernel(oa_ref, ob_ref, ga_ref, gb_ref, x_ref, wa_ref, wb_ref, wo_ref, g_ref, b_ref,
                  x1_ref, x1b_ref):
    for r in range(MERGE_TM // MERGE_RS):
        rows = slice(r * MERGE_RS, (r + 1) * MERGE_RS)
        ya = _dot(oa_ref[rows, :], wa_ref[...])
        yb = _dot(ob_ref[rows, :], wb_ref[...])
        mixed = ga_ref[rows, :].astype(F32) * ya + gb_ref[rows, :].astype(F32) * yb
        z = _dot(mixed.astype(BF16), wo_ref[...])
        x1 = _layer_norm(ALPHA * x_ref[rows, :] + z, g_ref[...], b_ref[...])
        x1_ref[rows, :] = x1
        x1b_ref[rows, :] = x1.astype(BF16)


def _merge(oa, ob, gates, x, wa, wb, wo, g1, b1):
    tm = MERGE_TM
    s = x.shape[0]
    row = lambda i: (i, 0)
    const = lambda i: (0, 0)
    resident = dict(pipeline_mode=pl.Buffered(1))
    return pl.pallas_call(
        _merge_kernel,
        out_shape=(jax.ShapeDtypeStruct((s, D_MODEL), F32), jax.ShapeDtypeStruct((s, D_MODEL), BF16)),
        grid=(s // tm,),
        in_specs=[pl.BlockSpec((tm, HG_WIDTH), row), pl.BlockSpec((tm, ATT_WIDTH), row),
                  pl.BlockSpec((tm, D_MODEL), row), pl.BlockSpec((tm, D_MODEL), lambda i: (i, 1)),
                  pl.BlockSpec((tm, D_MODEL), row),
                  pl.BlockSpec((HG_WIDTH, D_MODEL), const, **resident),
                  pl.BlockSpec((ATT_WIDTH, D_MODEL), const, **resident),
                  pl.BlockSpec((D_MODEL, D_MODEL), const, **resident),
                  pl.BlockSpec((1, D_MODEL), const), pl.BlockSpec((1, D_MODEL), const)],
        out_specs=(pl.BlockSpec((tm, D_MODEL), row), pl.BlockSpec((tm, D_MODEL), row)),
        compiler_params=pltpu.CompilerParams(
            dimension_semantics=("parallel",), vmem_limit_bytes=VMEM_LIMIT),
        name="merge_ln1",
    )(oa, ob, gates, gates, x, wa, wb, wo, g1, b1)


def _ffn_kernel(xb_ref, x_ref, w1_ref, w2_ref, g_ref, b_ref, o_ref, acc_ref):
    j = pl.program_id(1)

    @pl.when(j == 0)
    def _():
        acc_ref[...] = jnp.zeros_like(acc_ref)

    h = jnp.maximum(_dot(xb_ref[...], w1_ref[...]), 0.0)
    acc_ref[...] += _dot((h * h).astype(BF16), w2_ref[...])

    @pl.when(j == pl.num_programs(1) - 1)
    def _():
        o_ref[...] = _layer_norm(ALPHA * x_ref[...] + acc_ref[...], g_ref[...], b_ref[...])


def _ffn(x1b, x1, w1, w2, g2, b2):
    tm, tf = 512, 512
    s = x1.shape[0]
    return pl.pallas_call(
        _ffn_kernel,
        out_shape=jax.ShapeDtypeStruct((s, D_MODEL), F32),
        grid=(s // tm, D_FF // tf),
        in_specs=[pl.BlockSpec((tm, D_MODEL), lambda i, j: (i, 0)),
                  pl.BlockSpec((tm, D_MODEL), lambda i, j: (i, 0)),
                  pl.BlockSpec((D_MODEL, tf), lambda i, j: (0, j)),
                  pl.BlockSpec((tf, D_MODEL), lambda i, j: (j, 0)),
                  pl.BlockSpec((1, D_MODEL), lambda i, j: (0, 0)),
                  pl.BlockSpec((1, D_MODEL), lambda i, j: (0, 0))],
        out_specs=pl.BlockSpec((tm, D_MODEL), lambda i, j: (i, 0)),
        scratch_shapes=[pltpu.VMEM((tm, D_MODEL), F32)],
        compiler_params=pltpu.CompilerParams(
            dimension_semantics=("parallel", "arbitrary"), vmem_limit_bytes=VMEM_LIMIT),
        name="ffn_ln2",
    )(x1b, x1, w1, w2, g2, b2)


def _rope_tables(seq):
    half = ATT_DIM // 2
    inv = ROPE_THETA ** (-np.arange(half, dtype=np.float64) / half)
    ang = np.arange(seq, dtype=np.float64)[:, None] * inv[None, :]
    cos, sin = np.cos(ang), np.sin(ang)
    cos_t = np.concatenate([cos, cos, cos, cos], axis=1).astype(np.float32)
    sin_t = np.concatenate([-sin, sin, -sin, sin], axis=1).astype(np.float32)
    return jnp.asarray(cos_t), jnp.asarray(sin_t)


def kernel(x, w_in, hg_lb_logits, hg_norm_gain, attn_sinks, w_branch_a, w_branch_b, w_out,
           ln1_gain, ln1_bias, w_ff1, w_ff2, ln2_gain, ln2_bias):
    b, s, d = x.shape
    assert (b, s, d) == (1, SEQ, D_MODEL) and w_in.shape == (1, D_MODEL, D_IN)
    x2 = x.reshape(s, d)
    cos_t, sin_t = _rope_tables(s)
    oa, aq, akv, gates, (wa, wb, wo, w1, w2) = _proj(
        x2, w_in[0].astype(BF16), hg_lb_logits, cos_t, sin_t, hg_norm_gain,
        (w_branch_a[0], w_branch_b[0], w_out[0], w_ff1[0], w_ff2[0]))
    sinks_b = jnp.broadcast_to(attn_sinks[0][:, None], (ATT_HEADS, LANES))
    ob = _attn(aq, akv, sinks_b)
    x1, x1b = _merge(oa, ob, gates, x2, wa, wb, wo, ln1_gain, ln1_bias)
    out = _ffn(x1b, x1, w1, w2, ln2_gain, ln2_bias)
    return out.reshape(b, s, d)
```

```python
import math

import jax
import jax.numpy as jnp
import numpy as np
from jax import lax
from jax.experimental import pallas as pl
from jax.experimental.pallas import tpu as pltpu

F32 = jnp.float32
BF16 = jnp.bfloat16

D_MODEL = 2048
SEQ = 8192
HG_HEADS = 8
HG_DIM = 128
HG_WIDTH = HG_HEADS * HG_DIM
ATT_HEADS = 16
ATT_KV_HEADS = 4
ATT_DIM = 64
ATT_GROUP = ATT_HEADS // ATT_KV_HEADS
ATT_WIDTH = ATT_HEADS * ATT_DIM
KV_WIDTH = ATT_KV_HEADS * ATT_DIM
WINDOW = 128
ROPE_THETA = 10000.0
D_FF = 4 * D_MODEL
D_IN = 4 * HG_WIDTH + ATT_WIDTH + 2 * KV_WIDTH + 2 * D_MODEL
ALPHA = 2.0 ** 0.25
LN_EPS = 1e-5
RMS_EPS = 1e-6
LOG2E = math.log2(math.e)

LANES = 128
HG_CHUNK = 64
HG_SUB = 8
NEG_BIG = -1e30

VMEM_LIMIT = 56 * 1024 * 1024
PROJ_VMEM_LIMIT = 60 * 1024 * 1024


def _dot(a, b):
    return jnp.dot(a, b, preferred_element_type=F32)


def _dot_nt(a, b):
    return lax.dot_general(a, b, (((1,), (1,)), ((), ())), preferred_element_type=F32)


def _dot_tn(a, b):
    return lax.dot_general(a, b, (((0,), (0,)), ((), ())), preferred_element_type=F32)


def _hgrn_consts():
    C, SB = HG_CHUNK, HG_SUB
    ri = lax.broadcasted_iota(jnp.int32, (C, C), 0)
    ci = lax.broadcasted_iota(jnp.int32, (C, C), 1)
    tri = (ri >= ci).astype(BF16)
    row = lax.broadcasted_iota(jnp.int32, (C, 1), 0)
    sub_row = lax.broadcasted_iota(jnp.int32, (SB, 1), 0)
    levels = []
    half = C // 2
    while half >= SB:
        pair = ((ri // (2 * half)) == (ci // (2 * half))) & ((ri % (2 * half)) >= half) \
            & ((ci % (2 * half)) < half)
        levels.append((half, (row % (2 * half)) >= half, jnp.where(pair, 1.0, 0.0)))
        half //= 2
    return tri, sub_row, levels


class _HgrnChunk:
    WEIGHTS = dict(cumsum=8, diag=95, state=20, level=15, off_values=3, finish=40)
    TOTAL_WEIGHT = HG_HEADS * (8 + 95 + 20 + 3 * 15 + 3 + 40)

    def __init__(self, consts, load, gain_ref, o_ref, r0, state_ref, c_scr, v_scr):
        self.tri, self.sub_row, self.levels = consts
        assert len(self.levels) == 3
        self.load, self.gain_ref, self.o_ref, self.r0 = load, gain_ref, o_ref, r0
        self.state_ref, self.c_scr, self.v_scr = state_ref, c_scr, v_scr
        hs = range(HG_HEADS)
        self.qf = [load('q', h).astype(F32) for h in hs]
        self.kk = [load('k', h).astype(F32) for h in hs]
        self.vb = [load('v', h) for h in hs]
        self.b, self.o, self.r = [], [], []
        self.s_off = [None] * HG_HEADS

    def cumsum(self):
        for h in range(HG_HEADS):
            lf = self.load('lf', h)
            hi = lf.astype(BF16)
            lo = (lf - hi.astype(F32)).astype(BF16)
            both = _dot(self.tri, jnp.concatenate([hi, lo], axis=1))
            self.b.append((both[:, :HG_DIM] + both[:, HG_DIM:]) * LOG2E)
            yield self.WEIGHTS['cumsum']

    def state(self):
        C = HG_CHUNK
        for h in range(HG_HEADS):
            b, st = self.b[h], self.state_ref[h]
            self.o.append(_dot_nt((self.qf[h] * jnp.exp2(b)).astype(BF16), st.astype(BF16)))
            b_last = b[C - 1:C, :]
            kd = self.kk[h] * jnp.exp2(b_last - b)
            self.state_ref[h] = st * jnp.exp2(b_last) + _dot_tn(self.vb[h], kd.astype(BF16))
            yield self.WEIGHTS['state']

    def level(self, k):
        lvl, second_half, pair = self.levels[k]
        for h in range(HG_HEADS):
            b = self.b[h]
            ref = jnp.concatenate(
                [jnp.broadcast_to(b[base + lvl:base + lvl + 1, :], (2 * lvl, HG_DIM))
                 for base in range(0, HG_CHUNK, 2 * lvl)], axis=0)
            e = jnp.exp2(-jnp.abs(b - ref))
            qs = jnp.where(second_half, self.qf[h] * e, 0.0)
            ks = jnp.where(second_half, 0.0, self.kk[h] * e)
            s = _dot_nt(qs.astype(BF16), ks.astype(BF16)) * pair
            self.s_off[h] = s if self.s_off[h] is None else self.s_off[h] + s
            yield self.WEIGHTS['level']

    def off_values(self):
        for h in range(HG_HEADS):
            self.o[h] = self.o[h] + _dot(self.s_off[h].astype(BF16), self.vb[h])
            yield self.WEIGHTS['off_values']

    def diag(self):
        C, SB = HG_CHUNK, HG_SUB
        for h in range(HG_HEADS):
            b = self.b[h]
            self.c_scr[h] = b - jnp.log2(jnp.maximum(self.kk[h], 0.0))
            self.v_scr[h] = self.vb[h].astype(F32)
            scores = []
            for i in range(C // SB):
                base = i * SB
                bi = b[base:base + SB, :]
                qi = self.qf[h][base:base + SB, :]
                for j in range(SB):
                    cj = self.c_scr[h, base + j:base + j + 1, :]
                    p = qi * jnp.exp2(jnp.where(self.sub_row >= j, bi - cj, NEG_BIG))
                    scores.append(jnp.sum(p, axis=-1, keepdims=True))
            self.r.append(scores)
            yield self.WEIGHTS['diag']

    def finish(self):
        C, SB = HG_CHUNK, HG_SUB
        for h in range(HG_HEADS):
            sl = slice(h * HG_DIM, (h + 1) * HG_DIM)
            diag = []
            for i in range(C // SB):
                acc = None
                for j in range(SB):
                    term = self.r[h][i * SB + j] * self.v_scr[h, i * SB + j:i * SB + j + 1, :]
                    acc = term if acc is None else acc + term
                diag.append(acc)
            out = self.o[h] + jnp.concatenate(diag, axis=0)
            ms = jnp.mean(out * out, axis=-1, keepdims=True)
            out = out * lax.rsqrt(ms + RMS_EPS) * self.gain_ref[:, sl]
            out = out * self.load('g', h).astype(F32)
            self.o_ref[pl.ds(self.r0, C), sl] = out.astype(self.o_ref.dtype)
            yield self.WEIGHTS['finish']


PROJ_TM = 1024
PROJ_TN = 512
PROJ_RS = 128
PROJ_EDGES = tuple(c // PROJ_TN for c in (
    0, HG_WIDTH, 2 * HG_WIDTH, 3 * HG_WIDTH, 4 * HG_WIDTH, 4 * HG_WIDTH + ATT_WIDTH,
    4 * HG_WIDTH + ATT_WIDTH + 2 * KV_WIDTH, D_IN))
HGRN_SECTION = {'q': 0, 'k': 1, 'v': 2, 'g': 3}
HGRN_CHUNKS_PER_STEP = PROJ_TM // HG_CHUNK // (PROJ_EDGES[7] - PROJ_EDGES[6])
Q_SCALE = ATT_DIM ** -0.5 * LOG2E


def _rope(t, cos, sin_signed):
    w = t.shape[1]
    reps = w // cos.shape[1]
    cos_t = jnp.concatenate([cos] * reps, axis=1) if reps > 1 else cos
    sin_t = jnp.concatenate([sin_signed] * reps, axis=1) if reps > 1 else sin_signed
    lane = lax.broadcasted_iota(jnp.int32, t.shape, 1)
    first_half = (lane % ATT_DIM) < (ATT_DIM // 2)
    rot = jnp.where(first_half, pltpu.roll(t, w - ATT_DIM // 2, 1), pltpu.roll(t, ATT_DIM // 2, 1))
    return t * cos_t + rot * sin_t


def _dup_heads(t):
    lane = lax.broadcasted_iota(jnp.int32, (t.shape[0], LANES), 1)
    low = lane < ATT_DIM
    out = []
    for g in range(ATT_KV_HEADS):
        src = t[:, LANES * (g // 2):LANES * (g // 2 + 1)]
        swapped = pltpu.roll(src, ATT_DIM, 1)
        out.append(jnp.where(low, src, swapped) if g % 2 == 0 else jnp.where(low, swapped, src))
    return jnp.concatenate(out, axis=1)


def _proj_kernel(x_ref, w_ref, lbl_ref, cos_ref, sin_ref, gain_ref,
                 oa_ref, oaq_ref, okv_ref, og_ref,
                 xb_ref, hp_ref, lf_ref, state_ref, c_scr, v_scr):
    i = pl.program_id(0)
    j = pl.program_id(1)
    e = PROJ_EDGES
    tiles_per_section = HG_WIDTH // PROJ_TN
    heads_per_tile = PROJ_TN // HG_DIM

    @pl.when(j == 0)
    def _():
        xb_ref[...] = x_ref[...].astype(BF16)

    @pl.when((i == 0) & (j == 0))
    def _():
        state_ref[...] = jnp.zeros_like(state_ref)

    def segment(s, interleave=None):
        def deco(epilogue):
            @pl.when((j >= e[s]) & (j < e[s + 1]))
            def _():
                n_sub = PROJ_TM // PROJ_RS
                stages, total = interleave() if interleave is not None else (iter(()), 0)
                budget = 0.0
                for r in range(n_sub):
                    budget += total / n_sub
                    while budget > 0:
                        weight = next(stages, None)
                        if weight is None:
                            break
                        budget -= weight
                    rows = slice(r * PROJ_RS, (r + 1) * PROJ_RS)
                    epilogue(rows, _dot(xb_ref[rows, :], w_ref[...]))
                for _ in stages:
                    pass
        return deco

    @segment(0)
    def _(rows, acc):
        hp_ref[j, rows, :] = (acc * jax.nn.sigmoid(acc)).astype(BF16)

    @segment(1)
    def _(rows, acc):
        lbl = lbl_ref[...]
        lexp = jnp.exp(lbl - jnp.max(lbl, axis=0, keepdims=True))
        lb = lexp[0:1, :] / jnp.sum(lexp, axis=0, keepdims=True)
        f = lb + (1.0 - lb) * jax.nn.sigmoid(acc)
        lf_ref[j - e[1], rows, :] = jnp.log(f)
        hp_ref[j, rows, :] = (1.0 - f).astype(BF16)

    @segment(2)
    def _(rows, acc):
        hp_ref[j, rows, :] = acc.astype(BF16)

    @segment(3)
    def _(rows, acc):
        hp_ref[j, rows, :] = (acc * jax.nn.sigmoid(acc)).astype(BF16)

    @segment(4)
    def _(rows, acc):
        oaq_ref[rows, :] = (_rope(acc, cos_ref[rows, :], sin_ref[rows, :]) * Q_SCALE).astype(BF16)

    @segment(5)
    def _(rows, acc):
        k = _rope(acc[:, :KV_WIDTH], cos_ref[rows, :], sin_ref[rows, :])
        okv_ref[rows, :] = jnp.concatenate(
            [_dup_heads(k), _dup_heads(acc[:, KV_WIDTH:])], axis=1).astype(BF16)

    def hgrn_chunks():
        consts = _hgrn_consts()
        chunks = []
        for u in range(HGRN_CHUNKS_PER_STEP):
            r0 = pl.multiple_of(((j - e[6]) * HGRN_CHUNKS_PER_STEP + u) * HG_CHUNK, HG_CHUNK)

            def load(name, h, r0=r0):
                lanes = slice((h % heads_per_tile) * HG_DIM, (h % heads_per_tile + 1) * HG_DIM)
                if name == 'lf':
                    return lf_ref[h // heads_per_tile, pl.ds(r0, HG_CHUNK), lanes]
                tile = HGRN_SECTION[name] * tiles_per_section + h // heads_per_tile
                return hp_ref[tile, pl.ds(r0, HG_CHUNK), lanes]

            chunks.append(_HgrnChunk(consts, load, gain_ref, oa_ref, r0, state_ref,
                                     c_scr.at[u], v_scr.at[u]))

        def stages():
            for c in chunks:
                yield from c.cumsum()
            for c in chunks:
                yield from c.diag()
            for c in chunks:
                yield from c.state()
                for k in range(3):
                    yield from c.level(k)
                yield from c.off_values()
                yield from c.finish()

        return stages(), len(chunks) * _HgrnChunk.TOTAL_WEIGHT

    @segment(6, interleave=hgrn_chunks)
    def _(rows, acc):
        og_ref[rows, :] = jax.nn.sigmoid(acc).astype(BF16)


def _proj(x, wb, lb_logits, cos, sin_signed, gain):
    tm, tn = PROJ_TM, PROJ_TN
    m, k = x.shape
    e = PROJ_EDGES
    assert e[6] - e[5] == 1 and 2 * KV_WIDTH == tn
    assert (tm // HG_CHUNK) % (e[7] - e[6]) == 0

    def span_spec(s, width=tn):
        lo, hi = e[s], e[s + 1]
        return pl.BlockSpec((tm, width), lambda i, j: (i, jnp.clip(j - lo, 0, hi - lo - 1)))

    def sds(width, dtype):
        return jax.ShapeDtypeStruct((m, width), dtype)

    row_tile = lambda i, j: (i, 0)
    return pl.pallas_call(
        _proj_kernel,
        out_shape=(sds(HG_WIDTH, BF16), sds(ATT_WIDTH, BF16), sds(4 * KV_WIDTH, BF16),
                   sds(2 * D_MODEL, BF16)),
        grid=(m // tm, D_IN // tn),
        in_specs=[pl.BlockSpec((tm, k), row_tile),
                  pl.BlockSpec((k, tn), lambda i, j: (0, j)),
                  pl.BlockSpec((lb_logits.shape[0], tn),
                               lambda i, j: (0, jnp.clip(j - e[1], 0, e[2] - e[1] - 1))),
                  pl.BlockSpec((tm, LANES), row_tile),
                  pl.BlockSpec((tm, LANES), row_tile),
                  pl.BlockSpec((1, HG_WIDTH), lambda i, j: (0, 0))],
        out_specs=(pl.BlockSpec((tm, HG_WIDTH), row_tile), span_spec(4),
                   pl.BlockSpec((tm, 4 * KV_WIDTH), row_tile), span_spec(6)),
        scratch_shapes=[pltpu.VMEM((tm, k), BF16),
                        pltpu.VMEM((4 * HG_WIDTH // tn, tm, tn), BF16),
                        pltpu.VMEM((HG_WIDTH // tn, tm, tn), F32),
                        pltpu.VMEM((HG_HEADS, HG_DIM, HG_DIM), F32),
                        pltpu.VMEM((HGRN_CHUNKS_PER_STEP, HG_HEADS, HG_CHUNK, HG_DIM), F32),
                        pltpu.VMEM((HGRN_CHUNKS_PER_STEP, HG_HEADS, HG_CHUNK, HG_DIM), F32)],
        compiler_params=pltpu.CompilerParams(
            dimension_semantics=("arbitrary", "arbitrary"), vmem_limit_bytes=PROJ_VMEM_LIMIT),
        name="proj_hgrn2",
    )(x, wb, lb_logits, cos, sin_signed, gain)


def _attn_kernel(q_ref, kvp_ref, kvc_ref, sink_ref, *refs):
    n_side = SIDE_CASTS
    side_in, o_ref, side_out = refs[:n_side], refs[n_side], refs[n_side + 1:]
    for src, dst in zip(side_in, side_out):
        dst[...] = src[...].astype(BF16)
    n = pl.program_id(0)
    blk = q_ref.shape[0]
    qi = lax.broadcasted_iota(jnp.int32, (blk, 2 * blk), 0)
    ci = lax.broadcasted_iota(jnp.int32, (blk, 2 * blk), 1)
    rel = qi + blk - ci
    first_key = jnp.where(n > 0, 0, blk)
    bias1 = jnp.where(rel >= 0, jnp.where(rel < blk, jnp.where(ci >= first_key, 0.0, NEG_BIG),
                                          NEG_BIG), NEG_BIG)
    bias = jnp.concatenate([bias1] * ATT_GROUP, axis=0)
    lane = lax.broadcasted_iota(jnp.int32, (blk, LANES), 1)
    low = lane < ATT_DIM
    zero = jnp.zeros((), BF16)

    groups = range(ATT_KV_HEADS)
    pairs = range(ATT_GROUP // 2)
    s = []
    for g in groups:
        ksl = slice(g * LANES, (g + 1) * LANES)
        kd = jnp.concatenate([kvp_ref[:, ksl], kvc_ref[:, ksl]], axis=0)
        qs = []
        for pr in pairs:
            col = (g * len(pairs) + pr) * LANES
            qp = q_ref[:, col:col + LANES]
            qs += [jnp.where(low, qp, zero), jnp.where(low, zero, qp)]
        s.append(_dot_nt(jnp.concatenate(qs, axis=0), kd) + bias)

    p, denom = [], []
    for g in groups:
        sink = jnp.concatenate(
            [jnp.broadcast_to(sink_ref[g * ATT_GROUP + u:g * ATT_GROUP + u + 1, :], (blk, LANES))
             for u in range(ATT_GROUP)], axis=0) * LOG2E
        m = jnp.max(jnp.maximum(s[g][:, :blk], s[g][:, blk:]), axis=-1, keepdims=True)
        m = jnp.maximum(jnp.broadcast_to(m, (ATT_GROUP * blk, LANES)), sink)
        w = jnp.exp2(s[g] - jnp.concatenate([m, m], axis=1))
        total = jnp.sum(w[:, :blk] + w[:, blk:], axis=-1, keepdims=True)
        denom.append(jnp.broadcast_to(total, (ATT_GROUP * blk, LANES)) + jnp.exp2(sink - m))
        p.append(w.astype(BF16))

    t = []
    for g in groups:
        vsl = slice(2 * KV_WIDTH + g * LANES, 2 * KV_WIDTH + (g + 1) * LANES)
        vd = jnp.concatenate([kvp_ref[:, vsl], kvc_ref[:, vsl]], axis=0)
        t.append(_dot(p[g], vd) / denom[g])

    for g in groups:
        for pr in pairs:
            out = jnp.where(low, t[g][2 * pr * blk:(2 * pr + 1) * blk],
                            t[g][(2 * pr + 1) * blk:(2 * pr + 2) * blk])
            col = (g * len(pairs) + pr) * LANES
            o_ref[:, col:col + LANES] = out.astype(o_ref.dtype)


SIDE_CASTS = 5


def _attn(aq, akv, sinks_b, side_weights):
    blk = WINDOW
    s = aq.shape[0]
    steps = s // blk
    assert len(side_weights) == SIDE_CASTS
    side_specs, side_shapes = [], []
    for w in side_weights:
        slab = w.shape[0] // steps
        assert slab * steps == w.shape[0] and slab % 16 == 0
        side_specs.append(pl.BlockSpec((slab, w.shape[1]), lambda n: (n, 0)))
        side_shapes.append(jax.ShapeDtypeStruct(w.shape, BF16))
    outs = pl.pallas_call(
        _attn_kernel,
        out_shape=(jax.ShapeDtypeStruct((s, ATT_WIDTH), BF16), *side_shapes),
        grid=(steps,),
        in_specs=[pl.BlockSpec((blk, ATT_WIDTH), lambda n: (n, 0)),
                  pl.BlockSpec((blk, 4 * KV_WIDTH), lambda n: (jnp.maximum(n - 1, 0), 0)),
                  pl.BlockSpec((blk, 4 * KV_WIDTH), lambda n: (n, 0)),
                  pl.BlockSpec((ATT_HEADS, LANES), lambda n: (0, 0)), *side_specs],
        out_specs=(pl.BlockSpec((blk, ATT_WIDTH), lambda n: (n, 0)), *side_specs),
        compiler_params=pltpu.CompilerParams(
            dimension_semantics=("parallel",), vmem_limit_bytes=VMEM_LIMIT),
        name="swa",
    )(aq, akv, akv, sinks_b, *side_weights)
    return outs[0], outs[1:]


def _layer_norm(r, gain, bias):
    mu = jnp.mean(r, axis=-1, keepdims=True)
    d = r - mu
    var = jnp.mean(d * d, axis=-1, keepdims=True)
    return d * lax.rsqrt(var + LN_EPS) * gain + bias


MERGE_TM = 256
MERGE_RS = 128


def _merge_kernel(oa_ref, ob_ref, ga_ref, gb_ref, x_ref, wa_ref, wb_ref, wo_ref, g_ref, b_ref,
                  x1_ref, x1b_ref):
    for r in range(MERGE_TM // MERGE_RS):
        rows = slice(r * MERGE_RS, (r + 1) * MERGE_RS)
        ya = _dot(oa_ref[rows, :], wa_ref[...])
        yb = _dot(ob_ref[rows, :], wb_ref[...])
        mixed = ga_ref[rows, :].astype(F32) * ya + gb_ref[rows, :].astype(F32) * yb
        z = _dot(mixed.astype(BF16), wo_ref[...])
        x1 = _layer_norm(ALPHA * x_ref[rows, :] + z, g_ref[...], b_ref[...])
        x1_ref[rows, :] = x1
        x1b_ref[rows, :] = x1.astype(BF16)


def _merge(oa, ob, gates, x, wa, wb, wo, g1, b1):
    tm = MERGE_TM
    s = x.shape[0]
    row = lambda i: (i, 0)
    const = lambda i: (0, 0)
    resident = dict(pipeline_mode=pl.Buffered(1))
    return pl.pallas_call(
        _merge_kernel,
        out_shape=(jax.ShapeDtypeStruct((s, D_MODEL), F32), jax.ShapeDtypeStruct((s, D_MODEL), BF16)),
        grid=(s // tm,),
        in_specs=[pl.BlockSpec((tm, HG_WIDTH), row), pl.BlockSpec((tm, ATT_WIDTH), row),
                  pl.BlockSpec((tm, D_MODEL), row), pl.BlockSpec((tm, D_MODEL), lambda i: (i, 1)),
                  pl.BlockSpec((tm, D_MODEL), row),
                  pl.BlockSpec((HG_WIDTH, D_MODEL), const, **resident),
                  pl.BlockSpec((ATT_WIDTH, D_MODEL), const, **resident),
                  pl.BlockSpec((D_MODEL, D_MODEL), const, **resident),
                  pl.BlockSpec((1, D_MODEL), const), pl.BlockSpec((1, D_MODEL), const)],
        out_specs=(pl.BlockSpec((tm, D_MODEL), row), pl.BlockSpec((tm, D_MODEL), row)),
        compiler_params=pltpu.CompilerParams(
            dimension_semantics=("parallel",), vmem_limit_bytes=VMEM_LIMIT),
        name="merge_ln1",
    )(oa, ob, gates, gates, x, wa, wb, wo, g1, b1)


def _ffn_kernel(xb_ref, x_ref, w1_ref, w2_ref, g_ref, b_ref, o_ref, acc_ref):
    j = pl.program_id(1)

    @pl.when(j == 0)
    def _():
        acc_ref[...] = jnp.zeros_like(acc_ref)

    h = jnp.maximum(_dot(xb_ref[...], w1_ref[...]), 0.0)
    acc_ref[...] += _dot((h * h).astype(BF16), w2_ref[...])

    @pl.when(j == pl.num_programs(1) - 1)
    def _():
        o_ref[...] = _layer_norm(ALPHA * x_ref[...] + acc_ref[...], g_ref[...], b_ref[...])


def _ffn(x1b, x1, w1, w2, g2, b2):
    tm, tf = 512, 512
    s = x1.shape[0]
    return pl.pallas_call(
        _ffn_kernel,
        out_shape=jax.ShapeDtypeStruct((s, D_MODEL), F32),
        grid=(s // tm, D_FF // tf),
        in_specs=[pl.BlockSpec((tm, D_MODEL), lambda i, j: (i, 0)),
                  pl.BlockSpec((tm, D_MODEL), lambda i, j: (i, 0)),
                  pl.BlockSpec((D_MODEL, tf), lambda i, j: (0, j)),
                  pl.BlockSpec((tf, D_MODEL), lambda i, j: (j, 0)),
                  pl.BlockSpec((1, D_MODEL), lambda i, j: (0, 0)),
                  pl.BlockSpec((1, D_MODEL), lambda i, j: (0, 0))],
        out_specs=pl.BlockSpec((tm, D_MODEL), lambda i, j: (i, 0)),
        scratch_shapes=[pltpu.VMEM((tm, D_MODEL), F32)],
        compiler_params=pltpu.CompilerParams(
            dimension_semantics=("parallel", "arbitrary"), vmem_limit_bytes=VMEM_LIMIT),
        name="ffn_ln2",
    )(x1b, x1, w1, w2, g2, b2)


def _rope_tables(seq):
    half = ATT_DIM // 2
    inv = ROPE_THETA ** (-np.arange(half, dtype=np.float64) / half)
    ang = np.arange(seq, dtype=np.float64)[:, None] * inv[None, :]
    cos, sin = np.cos(ang), np.sin(ang)
    cos_t = np.concatenate([cos, cos, cos, cos], axis=1).astype(np.float32)
    sin_t = np.concatenate([-sin, sin, -sin, sin], axis=1).astype(np.float32)
    return jnp.asarray(cos_t), jnp.asarray(sin_t)


def kernel(x, w_in, hg_lb_logits, hg_norm_gain, attn_sinks, w_branch_a, w_branch_b, w_out,
           ln1_gain, ln1_bias, w_ff1, w_ff2, ln2_gain, ln2_bias):
    b, s, d = x.shape
    assert (b, s, d) == (1, SEQ, D_MODEL) and w_in.shape == (1, D_MODEL, D_IN)
    x2 = x.reshape(s, d)
    cos_t, sin_t = _rope_tables(s)
    oa, aq, akv, gates = _proj(x2, w_in[0].astype(BF16), hg_lb_logits, cos_t, sin_t, hg_norm_gain)
    sinks_b = jnp.broadcast_to(attn_sinks[0][:, None], (ATT_HEADS, LANES))
    ob, (wa, wb, wo, w1, w2) = _attn(
        aq, akv, sinks_b, (w_branch_a[0], w_branch_b[0], w_out[0], w_ff1[0], w_ff2[0]))
    x1, x1b = _merge(oa, ob, gates, x2, wa, wb, wo, ln1_gain, ln1_bias)
    out = _ffn(x1b, x1, w1, w2, ln2_gain, ln2_bias)
    return out.reshape(b, s, d)
```

```python
import math

import jax
import jax.numpy as jnp
import numpy as np
from jax import lax
from jax.experimental import pallas as pl
from jax.experimental.pallas import tpu as pltpu

F32 = jnp.float32
BF16 = jnp.bfloat16

D_MODEL = 2048
SEQ = 8192
HG_HEADS = 8
HG_DIM = 128
HG_WIDTH = HG_HEADS * HG_DIM
ATT_HEADS = 16
ATT_KV_HEADS = 4
ATT_DIM = 64
ATT_GROUP = ATT_HEADS // ATT_KV_HEADS
ATT_WIDTH = ATT_HEADS * ATT_DIM
KV_WIDTH = ATT_KV_HEADS * ATT_DIM
WINDOW = 128
ROPE_THETA = 10000.0
D_FF = 4 * D_MODEL
D_IN = 4 * HG_WIDTH + ATT_WIDTH + 2 * KV_WIDTH + 2 * D_MODEL
ALPHA = 2.0 ** 0.25
LN_EPS = 1e-5
RMS_EPS = 1e-6
LOG2E = math.log2(math.e)

LANES = 128
HG_CHUNK = 64
HG_SUB = 8
NEG_BIG = -1e30

VMEM_LIMIT = 56 * 1024 * 1024
PROJ_VMEM_LIMIT = 60 * 1024 * 1024


def _dot(a, b):
    return jnp.dot(a, b, preferred_element_type=F32)


def _dot_nt(a, b):
    return lax.dot_general(a, b, (((1,), (1,)), ((), ())), preferred_element_type=F32)


def _dot_tn(a, b):
    return lax.dot_general(a, b, (((0,), (0,)), ((), ())), preferred_element_type=F32)


def _hgrn_consts():
    C, SB = HG_CHUNK, HG_SUB
    ri = lax.broadcasted_iota(jnp.int32, (C, C), 0)
    ci = lax.broadcasted_iota(jnp.int32, (C, C), 1)
    tri = (ri >= ci).astype(BF16)
    row = lax.broadcasted_iota(jnp.int32, (C, 1), 0)
    sub_row = lax.broadcasted_iota(jnp.int32, (SB, 1), 0)
    levels = []
    half = C // 2
    while half >= SB:
        pair = ((ri // (2 * half)) == (ci // (2 * half))) & ((ri % (2 * half)) >= half) \
            & ((ci % (2 * half)) < half)
        levels.append((half, (row % (2 * half)) >= half, jnp.where(pair, 1.0, 0.0)))
        half //= 2
    return tri, sub_row, levels


class _HgrnChunk:
    WEIGHTS = dict(cumsum=8, diag=95, state=20, level=15, off_values=3, finish=40)
    TOTAL_WEIGHT = HG_HEADS * (8 + 95 + 20 + 3 * 15 + 3 + 40)

    def __init__(self, consts, load, gain_ref, o_ref, r0, state_ref, c_scr, v_scr):
        self.tri, self.sub_row, self.levels = consts
        assert len(self.levels) == 3
        self.load, self.gain_ref, self.o_ref, self.r0 = load, gain_ref, o_ref, r0
        self.state_ref, self.c_scr, self.v_scr = state_ref, c_scr, v_scr
        hs = range(HG_HEADS)
        self.qf = [load('q', h).astype(F32) for h in hs]
        self.kk = [load('k', h).astype(F32) for h in hs]
        self.vb = [load('v', h) for h in hs]
        self.b, self.o, self.r = [], [], []
        self.s_off = [None] * HG_HEADS

    def cumsum(self):
        for h in range(HG_HEADS):
            lf = self.load('lf', h)
            hi = lf.astype(BF16)
            lo = (lf - hi.astype(F32)).astype(BF16)
            both = _dot(self.tri, jnp.concatenate([hi, lo], axis=1))
            self.b.append((both[:, :HG_DIM] + both[:, HG_DIM:]) * LOG2E)
            yield self.WEIGHTS['cumsum']

    def state(self):
        C = HG_CHUNK
        for h in range(HG_HEADS):
            b, st = self.b[h], self.state_ref[h]
            self.o.append(_dot_nt((self.qf[h] * jnp.exp2(b)).astype(BF16), st.astype(BF16)))
            b_last = b[C - 1:C, :]
            kd = self.kk[h] * jnp.exp2(b_last - b)
            self.state_ref[h] = st * jnp.exp2(b_last) + _dot_tn(self.vb[h], kd.astype(BF16))
            yield self.WEIGHTS['state']

    def level(self, k):
        lvl, second_half, pair = self.levels[k]
        for h in range(HG_HEADS):
            b = self.b[h]
            ref = jnp.concatenate(
                [jnp.broadcast_to(b[base + lvl:base + lvl + 1, :], (2 * lvl, HG_DIM))
                 for base in range(0, HG_CHUNK, 2 * lvl)], axis=0)
            e = jnp.exp2(-jnp.abs(b - ref))
            qs = jnp.where(second_half, self.qf[h] * e, 0.0)
            ks = jnp.where(second_half, 0.0, self.kk[h] * e)
            s = _dot_nt(qs.astype(BF16), ks.astype(BF16)) * pair
            self.s_off[h] = s if self.s_off[h] is None else self.s_off[h] + s
            yield self.WEIGHTS['level']

    def off_values(self):
        for h in range(HG_HEADS):
            self.o[h] = self.o[h] + _dot(self.s_off[h].astype(BF16), self.vb[h])
            yield self.WEIGHTS['off_values']

    def diag(self):
        C, SB = HG_CHUNK, HG_SUB
        for h in range(HG_HEADS):
            b = self.b[h]
            self.c_scr[h] = b - jnp.log2(jnp.maximum(self.kk[h], 0.0))
            self.v_scr[h] = self.vb[h].astype(F32)
            scores = []
            for i in range(C // SB):
                base = i * SB
                bi = b[base:base + SB, :]
                qi = self.qf[h][base:base + SB, :]
                for j in range(SB):
                    cj = self.c_scr[h, base + j:base + j + 1, :]
                    p = qi * jnp.exp2(jnp.where(self.sub_row >= j, bi - cj, NEG_BIG))
                    scores.append(jnp.sum(p, axis=-1, keepdims=True))
            self.r.append(scores)
            yield self.WEIGHTS['diag']

    def finish(self):
        C, SB = HG_CHUNK, HG_SUB
        for h in range(HG_HEADS):
            sl = slice(h * HG_DIM, (h + 1) * HG_DIM)
            diag = []
            for i in range(C // SB):
                acc = None
                for j in range(SB):
                    term = self.r[h][i * SB + j] * self.v_scr[h, i * SB + j:i * SB + j + 1, :]
                    acc = term if acc is None else acc + term
                diag.append(acc)
            out = self.o[h] + jnp.concatenate(diag, axis=0)
            ms = jnp.mean(out * out, axis=-1, keepdims=True)
            out = out * lax.rsqrt(ms + RMS_EPS) * self.gain_ref[:, sl]
            out = out * self.load('g', h).astype(F32)
            self.o_ref[pl.ds(self.r0, C), sl] = out.astype(self.o_ref.dtype)
            yield self.WEIGHTS['finish']


PROJ_TM = 1024
PROJ_TN = 512
PROJ_RS = 128
PROJ_EDGES = tuple(c // PROJ_TN for c in (
    0, HG_WIDTH, 2 * HG_WIDTH, 3 * HG_WIDTH, 4 * HG_WIDTH, 4 * HG_WIDTH + ATT_WIDTH,
    4 * HG_WIDTH + ATT_WIDTH + 2 * KV_WIDTH, D_IN))
HGRN_SECTION = {'q': 0, 'k': 1, 'v': 2, 'g': 3}
HGRN_CHUNKS_PER_STEP = PROJ_TM // HG_CHUNK // (PROJ_EDGES[7] - PROJ_EDGES[6])
Q_SCALE = ATT_DIM ** -0.5 * LOG2E


def _rope(t, cos, sin_signed):
    w = t.shape[1]
    reps = w // cos.shape[1]
    cos_t = jnp.concatenate([cos] * reps, axis=1) if reps > 1 else cos
    sin_t = jnp.concatenate([sin_signed] * reps, axis=1) if reps > 1 else sin_signed
    lane = lax.broadcasted_iota(jnp.int32, t.shape, 1)
    first_half = (lane % ATT_DIM) < (ATT_DIM // 2)
    rot = jnp.where(first_half, pltpu.roll(t, w - ATT_DIM // 2, 1), pltpu.roll(t, ATT_DIM // 2, 1))
    return t * cos_t + rot * sin_t


def _dup_heads(t):
    lane = lax.broadcasted_iota(jnp.int32, (t.shape[0], LANES), 1)
    low = lane < ATT_DIM
    out = []
    for g in range(ATT_KV_HEADS):
        src = t[:, LANES * (g // 2):LANES * (g // 2 + 1)]
        swapped = pltpu.roll(src, ATT_DIM, 1)
        out.append(jnp.where(low, src, swapped) if g % 2 == 0 else jnp.where(low, swapped, src))
    return jnp.concatenate(out, axis=1)


def _proj_kernel(x_ref, w_ref, lbl_ref, cos_ref, sin_ref, gain_ref,
                 oa_ref, oaq_ref, okv_ref, og_ref,
                 xb_ref, hp_ref, lf_ref, state_ref, c_scr, v_scr):
    i = pl.program_id(0)
    j = pl.program_id(1)
    e = PROJ_EDGES
    tiles_per_section = HG_WIDTH // PROJ_TN
    heads_per_tile = PROJ_TN // HG_DIM

    @pl.when(j == 0)
    def _():
        xb_ref[...] = x_ref[...].astype(BF16)

    @pl.when((i == 0) & (j == 0))
    def _():
        state_ref[...] = jnp.zeros_like(state_ref)

    def segment(s, interleave=None):
        def deco(epilogue):
            @pl.when((j >= e[s]) & (j < e[s + 1]))
            def _():
                n_sub = PROJ_TM // PROJ_RS
                stages, total = interleave() if interleave is not None else (iter(()), 0)
                budget = 0.0
                for r in range(n_sub):
                    budget += total / n_sub
                    while budget > 0:
                        weight = next(stages, None)
                        if weight is None:
                            break
                        budget -= weight
                    rows = slice(r * PROJ_RS, (r + 1) * PROJ_RS)
                    epilogue(rows, _dot(xb_ref[rows, :], w_ref[...]))
                for _ in stages:
                    pass
        return deco

    @segment(0)
    def _(rows, acc):
        hp_ref[j, rows, :] = (acc * jax.nn.sigmoid(acc)).astype(BF16)

    @segment(1)
    def _(rows, acc):
        lbl = lbl_ref[...]
        lexp = jnp.exp(lbl - jnp.max(lbl, axis=0, keepdims=True))
        lb = lexp[0:1, :] / jnp.sum(lexp, axis=0, keepdims=True)
        f = lb + (1.0 - lb) * jax.nn.sigmoid(acc)
        lf_ref[j - e[1], rows, :] = jnp.log(f)
        hp_ref[j, rows, :] = (1.0 - f).astype(BF16)

    @segment(2)
    def _(rows, acc):
        hp_ref[j, rows, :] = acc.astype(BF16)

    @segment(3)
    def _(rows, acc):
        hp_ref[j, rows, :] = (acc * jax.nn.sigmoid(acc)).astype(BF16)

    @segment(4)
    def _(rows, acc):
        oaq_ref[rows, :] = (_rope(acc, cos_ref[rows, :], sin_ref[rows, :]) * Q_SCALE).astype(BF16)

    @segment(5)
    def _(rows, acc):
        k = _rope(acc[:, :KV_WIDTH], cos_ref[rows, :], sin_ref[rows, :])
        okv_ref[rows, :] = jnp.concatenate(
            [_dup_heads(k), _dup_heads(acc[:, KV_WIDTH:])], axis=1).astype(BF16)

    def hgrn_chunks():
        consts = _hgrn_consts()
        chunks = []
        for u in range(HGRN_CHUNKS_PER_STEP):
            r0 = pl.multiple_of(((j - e[6]) * HGRN_CHUNKS_PER_STEP + u) * HG_CHUNK, HG_CHUNK)

            def load(name, h, r0=r0):
                lanes = slice((h % heads_per_tile) * HG_DIM, (h % heads_per_tile + 1) * HG_DIM)
                if name == 'lf':
                    return lf_ref[h // heads_per_tile, pl.ds(r0, HG_CHUNK), lanes]
                tile = HGRN_SECTION[name] * tiles_per_section + h // heads_per_tile
                return hp_ref[tile, pl.ds(r0, HG_CHUNK), lanes]

            chunks.append(_HgrnChunk(consts, load, gain_ref, oa_ref, r0, state_ref,
                                     c_scr.at[u], v_scr.at[u]))

        def stages():
            for c in chunks:
                yield from c.cumsum()
            for c in chunks:
                yield from c.diag()
            for c in chunks:
                yield from c.state()
                for k in range(3):
                    yield from c.level(k)
                yield from c.off_values()
                yield from c.finish()

        return stages(), len(chunks) * _HgrnChunk.TOTAL_WEIGHT

    @segment(6, interleave=hgrn_chunks)
    def _(rows, acc):
        og_ref[rows, :] = jax.nn.sigmoid(acc).astype(BF16)


def _proj(x, wb, lb_logits, cos, sin_signed, gain):
    tm, tn = PROJ_TM, PROJ_TN
    m, k = x.shape
    e = PROJ_EDGES
    assert e[6] - e[5] == 1 and 2 * KV_WIDTH == tn
    assert (tm // HG_CHUNK) % (e[7] - e[6]) == 0

    def span_spec(s, width=tn):
        lo, hi = e[s], e[s + 1]
        return pl.BlockSpec((tm, width), lambda i, j: (i, jnp.clip(j - lo, 0, hi - lo - 1)))

    def sds(width, dtype):
        return jax.ShapeDtypeStruct((m, width), dtype)

    row_tile = lambda i, j: (i, 0)
    return pl.pallas_call(
        _proj_kernel,
        out_shape=(sds(HG_WIDTH, BF16), sds(ATT_WIDTH, BF16), sds(4 * KV_WIDTH, BF16),
                   sds(2 * D_MODEL, BF16)),
        grid=(m // tm, D_IN // tn),
        in_specs=[pl.BlockSpec((tm, k), row_tile),
                  pl.BlockSpec((k, tn), lambda i, j: (0, j)),
                  pl.BlockSpec((lb_logits.shape[0], tn),
                               lambda i, j: (0, jnp.clip(j - e[1], 0, e[2] - e[1] - 1))),
                  pl.BlockSpec((tm, LANES), row_tile),
                  pl.BlockSpec((tm, LANES), row_tile),
                  pl.BlockSpec((1, HG_WIDTH), lambda i, j: (0, 0))],
        out_specs=(pl.BlockSpec((tm, HG_WIDTH), row_tile), span_spec(4),
                   pl.BlockSpec((tm, 4 * KV_WIDTH), row_tile), span_spec(6)),
        scratch_shapes=[pltpu.VMEM((tm, k), BF16),
                        pltpu.VMEM((4 * HG_WIDTH // tn, tm, tn), BF16),
                        pltpu.VMEM((HG_WIDTH // tn, tm, tn), F32),
                        pltpu.VMEM((HG_HEADS, HG_DIM, HG_DIM), F32),
                        pltpu.VMEM((HGRN_CHUNKS_PER_STEP, HG_HEADS, HG_CHUNK, HG_DIM), F32),
                        pltpu.VMEM((HGRN_CHUNKS_PER_STEP, HG_HEADS, HG_CHUNK, HG_DIM), F32)],
        compiler_params=pltpu.CompilerParams(
            dimension_semantics=("arbitrary", "arbitrary"), vmem_limit_bytes=PROJ_VMEM_LIMIT),
        name="proj_hgrn2",
    )(x, wb, lb_logits, cos, sin_signed, gain)


def _attn_kernel(q_ref, kvp_ref, kvc_ref, sink_ref, *refs):
    n_side = (len(refs) - 1) // 2
    side_in, o_ref, side_out = refs[:n_side], refs[n_side], refs[n_side + 1:]
    _side_cast(side_in, side_out)
    n = pl.program_id(0)
    blk = q_ref.shape[0]
    qi = lax.broadcasted_iota(jnp.int32, (blk, 2 * blk), 0)
    ci = lax.broadcasted_iota(jnp.int32, (blk, 2 * blk), 1)
    rel = qi + blk - ci
    first_key = jnp.where(n > 0, 0, blk)
    bias1 = jnp.where(rel >= 0, jnp.where(rel < blk, jnp.where(ci >= first_key, 0.0, NEG_BIG),
                                          NEG_BIG), NEG_BIG)
    bias = jnp.concatenate([bias1] * ATT_GROUP, axis=0)
    lane = lax.broadcasted_iota(jnp.int32, (blk, LANES), 1)
    low = lane < ATT_DIM
    zero = jnp.zeros((), BF16)

    groups = range(ATT_KV_HEADS)
    pairs = range(ATT_GROUP // 2)
    s = []
    for g in groups:
        ksl = slice(g * LANES, (g + 1) * LANES)
        kd = jnp.concatenate([kvp_ref[:, ksl], kvc_ref[:, ksl]], axis=0)
        qs = []
        for pr in pairs:
            col = (g * len(pairs) + pr) * LANES
            qp = q_ref[:, col:col + LANES]
            qs += [jnp.where(low, qp, zero), jnp.where(low, zero, qp)]
        s.append(_dot_nt(jnp.concatenate(qs, axis=0), kd) + bias)

    p, denom = [], []
    for g in groups:
        sink = jnp.concatenate(
            [jnp.broadcast_to(sink_ref[g * ATT_GROUP + u:g * ATT_GROUP + u + 1, :], (blk, LANES))
             for u in range(ATT_GROUP)], axis=0) * LOG2E
        m = jnp.max(jnp.maximum(s[g][:, :blk], s[g][:, blk:]), axis=-1, keepdims=True)
        m = jnp.maximum(jnp.broadcast_to(m, (ATT_GROUP * blk, LANES)), sink)
        w = jnp.exp2(s[g] - jnp.concatenate([m, m], axis=1))
        total = jnp.sum(w[:, :blk] + w[:, blk:], axis=-1, keepdims=True)
        denom.append(jnp.broadcast_to(total, (ATT_GROUP * blk, LANES)) + jnp.exp2(sink - m))
        p.append(w.astype(BF16))

    t = []
    for g in groups:
        vsl = slice(2 * KV_WIDTH + g * LANES, 2 * KV_WIDTH + (g + 1) * LANES)
        vd = jnp.concatenate([kvp_ref[:, vsl], kvc_ref[:, vsl]], axis=0)
        t.append(_dot(p[g], vd) / denom[g])

    for g in groups:
        for pr in pairs:
            out = jnp.where(low, t[g][2 * pr * blk:(2 * pr + 1) * blk],
                            t[g][(2 * pr + 1) * blk:(2 * pr + 2) * blk])
            col = (g * len(pairs) + pr) * LANES
            o_ref[:, col:col + LANES] = out.astype(o_ref.dtype)


def _side_cast_specs(side_weights, steps):
    specs, shapes = [], []
    for w in side_weights:
        slab = w.shape[0] // steps
        assert slab * steps == w.shape[0] and slab % 16 == 0
        specs.append(pl.BlockSpec((slab, w.shape[1]), lambda n: (n, 0)))
        shapes.append(jax.ShapeDtypeStruct(w.shape, BF16))
    return specs, shapes


def _side_cast(side_in, side_out):
    for src, dst in zip(side_in, side_out):
        dst[...] = src[...].astype(BF16)


def _attn(aq, akv, sinks_b, side_weights):
    blk = WINDOW
    s = aq.shape[0]
    steps = s // blk
    side_specs, side_shapes = _side_cast_specs(side_weights, steps)
    outs = pl.pallas_call(
        _attn_kernel,
        out_shape=(jax.ShapeDtypeStruct((s, ATT_WIDTH), BF16), *side_shapes),
        grid=(steps,),
        in_specs=[pl.BlockSpec((blk, ATT_WIDTH), lambda n: (n, 0)),
                  pl.BlockSpec((blk, 4 * KV_WIDTH), lambda n: (jnp.maximum(n - 1, 0), 0)),
                  pl.BlockSpec((blk, 4 * KV_WIDTH), lambda n: (n, 0)),
                  pl.BlockSpec((ATT_HEADS, LANES), lambda n: (0, 0)), *side_specs],
        out_specs=(pl.BlockSpec((blk, ATT_WIDTH), lambda n: (n, 0)), *side_specs),
        compiler_params=pltpu.CompilerParams(
            dimension_semantics=("parallel",), vmem_limit_bytes=VMEM_LIMIT),
        name="swa",
    )(aq, akv, akv, sinks_b, *side_weights)
    return outs[0], outs[1:]


def _layer_norm(r, gain, bias):
    mu = jnp.mean(r, axis=-1, keepdims=True)
    d = r - mu
    var = jnp.mean(d * d, axis=-1, keepdims=True)
    return d * lax.rsqrt(var + LN_EPS) * gain + bias


MERGE_TM = 256
MERGE_RS = 128


def _merge_kernel(oa_ref, ob_ref, ga_ref, gb_ref, x_ref, wa_ref, wb_ref, wo_ref, g_ref, b_ref,
                  *refs):
    n_side = (len(refs) - 2) // 2
    side_in, (x1_ref, x1b_ref), side_out = refs[:n_side], refs[n_side:n_side + 2], refs[n_side + 2:]
    _side_cast(side_in, side_out)
    for r in range(MERGE_TM // MERGE_RS):
        rows = slice(r * MERGE_RS, (r + 1) * MERGE_RS)
        ya = _dot(oa_ref[rows, :], wa_ref[...])
        yb = _dot(ob_ref[rows, :], wb_ref[...])
        mixed = ga_ref[rows, :].astype(F32) * ya + gb_ref[rows, :].astype(F32) * yb
        z = _dot(mixed.astype(BF16), wo_ref[...])
        x1 = _layer_norm(ALPHA * x_ref[rows, :] + z, g_ref[...], b_ref[...])
        x1_ref[rows, :] = x1
        x1b_ref[rows, :] = x1.astype(BF16)


def _merge(oa, ob, gates, x, wa, wb, wo, g1, b1, side_weights):
    tm = MERGE_TM
    s = x.shape[0]
    row = lambda i: (i, 0)
    const = lambda i: (0, 0)
    resident = dict(pipeline_mode=pl.Buffered(1))
    side_specs, side_shapes = _side_cast_specs(side_weights, s // tm)
    outs = pl.pallas_call(
        _merge_kernel,
        out_shape=(jax.ShapeDtypeStruct((s, D_MODEL), F32), jax.ShapeDtypeStruct((s, D_MODEL), BF16),
                   *side_shapes),
        grid=(s // tm,),
        in_specs=[pl.BlockSpec((tm, HG_WIDTH), row), pl.BlockSpec((tm, ATT_WIDTH), row),
                  pl.BlockSpec((tm, D_MODEL), row), pl.BlockSpec((tm, D_MODEL), lambda i: (i, 1)),
                  pl.BlockSpec((tm, D_MODEL), row),
                  pl.BlockSpec((HG_WIDTH, D_MODEL), const, **resident),
                  pl.BlockSpec((ATT_WIDTH, D_MODEL), const, **resident),
                  pl.BlockSpec((D_MODEL, D_MODEL), const, **resident),
                  pl.BlockSpec((1, D_MODEL), const), pl.BlockSpec((1, D_MODEL), const),
                  *side_specs],
        out_specs=(pl.BlockSpec((tm, D_MODEL), row), pl.BlockSpec((tm, D_MODEL), row),
                   *side_specs),
        compiler_params=pltpu.CompilerParams(
            dimension_semantics=("parallel",), vmem_limit_bytes=VMEM_LIMIT),
        name="merge_ln1",
    )(oa, ob, gates, gates, x, wa, wb, wo, g1, b1, *side_weights)
    return outs[0], outs[1], outs[2:]


def _ffn_kernel(xb_ref, x_ref, w1_ref, w2_ref, g_ref, b_ref, o_ref, acc_ref):
    j = pl.program_id(1)

    @pl.when(j == 0)
    def _():
        acc_ref[...] = jnp.zeros_like(acc_ref)

    h = jnp.maximum(_dot(xb_ref[...], w1_ref[...]), 0.0)
    acc_ref[...] += _dot((h * h).astype(BF16), w2_ref[...])

    @pl.when(j == pl.num_programs(1) - 1)
    def _():
        o_ref[...] = _layer_norm(ALPHA * x_ref[...] + acc_ref[...], g_ref[...], b_ref[...])


def _ffn(x1b, x1, w1, w2, g2, b2):
    tm, tf = 512, 512
    s = x1.shape[0]
    return pl.pallas_call(
        _ffn_kernel,
        out_shape=jax.ShapeDtypeStruct((s, D_MODEL), F32),
        grid=(s // tm, D_FF // tf),
        in_specs=[pl.BlockSpec((tm, D_MODEL), lambda i, j: (i, 0)),
                  pl.BlockSpec((tm, D_MODEL), lambda i, j: (i, 0)),
                  pl.BlockSpec((D_MODEL, tf), lambda i, j: (0, j)),
                  pl.BlockSpec((tf, D_MODEL), lambda i, j: (j, 0)),
                  pl.BlockSpec((1, D_MODEL), lambda i, j: (0, 0)),
                  pl.BlockSpec((1, D_MODEL), lambda i, j: (0, 0))],
        out_specs=pl.BlockSpec((tm, D_MODEL), lambda i, j: (i, 0)),
        scratch_shapes=[pltpu.VMEM((tm, D_MODEL), F32)],
        compiler_params=pltpu.CompilerParams(
            dimension_semantics=("parallel", "arbitrary"), vmem_limit_bytes=VMEM_LIMIT),
        name="ffn_ln2",
    )(x1b, x1, w1, w2, g2, b2)


def _rope_tables(seq):
    half = ATT_DIM // 2
    inv = ROPE_THETA ** (-np.arange(half, dtype=np.float64) / half)
    ang = np.arange(seq, dtype=np.float64)[:, None] * inv[None, :]
    cos, sin = np.cos(ang), np.sin(ang)
    cos_t = np.concatenate([cos, cos, cos, cos], axis=1).astype(np.float32)
    sin_t = np.concatenate([-sin, sin, -sin, sin], axis=1).astype(np.float32)
    return jnp.asarray(cos_t), jnp.asarray(sin_t)


def kernel(x, w_in, hg_lb_logits, hg_norm_gain, attn_sinks, w_branch_a, w_branch_b, w_out,
           ln1_gain, ln1_bias, w_ff1, w_ff2, ln2_gain, ln2_bias):
    b, s, d = x.shape
    assert (b, s, d) == (1, SEQ, D_MODEL) and w_in.shape == (1, D_MODEL, D_IN)
    x2 = x.reshape(s, d)
    cos_t, sin_t = _rope_tables(s)
    oa, aq, akv, gates = _proj(x2, w_in[0].astype(BF16), hg_lb_logits, cos_t, sin_t, hg_norm_gain)
    sinks_b = jnp.broadcast_to(attn_sinks[0][:, None], (ATT_HEADS, LANES))
    ob, (wa, wb, wo) = _attn(aq, akv, sinks_b, (w_branch_a[0], w_branch_b[0], w_out[0]))
    x1, x1b, (w1, w2) = _merge(oa, ob, gates, x2, wa, wb, wo, ln1_gain, ln1_bias,
                               (w_ff1[0], w_ff2[0]))
    out = _ffn(x1b, x1, w1, w2, ln2_gain, ln2_bias)
    return out.reshape(b, s, d)
```

```python
import math

import jax
import jax.numpy as jnp
import numpy as np
from jax import lax
from jax.experimental import pallas as pl
from jax.experimental.pallas import tpu as pltpu

F32 = jnp.float32
BF16 = jnp.bfloat16

D_MODEL = 2048
SEQ = 8192
HG_HEADS = 8
HG_DIM = 128
HG_WIDTH = HG_HEADS * HG_DIM
ATT_HEADS = 16
ATT_KV_HEADS = 4
ATT_DIM = 64
ATT_GROUP = ATT_HEADS // ATT_KV_HEADS
ATT_WIDTH = ATT_HEADS * ATT_DIM
KV_WIDTH = ATT_KV_HEADS * ATT_DIM
WINDOW = 128
ROPE_THETA = 10000.0
D_FF = 4 * D_MODEL
D_IN = 4 * HG_WIDTH + ATT_WIDTH + 2 * KV_WIDTH + 2 * D_MODEL
ALPHA = 2.0 ** 0.25
LN_EPS = 1e-5
RMS_EPS = 1e-6
LOG2E = math.log2(math.e)

LANES = 128
HG_CHUNK = 64
HG_SUB = 8
NEG_BIG = -1e30

VMEM_LIMIT = 56 * 1024 * 1024
PROJ_VMEM_LIMIT = 60 * 1024 * 1024


def _dot(a, b):
    return jnp.dot(a, b, preferred_element_type=F32)


def _dot_nt(a, b):
    return lax.dot_general(a, b, (((1,), (1,)), ((), ())), preferred_element_type=F32)


def _dot_tn(a, b):
    return lax.dot_general(a, b, (((0,), (0,)), ((), ())), preferred_element_type=F32)


def _hgrn_consts():
    C, SB = HG_CHUNK, HG_SUB
    ri = lax.broadcasted_iota(jnp.int32, (C, C), 0)
    ci = lax.broadcasted_iota(jnp.int32, (C, C), 1)
    tri = (ri >= ci).astype(BF16)
    row = lax.broadcasted_iota(jnp.int32, (C, 1), 0)
    sub_row = lax.broadcasted_iota(jnp.int32, (SB, 1), 0)
    levels = []
    half = C // 2
    while half >= SB:
        pair = ((ri // (2 * half)) == (ci // (2 * half))) & ((ri % (2 * half)) >= half) \
            & ((ci % (2 * half)) < half)
        levels.append((half, (row % (2 * half)) >= half, jnp.where(pair, 1.0, 0.0)))
        half //= 2
    return tri, sub_row, levels


class _HgrnChunk:
    WEIGHTS = dict(cumsum=8, diag=95, state=20, level=15, off_values=3, finish=40)
    TOTAL_WEIGHT = HG_HEADS * (8 + 95 + 20 + 3 * 15 + 3 + 40)

    def __init__(self, consts, load, gain_ref, o_ref, r0, state_ref, c_scr, v_scr):
        self.tri, self.sub_row, self.levels = consts
        assert len(self.levels) == 3
        self.load, self.gain_ref, self.o_ref, self.r0 = load, gain_ref, o_ref, r0
        self.state_ref, self.c_scr, self.v_scr = state_ref, c_scr, v_scr
        hs = range(HG_HEADS)
        self.qf = [load('q', h).astype(F32) for h in hs]
        self.kk = [load('k', h).astype(F32) for h in hs]
        self.vb = [load('v', h) for h in hs]
        self.b, self.o, self.r = [], [], []
        self.s_off = [None] * HG_HEADS

    def cumsum(self):
        for h in range(HG_HEADS):
            lf = self.load('lf', h)
            hi = lf.astype(BF16)
            lo = (lf - hi.astype(F32)).astype(BF16)
            both = _dot(self.tri, jnp.concatenate([hi, lo], axis=1))
            self.b.append((both[:, :HG_DIM] + both[:, HG_DIM:]) * LOG2E)
            yield self.WEIGHTS['cumsum']

    def state(self):
        C = HG_CHUNK
        for h in range(HG_HEADS):
            b, st = self.b[h], self.state_ref[h]
            self.o.append(_dot_nt((self.qf[h] * jnp.exp2(b)).astype(BF16), st.astype(BF16)))
            b_last = b[C - 1:C, :]
            kd = self.kk[h] * jnp.exp2(b_last - b)
            self.state_ref[h] = st * jnp.exp2(b_last) + _dot_tn(self.vb[h], kd.astype(BF16))
            yield self.WEIGHTS['state']

    def level(self, k):
        lvl, second_half, pair = self.levels[k]
        for h in range(HG_HEADS):
            b = self.b[h]
            ref = jnp.concatenate(
                [jnp.broadcast_to(b[base + lvl:base + lvl + 1, :], (2 * lvl, HG_DIM))
                 for base in range(0, HG_CHUNK, 2 * lvl)], axis=0)
            e = jnp.exp2(-jnp.abs(b - ref))
            qs = jnp.where(second_half, self.qf[h] * e, 0.0)
            ks = jnp.where(second_half, 0.0, self.kk[h] * e)
            s = _dot_nt(qs.astype(BF16), ks.astype(BF16)) * pair
            self.s_off[h] = s if self.s_off[h] is None else self.s_off[h] + s
            yield self.WEIGHTS['level']

    def off_values(self):
        for h in range(HG_HEADS):
            self.o[h] = self.o[h] + _dot(self.s_off[h].astype(BF16), self.vb[h])
            yield self.WEIGHTS['off_values']

    def diag(self):
        C, SB = HG_CHUNK, HG_SUB
        for h in range(HG_HEADS):
            b = self.b[h]
            self.c_scr[h] = b - jnp.log2(jnp.maximum(self.kk[h], 0.0))
            self.v_scr[h] = self.vb[h].astype(F32)
            scores = []
            for i in range(C // SB):
                base = i * SB
                bi = b[base:base + SB, :]
                qi = self.qf[h][base:base + SB, :]
                for j in range(SB):
                    cj = self.c_scr[h, base + j:base + j + 1, :]
                    p = qi * jnp.exp2(jnp.where(self.sub_row >= j, bi - cj, NEG_BIG))
                    scores.append(jnp.sum(p, axis=-1, keepdims=True))
            self.r.append(scores)
            yield self.WEIGHTS['diag']

    def finish(self):
        C, SB = HG_CHUNK, HG_SUB
        for h in range(HG_HEADS):
            sl = slice(h * HG_DIM, (h + 1) * HG_DIM)
            diag = []
            for i in range(C // SB):
                acc = None
                for j in range(SB):
                    term = self.r[h][i * SB + j] * self.v_scr[h, i * SB + j:i * SB + j + 1, :]
                    acc = term if acc is None else acc + term
                diag.append(acc)
            out = self.o[h] + jnp.concatenate(diag, axis=0)
            ms = jnp.mean(out * out, axis=-1, keepdims=True)
            out = out * lax.rsqrt(ms + RMS_EPS) * self.gain_ref[:, sl]
            out = out * self.load('g', h).astype(F32)
            self.o_ref[pl.ds(self.r0, C), sl] = out.astype(self.o_ref.dtype)
            yield self.WEIGHTS['finish']


PROJ_TM = 1024
PROJ_TN = 512
PROJ_RS = 128
PROJ_EDGES = tuple(c // PROJ_TN for c in (
    0, HG_WIDTH, 2 * HG_WIDTH, 3 * HG_WIDTH, 4 * HG_WIDTH, 4 * HG_WIDTH + ATT_WIDTH,
    4 * HG_WIDTH + ATT_WIDTH + 2 * KV_WIDTH, D_IN))
HGRN_SECTION = {'q': 0, 'k': 1, 'v': 2, 'g': 3}
HGRN_CHUNKS_PER_STEP = PROJ_TM // HG_CHUNK // (PROJ_EDGES[7] - PROJ_EDGES[6])
Q_SCALE = ATT_DIM ** -0.5 * LOG2E


def _rope(t, cos, sin_signed):
    w = t.shape[1]
    reps = w // cos.shape[1]
    cos_t = jnp.concatenate([cos] * reps, axis=1) if reps > 1 else cos
    sin_t = jnp.concatenate([sin_signed] * reps, axis=1) if reps > 1 else sin_signed
    lane = lax.broadcasted_iota(jnp.int32, t.shape, 1)
    first_half = (lane % ATT_DIM) < (ATT_DIM // 2)
    rot = jnp.where(first_half, pltpu.roll(t, w - ATT_DIM // 2, 1), pltpu.roll(t, ATT_DIM // 2, 1))
    return t * cos_t + rot * sin_t


def _dup_heads(t):
    lane = lax.broadcasted_iota(jnp.int32, (t.shape[0], LANES), 1)
    low = lane < ATT_DIM
    out = []
    for g in range(ATT_KV_HEADS):
        src = t[:, LANES * (g // 2):LANES * (g // 2 + 1)]
        swapped = pltpu.roll(src, ATT_DIM, 1)
        out.append(jnp.where(low, src, swapped) if g % 2 == 0 else jnp.where(low, swapped, src))
    return jnp.concatenate(out, axis=1)


def _proj_kernel(x_ref, w_ref, lbl_ref, cos_ref, sin_ref, gain_ref,
                 oa_ref, oaq_ref, okv_ref, og_ref,
                 xb_ref, hp_ref, lf_ref, state_ref, c_scr, v_scr):
    i = pl.program_id(0)
    j = pl.program_id(1)
    e = PROJ_EDGES
    tiles_per_section = HG_WIDTH // PROJ_TN
    heads_per_tile = PROJ_TN // HG_DIM

    @pl.when(j == 0)
    def _():
        xb_ref[...] = x_ref[...].astype(BF16)

    @pl.when((i == 0) & (j == 0))
    def _():
        state_ref[...] = jnp.zeros_like(state_ref)

    def segment(s, interleave=None, rs=PROJ_RS):
        def deco(epilogue):
            @pl.when((j >= e[s]) & (j < e[s + 1]))
            def _():
                n_sub = PROJ_TM // rs
                todo = iter(range(n_sub))

                def sub_block():
                    r = next(todo)
                    rows = slice(r * rs, (r + 1) * rs)
                    epilogue(rows, _dot(xb_ref[rows, :], w_ref[...]))

                if interleave is None:
                    for _ in range(n_sub):
                        sub_block()
                else:
                    interleave(sub_block, n_sub)
                    assert next(todo, None) is None
        return deco

    @segment(0)
    def _(rows, acc):
        hp_ref[j, rows, :] = (acc * jax.nn.sigmoid(acc)).astype(BF16)

    @segment(1)
    def _(rows, acc):
        lbl = lbl_ref[...]
        lexp = jnp.exp(lbl - jnp.max(lbl, axis=0, keepdims=True))
        lb = lexp[0:1, :] / jnp.sum(lexp, axis=0, keepdims=True)
        f = lb + (1.0 - lb) * jax.nn.sigmoid(acc)
        lf_ref[j - e[1], rows, :] = jnp.log(f)
        hp_ref[j, rows, :] = (1.0 - f).astype(BF16)

    @segment(2)
    def _(rows, acc):
        hp_ref[j, rows, :] = acc.astype(BF16)

    @segment(3)
    def _(rows, acc):
        hp_ref[j, rows, :] = (acc * jax.nn.sigmoid(acc)).astype(BF16)

    @segment(4)
    def _(rows, acc):
        oaq_ref[rows, :] = (_rope(acc, cos_ref[rows, :], sin_ref[rows, :]) * Q_SCALE).astype(BF16)

    @segment(5)
    def _(rows, acc):
        k = _rope(acc[:, :KV_WIDTH], cos_ref[rows, :], sin_ref[rows, :])
        okv_ref[rows, :] = jnp.concatenate(
            [_dup_heads(k), _dup_heads(acc[:, KV_WIDTH:])], axis=1).astype(BF16)

    def hgrn_chunks(sub_block, n_sub):
        consts = _hgrn_consts()
        chunks = []
        for u in range(HGRN_CHUNKS_PER_STEP):
            r0 = pl.multiple_of(((j - e[6]) * HGRN_CHUNKS_PER_STEP + u) * HG_CHUNK, HG_CHUNK)

            def load(name, h, r0=r0):
                lanes = slice((h % heads_per_tile) * HG_DIM, (h % heads_per_tile + 1) * HG_DIM)
                if name == 'lf':
                    return lf_ref[h // heads_per_tile, pl.ds(r0, HG_CHUNK), lanes]
                tile = HGRN_SECTION[name] * tiles_per_section + h // heads_per_tile
                return hp_ref[tile, pl.ds(r0, HG_CHUNK), lanes]

            chunks.append(_HgrnChunk(consts, load, gain_ref, oa_ref, r0, state_ref,
                                     c_scr.at[u], v_scr.at[u]))

        def stages():
            for c in chunks:
                yield from c.cumsum()
            for c in chunks:
                yield from c.diag()
            for c in chunks:
                yield from c.state()
                for k in range(3):
                    yield from c.level(k)
                yield from c.off_values()
                yield from c.finish()

        todo, budget = stages(), 0.0
        for _ in range(n_sub):
            budget += len(chunks) * _HgrnChunk.TOTAL_WEIGHT / n_sub
            while budget > 0:
                weight = next(todo, None)
                if weight is None:
                    break
                budget -= weight
            sub_block()
        for _ in todo:
            pass

    @segment(6, interleave=hgrn_chunks, rs=PROJ_RS * 2)
    def _(rows, acc):
        og_ref[rows, :] = jax.nn.sigmoid(acc).astype(BF16)


def _proj(x, wb, lb_logits, cos, sin_signed, gain):
    tm, tn = PROJ_TM, PROJ_TN
    m, k = x.shape
    e = PROJ_EDGES
    assert e[6] - e[5] == 1 and 2 * KV_WIDTH == tn
    assert (tm // HG_CHUNK) % (e[7] - e[6]) == 0

    def span_spec(s, width=tn):
        lo, hi = e[s], e[s + 1]
        return pl.BlockSpec((tm, width), lambda i, j: (i, jnp.clip(j - lo, 0, hi - lo - 1)))

    def sds(width, dtype):
        return jax.ShapeDtypeStruct((m, width), dtype)

    row_tile = lambda i, j: (i, 0)
    return pl.pallas_call(
        _proj_kernel,
        out_shape=(sds(HG_WIDTH, BF16), sds(ATT_WIDTH, BF16), sds(4 * KV_WIDTH, BF16),
                   sds(2 * D_MODEL, BF16)),
        grid=(m // tm, D_IN // tn),
        in_specs=[pl.BlockSpec((tm, k), row_tile),
                  pl.BlockSpec((k, tn), lambda i, j: (0, j)),
                  pl.BlockSpec((lb_logits.shape[0], tn),
                               lambda i, j: (0, jnp.clip(j - e[1], 0, e[2] - e[1] - 1))),
                  pl.BlockSpec((tm, LANES), row_tile),
                  pl.BlockSpec((tm, LANES), row_tile),
                  pl.BlockSpec((1, HG_WIDTH), lambda i, j: (0, 0))],
        out_specs=(pl.BlockSpec((tm, HG_WIDTH), row_tile), span_spec(4),
                   pl.BlockSpec((tm, 4 * KV_WIDTH), row_tile), span_spec(6)),
        scratch_shapes=[pltpu.VMEM((tm, k), BF16),
                        pltpu.VMEM((4 * HG_WIDTH // tn, tm, tn), BF16),
                        pltpu.VMEM((HG_WIDTH // tn, tm, tn), F32),
                        pltpu.VMEM((HG_HEADS, HG_DIM, HG_DIM), F32),
                        pltpu.VMEM((HGRN_CHUNKS_PER_STEP, HG_HEADS, HG_CHUNK, HG_DIM), F32),
                        pltpu.VMEM((HGRN_CHUNKS_PER_STEP, HG_HEADS, HG_CHUNK, HG_DIM), F32)],
        compiler_params=pltpu.CompilerParams(
            dimension_semantics=("arbitrary", "arbitrary"), vmem_limit_bytes=PROJ_VMEM_LIMIT),
        name="proj_hgrn2",
    )(x, wb, lb_logits, cos, sin_signed, gain)


def _attn_kernel(q_ref, kvp_ref, kvc_ref, sink_ref, *refs):
    n_side = (len(refs) - 1) // 2
    side_in, o_ref, side_out = refs[:n_side], refs[n_side], refs[n_side + 1:]
    _side_cast(side_in, side_out)
    n = pl.program_id(0)
    blk = WINDOW
    nblk = q_ref.shape[0] // blk
    qi = lax.broadcasted_iota(jnp.int32, (blk, 2 * blk), 0)
    ci = lax.broadcasted_iota(jnp.int32, (blk, 2 * blk), 1)
    rel = qi + blk - ci
    band = (rel >= 0) & (rel < blk)
    lane = lax.broadcasted_iota(jnp.int32, (blk, LANES), 1)
    low = lane < ATT_DIM
    zero = jnp.zeros((), BF16)

    def keys_values(t, lanes):
        own = kvc_ref[t * blk:(t + 1) * blk, lanes]
        prev = kvp_ref[:, lanes] if t == 0 else kvc_ref[(t - 1) * blk:t * blk, lanes]
        return jnp.concatenate([prev, own], axis=0)

    units = [(t, g) for t in range(nblk) for g in range(ATT_KV_HEADS)]
    pairs = range(ATT_GROUP // 2)
    s = []
    for t, g in units:
        first_key = jnp.where(n > 0, 0, blk) if t == 0 else 0
        bias1 = jnp.where(band & (ci >= first_key), 0.0, NEG_BIG)
        bias = jnp.concatenate([bias1] * ATT_GROUP, axis=0)
        kd = keys_values(t, slice(g * LANES, (g + 1) * LANES))
        qs = []
        for pr in pairs:
            col = (g * len(pairs) + pr) * LANES
            qp = q_ref[t * blk:(t + 1) * blk, col:col + LANES]
            qs += [jnp.where(low, qp, zero), jnp.where(low, zero, qp)]
        s.append(_dot_nt(jnp.concatenate(qs, axis=0), kd) + bias)

    p, denom = [], []
    for u, (t, g) in enumerate(units):
        sink = jnp.concatenate(
            [jnp.broadcast_to(sink_ref[g * ATT_GROUP + h:g * ATT_GROUP + h + 1, :], (blk, LANES))
             for h in range(ATT_GROUP)], axis=0) * LOG2E
        m = jnp.max(jnp.maximum(s[u][:, :blk], s[u][:, blk:]), axis=-1, keepdims=True)
        m = jnp.maximum(jnp.broadcast_to(m, (ATT_GROUP * blk, LANES)), sink)
        w = jnp.exp2(s[u] - jnp.concatenate([m, m], axis=1))
        total = jnp.sum(w[:, :blk] + w[:, blk:], axis=-1, keepdims=True)
        denom.append(jnp.broadcast_to(total, (ATT_GROUP * blk, LANES)) + jnp.exp2(sink - m))
        p.append(w.astype(BF16))

    o = []
    for u, (t, g) in enumerate(units):
        vd = keys_values(t, slice(2 * KV_WIDTH + g * LANES, 2 * KV_WIDTH + (g + 1) * LANES))
        o.append(_dot(p[u], vd) / denom[u])

    for u, (t, g) in enumerate(units):
        for pr in pairs:
            out = jnp.where(low, o[u][2 * pr * blk:(2 * pr + 1) * blk],
                            o[u][(2 * pr + 1) * blk:(2 * pr + 2) * blk])
            col = (g * len(pairs) + pr) * LANES
            o_ref[t * blk:(t + 1) * blk, col:col + LANES] = out.astype(o_ref.dtype)


def _side_cast_specs(side_weights, steps):
    specs, shapes = [], []
    for w in side_weights:
        slab = w.shape[0] // steps
        assert slab * steps == w.shape[0] and slab % 16 == 0
        specs.append(pl.BlockSpec((slab, w.shape[1]), lambda n: (n, 0)))
        shapes.append(jax.ShapeDtypeStruct(w.shape, BF16))
    return specs, shapes


def _side_cast(side_in, side_out):
    for src, dst in zip(side_in, side_out):
        dst[...] = src[...].astype(BF16)


ATT_BLOCKS_PER_STEP = 2


def _attn(aq, akv, sinks_b, side_weights):
    blk, nblk = WINDOW, ATT_BLOCKS_PER_STEP
    s = aq.shape[0]
    steps = s // (blk * nblk)
    side_specs, side_shapes = _side_cast_specs(side_weights, steps)
    outs = pl.pallas_call(
        _attn_kernel,
        out_shape=(jax.ShapeDtypeStruct((s, ATT_WIDTH), BF16), *side_shapes),
        grid=(steps,),
        in_specs=[pl.BlockSpec((nblk * blk, ATT_WIDTH), lambda n: (n, 0)),
                  pl.BlockSpec((blk, 4 * KV_WIDTH), lambda n: (jnp.maximum(nblk * n - 1, 0), 0)),
                  pl.BlockSpec((nblk * blk, 4 * KV_WIDTH), lambda n: (n, 0)),
                  pl.BlockSpec((ATT_HEADS, LANES), lambda n: (0, 0)), *side_specs],
        out_specs=(pl.BlockSpec((nblk * blk, ATT_WIDTH), lambda n: (n, 0)), *side_specs),
        compiler_params=pltpu.CompilerParams(
            dimension_semantics=("parallel",), vmem_limit_bytes=VMEM_LIMIT),
        name="swa",
    )(aq, akv, akv, sinks_b, *side_weights)
    return outs[0], outs[1:]


def _layer_norm(r, gain, bias):
    mu = jnp.mean(r, axis=-1, keepdims=True)
    d = r - mu
    var = jnp.mean(d * d, axis=-1, keepdims=True)
    return d * lax.rsqrt(var + LN_EPS) * gain + bias


MERGE_TM = 256
MERGE_RS = 128


def _merge_kernel(oa_ref, ob_ref, ga_ref, gb_ref, x_ref, wa_ref, wb_ref, wo_ref, g_ref, b_ref,
                  *refs):
    n_side = (len(refs) - 2) // 2
    side_in, (x1_ref, x1b_ref), side_out = refs[:n_side], refs[n_side:n_side + 2], refs[n_side + 2:]
    _side_cast(side_in, side_out)
    for r in range(MERGE_TM // MERGE_RS):
        rows = slice(r * MERGE_RS, (r + 1) * MERGE_RS)
        ya = _dot(oa_ref[rows, :], wa_ref[...])
        yb = _dot(ob_ref[rows, :], wb_ref[...])
        mixed = ga_ref[rows, :].astype(F32) * ya + gb_ref[rows, :].astype(F32) * yb
        z = _dot(mixed.astype(BF16), wo_ref[...])
        x1 = _layer_norm(ALPHA * x_ref[rows, :] + z, g_ref[...], b_ref[...])
        x1_ref[rows, :] = x1
        x1b_ref[rows, :] = x1.astype(BF16)


def _merge(oa, ob, gates, x, wa, wb, wo, g1, b1, side_weights):
    tm = MERGE_TM
    s = x.shape[0]
    row = lambda i: (i, 0)
    const = lambda i: (0, 0)
    resident = dict(pipeline_mode=pl.Buffered(1))
    side_specs, side_shapes = _side_cast_specs(side_weights, s // tm)
    outs = pl.pallas_call(
        _merge_kernel,
        out_shape=(jax.ShapeDtypeStruct((s, D_MODEL), F32), jax.ShapeDtypeStruct((s, D_MODEL), BF16),
                   *side_shapes),
        grid=(s // tm,),
        in_specs=[pl.BlockSpec((tm, HG_WIDTH), row), pl.BlockSpec((tm, ATT_WIDTH), row),
                  pl.BlockSpec((tm, D_MODEL), row), pl.BlockSpec((tm, D_MODEL), lambda i: (i, 1)),
                  pl.BlockSpec((tm, D_MODEL), row),
                  pl.BlockSpec((HG_WIDTH, D_MODEL), const, **resident),
                  pl.BlockSpec((ATT_WIDTH, D_MODEL), const, **resident),
                  pl.BlockSpec((D_MODEL, D_MODEL), const, **resident),
                  pl.BlockSpec((1, D_MODEL), const), pl.BlockSpec((1, D_MODEL), const),
                  *side_specs],
        out_specs=(pl.BlockSpec((tm, D_MODEL), row), pl.BlockSpec((tm, D_MODEL), row),
                   *side_specs),
        compiler_params=pltpu.CompilerParams(
            dimension_semantics=("parallel",), vmem_limit_bytes=VMEM_LIMIT),
        name="merge_ln1",
    )(oa, ob, gates, gates, x, wa, wb, wo, g1, b1, *side_weights)
    return outs[0], outs[1], outs[2:]


FFN_TM = 512
FFN_TF = 512
FFN_RS = 256


def _ffn_kernel(xb_ref, x_ref, w1_ref, w2_ref, g_ref, b_ref, o_ref, acc_ref):
    j = pl.program_id(1)
    last = pl.num_programs(1) - 1

    def partial(rows):
        h = jnp.maximum(_dot(xb_ref[rows, :], w1_ref[...]), 0.0)
        return _dot((h * h).astype(BF16), w2_ref[...])

    @pl.when(j == 0)
    def _():
        acc_ref[...] = partial(slice(None))

    @pl.when((j > 0) & (j < last))
    def _():
        acc_ref[...] += partial(slice(None))

    @pl.when(j == last)
    def _():
        for r in range(FFN_TM // FFN_RS):
            rows = slice(r * FFN_RS, (r + 1) * FFN_RS)
            y = ALPHA * x_ref[rows, :] + acc_ref[rows, :] + partial(rows)
            o_ref[rows, :] = _layer_norm(y, g_ref[...], b_ref[...])


def _ffn(x1b, x1, w1, w2, g2, b2):
    tm, tf = FFN_TM, FFN_TF
    assert D_FF // tf >= 2
    s = x1.shape[0]
    return pl.pallas_call(
        _ffn_kernel,
        out_shape=jax.ShapeDtypeStruct((s, D_MODEL), F32),
        grid=(s // tm, D_FF // tf),
        in_specs=[pl.BlockSpec((tm, D_MODEL), lambda i, j: (i, 0)),
                  pl.BlockSpec((tm, D_MODEL), lambda i, j: (i, 0)),
                  pl.BlockSpec((D_MODEL, tf), lambda i, j: (0, j)),
                  pl.BlockSpec((tf, D_MODEL), lambda i, j: (j, 0)),
                  pl.BlockSpec((1, D_MODEL), lambda i, j: (0, 0)),
                  pl.BlockSpec((1, D_MODEL), lambda i, j: (0, 0))],
        out_specs=pl.BlockSpec((tm, D_MODEL), lambda i, j: (i, 0)),
        scratch_shapes=[pltpu.VMEM((tm, D_MODEL), F32)],
        compiler_params=pltpu.CompilerParams(
            dimension_semantics=("parallel", "arbitrary"), vmem_limit_bytes=VMEM_LIMIT),
        name="ffn_ln2",
    )(x1b, x1, w1, w2, g2, b2)


def _rope_tables(seq):
    half = ATT_DIM // 2
    inv = ROPE_THETA ** (-np.arange(half, dtype=np.float64) / half)
    ang = np.arange(seq, dtype=np.float64)[:, None] * inv[None, :]
    cos, sin = np.cos(ang), np.sin(ang)
    cos_t = np.concatenate([cos, cos, cos, cos], axis=1).astype(np.float32)
    sin_t = np.concatenate([-sin, sin, -sin, sin], axis=1).astype(np.float32)
    return jnp.asarray(cos_t), jnp.asarray(sin_t)


def kernel(x, w_in, hg_lb_logits, hg_norm_gain, attn_sinks, w_branch_a, w_branch_b, w_out,
           ln1_gain, ln1_bias, w_ff1, w_ff2, ln2_gain, ln2_bias):
    b, s, d = x.shape
    assert (b, s, d) == (1, SEQ, D_MODEL) and w_in.shape == (1, D_MODEL, D_IN)
    x2 = x.reshape(s, d)
    cos_t, sin_t = _rope_tables(s)
    oa, aq, akv, gates = _proj(x2, w_in[0].astype(BF16), hg_lb_logits, cos_t, sin_t, hg_norm_gain)
    sinks_b = jnp.broadcast_to(attn_sinks[0][:, None], (ATT_HEADS, LANES))
    ob, (wa, wb, wo) = _attn(aq, akv, sinks_b, (w_branch_a[0], w_branch_b[0], w_out[0]))
    x1, x1b, (w1, w2) = _merge(oa, ob, gates, x2, wa, wb, wo, ln1_gain, ln1_bias,
                               (w_ff1[0], w_ff2[0]))
    out = _ffn(x1b, x1, w1, w2, ln2_gain, ln2_bias)
    return out.reshape(b, s, d)
```

```python
import math

import jax
import jax.numpy as jnp
import numpy as np
from jax import lax
from jax.experimental import pallas as pl
from jax.experimental.pallas import tpu as pltpu

F32 = jnp.float32
BF16 = jnp.bfloat16

D_MODEL = 2048
SEQ = 8192
HG_HEADS = 8
HG_DIM = 128
HG_WIDTH = HG_HEADS * HG_DIM
ATT_HEADS = 16
ATT_KV_HEADS = 4
ATT_DIM = 64
ATT_GROUP = ATT_HEADS // ATT_KV_HEADS
ATT_WIDTH = ATT_HEADS * ATT_DIM
KV_WIDTH = ATT_KV_HEADS * ATT_DIM
WINDOW = 128
ROPE_THETA = 10000.0
D_FF = 4 * D_MODEL
D_IN = 4 * HG_WIDTH + ATT_WIDTH + 2 * KV_WIDTH + 2 * D_MODEL
ALPHA = 2.0 ** 0.25
LN_EPS = 1e-5
RMS_EPS = 1e-6
LOG2E = math.log2(math.e)

LANES = 128
HG_CHUNK = 64
HG_SUB = 8
NEG_BIG = -1e30

VMEM_LIMIT = 56 * 1024 * 1024
PROJ_VMEM_LIMIT = 60 * 1024 * 1024


def _dot(a, b):
    return jnp.dot(a, b, preferred_element_type=F32)


def _dot_nt(a, b):
    return lax.dot_general(a, b, (((1,), (1,)), ((), ())), preferred_element_type=F32)


def _dot_tn(a, b):
    return lax.dot_general(a, b, (((0,), (0,)), ((), ())), preferred_element_type=F32)


def _hgrn_consts():
    C, SB = HG_CHUNK, HG_SUB
    ri = lax.broadcasted_iota(jnp.int32, (C, C), 0)
    ci = lax.broadcasted_iota(jnp.int32, (C, C), 1)
    tri = (ri >= ci).astype(BF16)
    row = lax.broadcasted_iota(jnp.int32, (C, 1), 0)
    sub_row = lax.broadcasted_iota(jnp.int32, (SB, 1), 0)
    levels = []
    half = C // 2
    while half >= SB:
        pair = ((ri // (2 * half)) == (ci // (2 * half))) & ((ri % (2 * half)) >= half) \
            & ((ci % (2 * half)) < half)
        levels.append((half, (row % (2 * half)) >= half, jnp.where(pair, 1.0, 0.0)))
        half //= 2
    return tri, sub_row, levels


class _HgrnChunk:
    WEIGHTS = dict(cumsum=8, diag=95, state=20, level=15, off_values=3, finish=40)
    TOTAL_WEIGHT = HG_HEADS * (8 + 95 + 20 + 3 * 15 + 3 + 40)

    def __init__(self, consts, load, gain_ref, o_ref, r0, state_ref, c_scr, v_scr):
        self.tri, self.sub_row, self.levels = consts
        assert len(self.levels) == 3
        self.load, self.gain_ref, self.o_ref, self.r0 = load, gain_ref, o_ref, r0
        self.state_ref, self.c_scr, self.v_scr = state_ref, c_scr, v_scr
        hs = range(HG_HEADS)
        self.qf = [load('q', h).astype(F32) for h in hs]
        self.kk = [load('k', h).astype(F32) for h in hs]
        self.vb = [load('v', h) for h in hs]
        self.b, self.o, self.r = [], [], []
        self.s_off = [None] * HG_HEADS

    def cumsum(self):
        for h in range(HG_HEADS):
            lf = self.load('lf', h)
            hi = lf.astype(BF16)
            lo = (lf - hi.astype(F32)).astype(BF16)
            both = _dot(self.tri, jnp.concatenate([hi, lo], axis=1))
            self.b.append((both[:, :HG_DIM] + both[:, HG_DIM:]) * LOG2E)
            yield self.WEIGHTS['cumsum']

    def state(self):
        C = HG_CHUNK
        for h in range(HG_HEADS):
            b, st = self.b[h], self.state_ref[h]
            self.o.append(_dot_nt((self.qf[h] * jnp.exp2(b)).astype(BF16), st.astype(BF16)))
            b_last = b[C - 1:C, :]
            kd = self.kk[h] * jnp.exp2(b_last - b)
            self.state_ref[h] = st * jnp.exp2(b_last) + _dot_tn(self.vb[h], kd.astype(BF16))
            yield self.WEIGHTS['state']

    def level(self, k):
        lvl, second_half, pair = self.levels[k]
        for h in range(HG_HEADS):
            b = self.b[h]
            ref = jnp.concatenate(
                [jnp.broadcast_to(b[base + lvl:base + lvl + 1, :], (2 * lvl, HG_DIM))
                 for base in range(0, HG_CHUNK, 2 * lvl)], axis=0)
            e = jnp.exp2(-jnp.abs(b - ref))
            qs = jnp.where(second_half, self.qf[h] * e, 0.0)
            ks = jnp.where(second_half, 0.0, self.kk[h] * e)
            s = _dot_nt(qs.astype(BF16), ks.astype(BF16)) * pair
            self.s_off[h] = s if self.s_off[h] is None else self.s_off[h] + s
            yield self.WEIGHTS['level']

    def off_values(self):
        for h in range(HG_HEADS):
            self.o[h] = self.o[h] + _dot(self.s_off[h].astype(BF16), self.vb[h])
            yield self.WEIGHTS['off_values']

    def diag(self):
        C, SB = HG_CHUNK, HG_SUB
        for h in range(HG_HEADS):
            b = self.b[h]
            self.c_scr[h] = b - jnp.log2(jnp.maximum(self.kk[h], 0.0))
            self.v_scr[h] = self.vb[h].astype(F32)
            scores = []
            for i in range(C // SB):
                base = i * SB
                bi = b[base:base + SB, :]
                qi = self.qf[h][base:base + SB, :]
                for j in range(SB):
                    cj = self.c_scr[h, base + j:base + j + 1, :]
                    p = qi * jnp.exp2(jnp.where(self.sub_row >= j, bi - cj, NEG_BIG))
                    scores.append(jnp.sum(p, axis=-1, keepdims=True))
            self.r.append(scores)
            yield self.WEIGHTS['diag']

    def finish(self):
        C, SB = HG_CHUNK, HG_SUB
        for h in range(HG_HEADS):
            sl = slice(h * HG_DIM, (h + 1) * HG_DIM)
            diag = []
            for i in range(C // SB):
                acc = None
                for j in range(SB):
                    term = self.r[h][i * SB + j] * self.v_scr[h, i * SB + j:i * SB + j + 1, :]
                    acc = term if acc is None else acc + term
                diag.append(acc)
            out = self.o[h] + jnp.concatenate(diag, axis=0)
            ms = jnp.mean(out * out, axis=-1, keepdims=True)
            out = out * lax.rsqrt(ms + RMS_EPS) * self.gain_ref[:, sl]
            out = out * self.load('g', h).astype(F32)
            self.o_ref[pl.ds(self.r0, C), sl] = out.astype(self.o_ref.dtype)
            yield self.WEIGHTS['finish']


PROJ_TM = 1024
PROJ_TN = 512
PROJ_RS = 128
PROJ_EDGES = tuple(c // PROJ_TN for c in (
    0, HG_WIDTH, 2 * HG_WIDTH, 3 * HG_WIDTH, 4 * HG_WIDTH, 4 * HG_WIDTH + ATT_WIDTH,
    4 * HG_WIDTH + ATT_WIDTH + 2 * KV_WIDTH, D_IN))
HGRN_SECTION = {'q': 0, 'k': 1, 'v': 2, 'g': 3}
HGRN_CHUNKS_PER_STEP = PROJ_TM // HG_CHUNK // (PROJ_EDGES[7] - PROJ_EDGES[6])
Q_SCALE = ATT_DIM ** -0.5 * LOG2E


def _rope(t, cos, sin_signed):
    w = t.shape[1]
    reps = w // cos.shape[1]
    cos_t = jnp.concatenate([cos] * reps, axis=1) if reps > 1 else cos
    sin_t = jnp.concatenate([sin_signed] * reps, axis=1) if reps > 1 else sin_signed
    lane = lax.broadcasted_iota(jnp.int32, t.shape, 1)
    first_half = (lane % ATT_DIM) < (ATT_DIM // 2)
    rot = jnp.where(first_half, pltpu.roll(t, w - ATT_DIM // 2, 1), pltpu.roll(t, ATT_DIM // 2, 1))
    return t * cos_t + rot * sin_t


def _dup_heads(t):
    lane = lax.broadcasted_iota(jnp.int32, (t.shape[0], LANES), 1)
    low = lane < ATT_DIM
    out = []
    for g in range(ATT_KV_HEADS):
        src = t[:, LANES * (g // 2):LANES * (g // 2 + 1)]
        swapped = pltpu.roll(src, ATT_DIM, 1)
        out.append(jnp.where(low, src, swapped) if g % 2 == 0 else jnp.where(low, swapped, src))
    return jnp.concatenate(out, axis=1)


def _proj_kernel(x_ref, w_ref, lbl_ref, cos_ref, sin_ref, gain_ref,
                 oa_ref, oaq_ref, okv_ref, og_ref,
                 xb_ref, hp_ref, lf_ref, state_ref, c_scr, v_scr):
    i = pl.program_id(0)
    j = pl.program_id(1)
    e = PROJ_EDGES
    tiles_per_section = HG_WIDTH // PROJ_TN
    heads_per_tile = PROJ_TN // HG_DIM

    @pl.when(j == 0)
    def _():
        xb_ref[...] = x_ref[...].astype(BF16)

    @pl.when((i == 0) & (j == 0))
    def _():
        state_ref[...] = jnp.zeros_like(state_ref)

    def segment(s, interleave=None, rs=PROJ_RS):
        def deco(epilogue):
            @pl.when((j >= e[s]) & (j < e[s + 1]))
            def _():
                n_sub = PROJ_TM // rs
                todo = iter(range(n_sub))

                def sub_block():
                    r = next(todo)
                    rows = slice(r * rs, (r + 1) * rs)
                    epilogue(rows, _dot(xb_ref[rows, :], w_ref[...]))

                if interleave is None:
                    for _ in range(n_sub):
                        sub_block()
                else:
                    interleave(sub_block, n_sub)
                    assert next(todo, None) is None
        return deco

    @segment(0)
    def _(rows, acc):
        hp_ref[j, rows, :] = (acc * jax.nn.sigmoid(acc)).astype(BF16)

    @segment(1)
    def _(rows, acc):
        lbl = lbl_ref[...]
        lexp = jnp.exp(lbl - jnp.max(lbl, axis=0, keepdims=True))
        lb = lexp[0:1, :] / jnp.sum(lexp, axis=0, keepdims=True)
        f = lb + (1.0 - lb) * jax.nn.sigmoid(acc)
        lf_ref[j - e[1], rows, :] = jnp.log(f)
        hp_ref[j, rows, :] = (1.0 - f).astype(BF16)

    @segment(2)
    def _(rows, acc):
        hp_ref[j, rows, :] = acc.astype(BF16)

    @segment(3)
    def _(rows, acc):
        hp_ref[j, rows, :] = (acc * jax.nn.sigmoid(acc)).astype(BF16)

    @segment(4)
    def _(rows, acc):
        oaq_ref[rows, :] = (_rope(acc, cos_ref[rows, :], sin_ref[rows, :]) * Q_SCALE).astype(BF16)

    @segment(5)
    def _(rows, acc):
        k = _rope(acc[:, :KV_WIDTH], cos_ref[rows, :], sin_ref[rows, :])
        okv_ref[rows, :] = jnp.concatenate(
            [_dup_heads(k), _dup_heads(acc[:, KV_WIDTH:])], axis=1).astype(BF16)

    def hgrn_chunks(sub_block, n_sub):
        consts = _hgrn_consts()
        chunks = []
        for u in range(HGRN_CHUNKS_PER_STEP):
            r0 = pl.multiple_of(((j - e[6]) * HGRN_CHUNKS_PER_STEP + u) * HG_CHUNK, HG_CHUNK)

            def load(name, h, r0=r0):
                lanes = slice((h % heads_per_tile) * HG_DIM, (h % heads_per_tile + 1) * HG_DIM)
                if name == 'lf':
                    return lf_ref[h // heads_per_tile, pl.ds(r0, HG_CHUNK), lanes]
                tile = HGRN_SECTION[name] * tiles_per_section + h // heads_per_tile
                return hp_ref[tile, pl.ds(r0, HG_CHUNK), lanes]

            chunks.append(_HgrnChunk(consts, load, gain_ref, oa_ref, r0, state_ref,
                                     c_scr.at[u], v_scr.at[u]))

        def stages():
            for c in chunks:
                yield from c.cumsum()
            for c in chunks:
                yield from c.diag()
            for c in chunks:
                yield from c.state()
                for k in range(3):
                    yield from c.level(k)
                yield from c.off_values()
                yield from c.finish()

        todo, budget = stages(), 0.0
        for _ in range(n_sub):
            budget += len(chunks) * _HgrnChunk.TOTAL_WEIGHT / n_sub
            while budget > 0:
                weight = next(todo, None)
                if weight is None:
                    break
                budget -= weight
            sub_block()
        for _ in todo:
            pass

    @segment(6, interleave=hgrn_chunks, rs=PROJ_RS * 2)
    def _(rows, acc):
        og_ref[rows, :] = acc.astype(BF16)


def _proj(x, wb, lb_logits, cos, sin_signed, gain):
    tm, tn = PROJ_TM, PROJ_TN
    m, k = x.shape
    e = PROJ_EDGES
    assert e[6] - e[5] == 1 and 2 * KV_WIDTH == tn
    assert (tm // HG_CHUNK) % (e[7] - e[6]) == 0

    def span_spec(s, width=tn):
        lo, hi = e[s], e[s + 1]
        return pl.BlockSpec((tm, width), lambda i, j: (i, jnp.clip(j - lo, 0, hi - lo - 1)))

    def sds(width, dtype):
        return jax.ShapeDtypeStruct((m, width), dtype)

    row_tile = lambda i, j: (i, 0)
    return pl.pallas_call(
        _proj_kernel,
        out_shape=(sds(HG_WIDTH, BF16), sds(ATT_WIDTH, BF16), sds(4 * KV_WIDTH, BF16),
                   sds(2 * D_MODEL, BF16)),
        grid=(m // tm, D_IN // tn),
        in_specs=[pl.BlockSpec((tm, k), row_tile),
                  pl.BlockSpec((k, tn), lambda i, j: (0, j)),
                  pl.BlockSpec((lb_logits.shape[0], tn),
                               lambda i, j: (0, jnp.clip(j - e[1], 0, e[2] - e[1] - 1))),
                  pl.BlockSpec((tm, LANES), row_tile),
                  pl.BlockSpec((tm, LANES), row_tile),
                  pl.BlockSpec((1, HG_WIDTH), lambda i, j: (0, 0))],
        out_specs=(pl.BlockSpec((tm, HG_WIDTH), row_tile), span_spec(4),
                   pl.BlockSpec((tm, 4 * KV_WIDTH), row_tile), span_spec(6)),
        scratch_shapes=[pltpu.VMEM((tm, k), BF16),
                        pltpu.VMEM((4 * HG_WIDTH // tn, tm, tn), BF16),
                        pltpu.VMEM((HG_WIDTH // tn, tm, tn), F32),
                        pltpu.VMEM((HG_HEADS, HG_DIM, HG_DIM), F32),
                        pltpu.VMEM((HGRN_CHUNKS_PER_STEP, HG_HEADS, HG_CHUNK, HG_DIM), F32),
                        pltpu.VMEM((HGRN_CHUNKS_PER_STEP, HG_HEADS, HG_CHUNK, HG_DIM), F32)],
        compiler_params=pltpu.CompilerParams(
            dimension_semantics=("arbitrary", "arbitrary"), vmem_limit_bytes=PROJ_VMEM_LIMIT),
        name="proj_hgrn2",
    )(x, wb, lb_logits, cos, sin_signed, gain)


def _attn_kernel(q_ref, kvp_ref, kvc_ref, sink_ref, *refs):
    n_side = (len(refs) - 1) // 2
    side_in, o_ref, side_out = refs[:n_side], refs[n_side], refs[n_side + 1:]
    _side_cast(side_in, side_out)
    n = pl.program_id(0)
    blk = WINDOW
    nblk = q_ref.shape[0] // blk
    qi = lax.broadcasted_iota(jnp.int32, (blk, 2 * blk), 0)
    ci = lax.broadcasted_iota(jnp.int32, (blk, 2 * blk), 1)
    rel = qi + blk - ci
    band = (rel >= 0) & (rel < blk)
    lane = lax.broadcasted_iota(jnp.int32, (blk, LANES), 1)
    low = lane < ATT_DIM
    zero = jnp.zeros((), BF16)

    def keys_values(t, lanes):
        own = kvc_ref[t * blk:(t + 1) * blk, lanes]
        prev = kvp_ref[:, lanes] if t == 0 else kvc_ref[(t - 1) * blk:t * blk, lanes]
        return jnp.concatenate([prev, own], axis=0)

    units = [(t, g) for t in range(nblk) for g in range(ATT_KV_HEADS)]
    pairs = range(ATT_GROUP // 2)
    s = []
    for t, g in units:
        first_key = jnp.where(n > 0, 0, blk) if t == 0 else 0
        bias1 = jnp.where(band & (ci >= first_key), 0.0, NEG_BIG)
        bias = jnp.concatenate([bias1] * ATT_GROUP, axis=0)
        kd = keys_values(t, slice(g * LANES, (g + 1) * LANES))
        qs = []
        for pr in pairs:
            col = (g * len(pairs) + pr) * LANES
            qp = q_ref[t * blk:(t + 1) * blk, col:col + LANES]
            qs += [jnp.where(low, qp, zero), jnp.where(low, zero, qp)]
        s.append(_dot_nt(jnp.concatenate(qs, axis=0), kd) + bias)

    p, denom = [], []
    for u, (t, g) in enumerate(units):
        sink = jnp.concatenate(
            [jnp.broadcast_to(sink_ref[g * ATT_GROUP + h:g * ATT_GROUP + h + 1, :], (blk, LANES))
             for h in range(ATT_GROUP)], axis=0) * LOG2E
        m = jnp.max(jnp.maximum(s[u][:, :blk], s[u][:, blk:]), axis=-1, keepdims=True)
        m = jnp.maximum(jnp.broadcast_to(m, (ATT_GROUP * blk, LANES)), sink)
        w = jnp.exp2(s[u] - jnp.concatenate([m, m], axis=1))
        total = jnp.sum(w[:, :blk] + w[:, blk:], axis=-1, keepdims=True)
        denom.append(jnp.broadcast_to(total, (ATT_GROUP * blk, LANES)) + jnp.exp2(sink - m))
        p.append(w.astype(BF16))

    o = []
    for u, (t, g) in enumerate(units):
        vd = keys_values(t, slice(2 * KV_WIDTH + g * LANES, 2 * KV_WIDTH + (g + 1) * LANES))
        o.append(_dot(p[u], vd) / denom[u])

    for u, (t, g) in enumerate(units):
        for pr in pairs:
            out = jnp.where(low, o[u][2 * pr * blk:(2 * pr + 1) * blk],
                            o[u][(2 * pr + 1) * blk:(2 * pr + 2) * blk])
            col = (g * len(pairs) + pr) * LANES
            o_ref[t * blk:(t + 1) * blk, col:col + LANES] = out.astype(o_ref.dtype)


def _side_cast_specs(side_weights, steps):
    specs, shapes = [], []
    for w in side_weights:
        slab = w.shape[0] // steps
        assert slab * steps == w.shape[0] and slab % 16 == 0
        specs.append(pl.BlockSpec((slab, w.shape[1]), lambda n: (n, 0)))
        shapes.append(jax.ShapeDtypeStruct(w.shape, BF16))
    return specs, shapes


def _side_cast(side_in, side_out):
    for src, dst in zip(side_in, side_out):
        dst[...] = src[...].astype(BF16)


ATT_BLOCKS_PER_STEP = 2


def _attn(aq, akv, sinks_b, side_weights):
    blk, nblk = WINDOW, ATT_BLOCKS_PER_STEP
    s = aq.shape[0]
    steps = s // (blk * nblk)
    side_specs, side_shapes = _side_cast_specs(side_weights, steps)
    outs = pl.pallas_call(
        _attn_kernel,
        out_shape=(jax.ShapeDtypeStruct((s, ATT_WIDTH), BF16), *side_shapes),
        grid=(steps,),
        in_specs=[pl.BlockSpec((nblk * blk, ATT_WIDTH), lambda n: (n, 0)),
                  pl.BlockSpec((blk, 4 * KV_WIDTH), lambda n: (jnp.maximum(nblk * n - 1, 0), 0)),
                  pl.BlockSpec((nblk * blk, 4 * KV_WIDTH), lambda n: (n, 0)),
                  pl.BlockSpec((ATT_HEADS, LANES), lambda n: (0, 0)), *side_specs],
        out_specs=(pl.BlockSpec((nblk * blk, ATT_WIDTH), lambda n: (n, 0)), *side_specs),
        compiler_params=pltpu.CompilerParams(
            dimension_semantics=("parallel",), vmem_limit_bytes=VMEM_LIMIT),
        name="swa",
    )(aq, akv, akv, sinks_b, *side_weights)
    return outs[0], outs[1:]


def _layer_norm(r, gain, bias):
    mu = jnp.mean(r, axis=-1, keepdims=True)
    d = r - mu
    var = jnp.mean(d * d, axis=-1, keepdims=True)
    return d * lax.rsqrt(var + LN_EPS) * gain + bias


MERGE_TM = 256
MERGE_RS = 128


def _merge_kernel(oa_ref, ob_ref, ga_ref, gb_ref, x_ref, wa_ref, wb_ref, wo_ref, g_ref, b_ref,
                  *refs):
    n_side = (len(refs) - 2) // 2
    side_in, (x1_ref, x1b_ref), side_out = refs[:n_side], refs[n_side:n_side + 2], refs[n_side + 2:]
    _side_cast(side_in, side_out)
    for r in range(MERGE_TM // MERGE_RS):
        rows = slice(r * MERGE_RS, (r + 1) * MERGE_RS)
        ya = _dot(oa_ref[rows, :], wa_ref[...])
        yb = _dot(ob_ref[rows, :], wb_ref[...])
        mixed = (jax.nn.sigmoid(ga_ref[rows, :].astype(F32)) * ya
                 + jax.nn.sigmoid(gb_ref[rows, :].astype(F32)) * yb)
        z = _dot(mixed.astype(BF16), wo_ref[...])
        x1 = _layer_norm(ALPHA * x_ref[rows, :] + z, g_ref[...], b_ref[...])
        x1_ref[rows, :] = x1
        x1b_ref[rows, :] = x1.astype(BF16)


def _merge(oa, ob, gates, x, wa, wb, wo, g1, b1, side_weights):
    tm = MERGE_TM
    s = x.shape[0]
    row = lambda i: (i, 0)
    const = lambda i: (0, 0)
    resident = dict(pipeline_mode=pl.Buffered(1))
    side_specs, side_shapes = _side_cast_specs(side_weights, s // tm)
    outs = pl.pallas_call(
        _merge_kernel,
        out_shape=(jax.ShapeDtypeStruct((s, D_MODEL), F32), jax.ShapeDtypeStruct((s, D_MODEL), BF16),
                   *side_shapes),
        grid=(s // tm,),
        in_specs=[pl.BlockSpec((tm, HG_WIDTH), row), pl.BlockSpec((tm, ATT_WIDTH), row),
                  pl.BlockSpec((tm, D_MODEL), row), pl.BlockSpec((tm, D_MODEL), lambda i: (i, 1)),
                  pl.BlockSpec((tm, D_MODEL), row),
                  pl.BlockSpec((HG_WIDTH, D_MODEL), const, **resident),
                  pl.BlockSpec((ATT_WIDTH, D_MODEL), const, **resident),
                  pl.BlockSpec((D_MODEL, D_MODEL), const, **resident),
                  pl.BlockSpec((1, D_MODEL), const), pl.BlockSpec((1, D_MODEL), const),
                  *side_specs],
        out_specs=(pl.BlockSpec((tm, D_MODEL), row), pl.BlockSpec((tm, D_MODEL), row),
                   *side_specs),
        compiler_params=pltpu.CompilerParams(
            dimension_semantics=("parallel",), vmem_limit_bytes=VMEM_LIMIT),
        name="merge_ln1",
    )(oa, ob, gates, gates, x, wa, wb, wo, g1, b1, *side_weights)
    return outs[0], outs[1], outs[2:]


FFN_TM = 512
FFN_TF = 512
FFN_RS = 256


def _ffn_kernel(xb_ref, x_ref, w1_ref, w2_ref, g_ref, b_ref, o_ref, acc_ref):
    j = pl.program_id(1)
    last = pl.num_programs(1) - 1

    def partial(rows):
        h = jnp.maximum(_dot(xb_ref[rows, :], w1_ref[...]), 0.0)
        return _dot((h * h).astype(BF16), w2_ref[...])

    @pl.when(j == 0)
    def _():
        acc_ref[...] = partial(slice(None))

    @pl.when((j > 0) & (j < last))
    def _():
        acc_ref[...] += partial(slice(None))

    @pl.when(j == last)
    def _():
        for r in range(FFN_TM // FFN_RS):
            rows = slice(r * FFN_RS, (r + 1) * FFN_RS)
            y = ALPHA * x_ref[rows, :] + acc_ref[rows, :] + partial(rows)
            o_ref[rows, :] = _layer_norm(y, g_ref[...], b_ref[...])


def _ffn(x1b, x1, w1, w2, g2, b2):
    tm, tf = FFN_TM, FFN_TF
    assert D_FF // tf >= 2
    s = x1.shape[0]
    return pl.pallas_call(
        _ffn_kernel,
        out_shape=jax.ShapeDtypeStruct((s, D_MODEL), F32),
        grid=(s // tm, D_FF // tf),
        in_specs=[pl.BlockSpec((tm, D_MODEL), lambda i, j: (i, 0)),
                  pl.BlockSpec((tm, D_MODEL), lambda i, j: (i, 0)),
                  pl.BlockSpec((D_MODEL, tf), lambda i, j: (0, j)),
                  pl.BlockSpec((tf, D_MODEL), lambda i, j: (j, 0)),
                  pl.BlockSpec((1, D_MODEL), lambda i, j: (0, 0)),
                  pl.BlockSpec((1, D_MODEL), lambda i, j: (0, 0))],
        out_specs=pl.BlockSpec((tm, D_MODEL), lambda i, j: (i, 0)),
        scratch_shapes=[pltpu.VMEM((tm, D_MODEL), F32)],
        compiler_params=pltpu.CompilerParams(
            dimension_semantics=("parallel", "arbitrary"), vmem_limit_bytes=VMEM_LIMIT),
        name="ffn_ln2",
    )(x1b, x1, w1, w2, g2, b2)


def _rope_tables(seq):
    half = ATT_DIM // 2
    inv = ROPE_THETA ** (-np.arange(half, dtype=np.float64) / half)
    ang = np.arange(seq, dtype=np.float64)[:, None] * inv[None, :]
    cos, sin = np.cos(ang), np.sin(ang)
    cos_t = np.concatenate([cos, cos, cos, cos], axis=1).astype(np.float32)
    sin_t = np.concatenate([-sin, sin, -sin, sin], axis=1).astype(np.float32)
    return jnp.asarray(cos_t), jnp.asarray(sin_t)


def kernel(x, w_in, hg_lb_logits, hg_norm_gain, attn_sinks, w_branch_a, w_branch_b, w_out,
           ln1_gain, ln1_bias, w_ff1, w_ff2, ln2_gain, ln2_bias):
    b, s, d = x.shape
    assert (b, s, d) == (1, SEQ, D_MODEL) and w_in.shape == (1, D_MODEL, D_IN)
    x2 = x.reshape(s, d)
    cos_t, sin_t = _rope_tables(s)
    oa, aq, akv, gates = _proj(x2, w_in[0].astype(BF16), hg_lb_logits, cos_t, sin_t, hg_norm_gain)
    sinks_b = jnp.broadcast_to(attn_sinks[0][:, None], (ATT_HEADS, LANES))
    ob, (wa, wb, wo) = _attn(aq, akv, sinks_b, (w_branch_a[0], w_branch_b[0], w_out[0]))
    x1, x1b, (w1, w2) = _merge(oa, ob, gates, x2, wa, wb, wo, ln1_gain, ln1_bias,
                               (w_ff1[0], w_ff2[0]))
    out = _ffn(x1b, x1, w1, w2, ln2_gain, ln2_bias)
    return out.reshape(b, s, d)
```

```python
import math

import jax
import jax.numpy as jnp
import numpy as np
from jax import lax
from jax.experimental import pallas as pl
from jax.experimental.pallas import tpu as pltpu

F32 = jnp.float32
BF16 = jnp.bfloat16

D_MODEL = 2048
SEQ = 8192
HG_HEADS = 8
HG_DIM = 128
HG_WIDTH = HG_HEADS * HG_DIM
ATT_HEADS = 16
ATT_KV_HEADS = 4
ATT_DIM = 64
ATT_GROUP = ATT_HEADS // ATT_KV_HEADS
ATT_WIDTH = ATT_HEADS * ATT_DIM
KV_WIDTH = ATT_KV_HEADS * ATT_DIM
WINDOW = 128
ROPE_THETA = 10000.0
D_FF = 4 * D_MODEL
D_IN = 4 * HG_WIDTH + ATT_WIDTH + 2 * KV_WIDTH + 2 * D_MODEL
ALPHA = 2.0 ** 0.25
LN_EPS = 1e-5
RMS_EPS = 1e-6
LOG2E = math.log2(math.e)

LANES = 128
HG_CHUNK = 64
HG_SUB = 8
NEG_BIG = -1e30

VMEM_LIMIT = 56 * 1024 * 1024
PROJ_VMEM_LIMIT = 60 * 1024 * 1024


def _dot(a, b):
    return jnp.dot(a, b, preferred_element_type=F32)


def _dot_nt(a, b):
    return lax.dot_general(a, b, (((1,), (1,)), ((), ())), preferred_element_type=F32)


def _dot_tn(a, b):
    return lax.dot_general(a, b, (((0,), (0,)), ((), ())), preferred_element_type=F32)


def _hgrn_consts():
    C, SB = HG_CHUNK, HG_SUB
    ri = lax.broadcasted_iota(jnp.int32, (C, C), 0)
    ci = lax.broadcasted_iota(jnp.int32, (C, C), 1)
    tri = (ri >= ci).astype(BF16)
    row = lax.broadcasted_iota(jnp.int32, (C, 1), 0)
    sub_row = lax.broadcasted_iota(jnp.int32, (SB, 1), 0)
    levels = []
    half = C // 2
    while half >= SB:
        pair = ((ri // (2 * half)) == (ci // (2 * half))) & ((ri % (2 * half)) >= half) \
            & ((ci % (2 * half)) < half)
        levels.append((half, (row % (2 * half)) >= half, jnp.where(pair, 1.0, 0.0)))
        half //= 2
    return tri, sub_row, levels


class _HgrnChunk:
    WEIGHTS = dict(cumsum=8, diag=95, state=20, level=15, off_values=3, finish=40)
    TOTAL_WEIGHT = HG_HEADS * (8 + 95 + 20 + 3 * 15 + 3 + 40)

    def __init__(self, consts, load, gain_ref, o_ref, r0, state_ref, c_scr, v_scr):
        self.tri, self.sub_row, self.levels = consts
        assert len(self.levels) == 3
        self.load, self.gain_ref, self.o_ref, self.r0 = load, gain_ref, o_ref, r0
        self.state_ref, self.c_scr, self.v_scr = state_ref, c_scr, v_scr
        hs = range(HG_HEADS)
        self.qf = [load('q', h).astype(F32) for h in hs]
        self.kk = [load('k', h).astype(F32) for h in hs]
        self.vb = [load('v', h) for h in hs]
        self.b, self.o, self.r = [], [], []
        self.s_off = [None] * HG_HEADS

    def cumsum(self):
        for h in range(HG_HEADS):
            lf = self.load('lf', h)
            hi = lf.astype(BF16)
            lo = (lf - hi.astype(F32)).astype(BF16)
            both = _dot(self.tri, jnp.concatenate([hi, lo], axis=1))
            self.b.append((both[:, :HG_DIM] + both[:, HG_DIM:]) * LOG2E)
            yield self.WEIGHTS['cumsum']

    def state(self):
        C = HG_CHUNK
        for h in range(HG_HEADS):
            b, st = self.b[h], self.state_ref[h]
            self.o.append(_dot_nt((self.qf[h] * jnp.exp2(b)).astype(BF16), st.astype(BF16)))
            b_last = b[C - 1:C, :]
            kd = self.kk[h] * jnp.exp2(b_last - b)
            self.state_ref[h] = st * jnp.exp2(b_last) + _dot_tn(self.vb[h], kd.astype(BF16))
            yield self.WEIGHTS['state']

    def level(self, k):
        lvl, second_half, pair = self.levels[k]
        for h in range(HG_HEADS):
            b = self.b[h]
            ref = jnp.concatenate(
                [jnp.broadcast_to(b[base + lvl:base + lvl + 1, :], (2 * lvl, HG_DIM))
                 for base in range(0, HG_CHUNK, 2 * lvl)], axis=0)
            e = jnp.exp2(-jnp.abs(b - ref))
            qs = jnp.where(second_half, self.qf[h] * e, 0.0)
            ks = jnp.where(second_half, 0.0, self.kk[h] * e)
            s = _dot_nt(qs.astype(BF16), ks.astype(BF16)) * pair
            self.s_off[h] = s if self.s_off[h] is None else self.s_off[h] + s
            yield self.WEIGHTS['level']

    def off_values(self):
        for h in range(HG_HEADS):
            self.o[h] = self.o[h] + _dot(self.s_off[h].astype(BF16), self.vb[h])
            yield self.WEIGHTS['off_values']

    def diag(self):
        C, SB = HG_CHUNK, HG_SUB
        for h in range(HG_HEADS):
            b = self.b[h]
            self.c_scr[h] = b - jnp.log2(jnp.maximum(self.kk[h], 0.0))
            self.v_scr[h] = self.vb[h].astype(F32)
            scores = []
            for i in range(C // SB):
                base = i * SB
                bi = b[base:base + SB, :]
                qi = self.qf[h][base:base + SB, :]
                for j in range(SB):
                    cj = self.c_scr[h, base + j:base + j + 1, :]
                    p = qi * jnp.exp2(jnp.where(self.sub_row >= j, bi - cj, NEG_BIG))
                    scores.append(jnp.sum(p, axis=-1, keepdims=True))
            self.r.append(scores)
            yield self.WEIGHTS['diag']

    def finish(self):
        C, SB = HG_CHUNK, HG_SUB
        for h in range(HG_HEADS):
            sl = slice(h * HG_DIM, (h + 1) * HG_DIM)
            diag = []
            for i in range(C // SB):
                acc = None
                for j in range(SB):
                    term = self.r[h][i * SB + j] * self.v_scr[h, i * SB + j:i * SB + j + 1, :]
                    acc = term if acc is None else acc + term
                diag.append(acc)
            out = self.o[h] + jnp.concatenate(diag, axis=0)
            ms = jnp.mean(out * out, axis=-1, keepdims=True)
            out = out * lax.rsqrt(ms + RMS_EPS) * self.gain_ref[:, sl]
            out = out * self.load('g', h).astype(F32)
            self.o_ref[pl.ds(self.r0, C), sl] = out.astype(self.o_ref.dtype)
            yield self.WEIGHTS['finish']


PROJ_TM = 1024
PROJ_TN = 512
PROJ_RS = 128
PROJ_EDGES = tuple(c // PROJ_TN for c in (
    0, HG_WIDTH, 2 * HG_WIDTH, 3 * HG_WIDTH, 4 * HG_WIDTH, 4 * HG_WIDTH + ATT_WIDTH,
    4 * HG_WIDTH + ATT_WIDTH + 2 * KV_WIDTH, D_IN))
HGRN_SECTION = {'q': 0, 'k': 1, 'v': 2, 'g': 3}
HGRN_CHUNKS_PER_STEP = PROJ_TM // HG_CHUNK // (PROJ_EDGES[7] - PROJ_EDGES[6])
Q_SCALE = ATT_DIM ** -0.5 * LOG2E


def _rope(t, cos, sin_signed):
    w = t.shape[1]
    reps = w // cos.shape[1]
    cos_t = jnp.concatenate([cos] * reps, axis=1) if reps > 1 else cos
    sin_t = jnp.concatenate([sin_signed] * reps, axis=1) if reps > 1 else sin_signed
    lane = lax.broadcasted_iota(jnp.int32, t.shape, 1)
    first_half = (lane % ATT_DIM) < (ATT_DIM // 2)
    rot = jnp.where(first_half, pltpu.roll(t, w - ATT_DIM // 2, 1), pltpu.roll(t, ATT_DIM // 2, 1))
    return t * cos_t + rot * sin_t


def _dup_heads(t):
    lane = lax.broadcasted_iota(jnp.int32, (t.shape[0], LANES), 1)
    low = lane < ATT_DIM
    out = []
    for g in range(ATT_KV_HEADS):
        src = t[:, LANES * (g // 2):LANES * (g // 2 + 1)]
        swapped = pltpu.roll(src, ATT_DIM, 1)
        out.append(jnp.where(low, src, swapped) if g % 2 == 0 else jnp.where(low, swapped, src))
    return jnp.concatenate(out, axis=1)


def _proj_kernel(x_ref, w_ref, lbl_ref, cos_ref, sin_ref, gain_ref,
                 oa_ref, oaq_ref, okv_ref, og_ref,
                 xb_ref, hp_ref, lf_ref, state_ref, c_scr, v_scr):
    i = pl.program_id(0)
    j = pl.program_id(1)
    e = PROJ_EDGES
    tiles_per_section = HG_WIDTH // PROJ_TN
    heads_per_tile = PROJ_TN // HG_DIM

    @pl.when(j == 0)
    def _():
        xb_ref[...] = x_ref[...].astype(BF16)

    @pl.when((i == 0) & (j == 0))
    def _():
        state_ref[...] = jnp.zeros_like(state_ref)

    def segment(s, interleave=None, rs=PROJ_RS):
        def deco(epilogue):
            @pl.when((j >= e[s]) & (j < e[s + 1]))
            def _():
                n_sub = PROJ_TM // rs
                todo = iter(range(n_sub))

                def sub_block():
                    r = next(todo)
                    rows = slice(r * rs, (r + 1) * rs)
                    epilogue(rows, _dot(xb_ref[rows, :], w_ref[...]))

                if interleave is None:
                    for _ in range(n_sub):
                        sub_block()
                else:
                    interleave(sub_block, n_sub)
                    assert next(todo, None) is None
        return deco

    @segment(0)
    def _(rows, acc):
        hp_ref[j, rows, :] = (acc * jax.nn.sigmoid(acc)).astype(BF16)

    @segment(1)
    def _(rows, acc):
        lbl = lbl_ref[...]
        lexp = jnp.exp(lbl - jnp.max(lbl, axis=0, keepdims=True))
        lb = lexp[0:1, :] / jnp.sum(lexp, axis=0, keepdims=True)
        f = lb + (1.0 - lb) * jax.nn.sigmoid(acc)
        lf_ref[j - e[1], rows, :] = jnp.log(f)
        hp_ref[j, rows, :] = (1.0 - f).astype(BF16)

    @segment(2)
    def _(rows, acc):
        hp_ref[j, rows, :] = acc.astype(BF16)

    @segment(3)
    def _(rows, acc):
        hp_ref[j, rows, :] = (acc * jax.nn.sigmoid(acc)).astype(BF16)

    @segment(4)
    def _(rows, acc):
        oaq_ref[rows, :] = (_rope(acc, cos_ref[rows, :], sin_ref[rows, :]) * Q_SCALE).astype(BF16)

    @segment(5)
    def _(rows, acc):
        k = _rope(acc[:, :KV_WIDTH], cos_ref[rows, :], sin_ref[rows, :])
        okv_ref[rows, :] = jnp.concatenate(
            [_dup_heads(k), _dup_heads(acc[:, KV_WIDTH:])], axis=1).astype(BF16)

    def hgrn_chunks(sub_block, n_sub):
        consts = _hgrn_consts()
        chunks = []
        for u in range(HGRN_CHUNKS_PER_STEP):
            r0 = pl.multiple_of(((j - e[6]) * HGRN_CHUNKS_PER_STEP + u) * HG_CHUNK, HG_CHUNK)

            def load(name, h, r0=r0):
                lanes = slice((h % heads_per_tile) * HG_DIM, (h % heads_per_tile + 1) * HG_DIM)
                if name == 'lf':
                    return lf_ref[h // heads_per_tile, pl.ds(r0, HG_CHUNK), lanes]
                tile = HGRN_SECTION[name] * tiles_per_section + h // heads_per_tile
                return hp_ref[tile, pl.ds(r0, HG_CHUNK), lanes]

            chunks.append(_HgrnChunk(consts, load, gain_ref, oa_ref, r0, state_ref,
                                     c_scr.at[u], v_scr.at[u]))

        def stages():
            for c in chunks:
                yield from c.cumsum()
            for c in chunks:
                yield from c.diag()
            for c in chunks:
                yield from c.state()
                for k in range(3):
                    yield from c.level(k)
                yield from c.off_values()
                yield from c.finish()

        todo, budget = stages(), 0.0
        for _ in range(n_sub):
            budget += len(chunks) * _HgrnChunk.TOTAL_WEIGHT / n_sub
            while budget > 0:
                weight = next(todo, None)
                if weight is None:
                    break
                budget -= weight
            sub_block()
        for _ in todo:
            pass

    @segment(6, interleave=hgrn_chunks, rs=PROJ_RS * 2)
    def _(rows, acc):
        og_ref[rows, :] = acc.astype(BF16)


def _proj(x, wb, lb_logits, cos, sin_signed, gain):
    tm, tn = PROJ_TM, PROJ_TN
    m, k = x.shape
    e = PROJ_EDGES
    assert e[6] - e[5] == 1 and 2 * KV_WIDTH == tn
    assert (tm // HG_CHUNK) % (e[7] - e[6]) == 0

    def span_spec(s, width=tn):
        lo, hi = e[s], e[s + 1]
        return pl.BlockSpec((tm, width), lambda i, j: (i, jnp.clip(j - lo, 0, hi - lo - 1)))

    def sds(width, dtype):
        return jax.ShapeDtypeStruct((m, width), dtype)

    row_tile = lambda i, j: (i, 0)
    return pl.pallas_call(
        _proj_kernel,
        out_shape=(sds(HG_WIDTH, BF16), sds(ATT_WIDTH, BF16), sds(4 * KV_WIDTH, BF16),
                   sds(2 * D_MODEL, BF16)),
        grid=(m // tm, D_IN // tn),
        in_specs=[pl.BlockSpec((tm, k), row_tile),
                  pl.BlockSpec((k, tn), lambda i, j: (0, j)),
                  pl.BlockSpec((lb_logits.shape[0], tn),
                               lambda i, j: (0, jnp.clip(j - e[1], 0, e[2] - e[1] - 1))),
                  pl.BlockSpec((tm, LANES), row_tile),
                  pl.BlockSpec((tm, LANES), row_tile),
                  pl.BlockSpec((1, HG_WIDTH), lambda i, j: (0, 0))],
        out_specs=(pl.BlockSpec((tm, HG_WIDTH), row_tile), span_spec(4),
                   pl.BlockSpec((tm, 4 * KV_WIDTH), row_tile), span_spec(6)),
        scratch_shapes=[pltpu.VMEM((tm, k), BF16),
                        pltpu.VMEM((4 * HG_WIDTH // tn, tm, tn), BF16),
                        pltpu.VMEM((HG_WIDTH // tn, tm, tn), F32),
                        pltpu.VMEM((HG_HEADS, HG_DIM, HG_DIM), F32),
                        pltpu.VMEM((HGRN_CHUNKS_PER_STEP, HG_HEADS, HG_CHUNK, HG_DIM), F32),
                        pltpu.VMEM((HGRN_CHUNKS_PER_STEP, HG_HEADS, HG_CHUNK, HG_DIM), F32)],
        compiler_params=pltpu.CompilerParams(
            dimension_semantics=("arbitrary", "arbitrary"), vmem_limit_bytes=PROJ_VMEM_LIMIT),
        name="proj_hgrn2",
    )(x, wb, lb_logits, cos, sin_signed, gain)


def _attn_kernel(q_ref, kvp_ref, kvc_ref, sink_ref, *refs):
    n_side = (len(refs) - 1) // 2
    side_in, o_ref, side_out = refs[:n_side], refs[n_side], refs[n_side + 1:]
    _side_cast(side_in, side_out)
    n = pl.program_id(0)
    blk = WINDOW
    nblk = q_ref.shape[0] // blk
    qi = lax.broadcasted_iota(jnp.int32, (blk, 2 * blk), 0)
    ci = lax.broadcasted_iota(jnp.int32, (blk, 2 * blk), 1)
    rel = qi + blk - ci
    band = (rel >= 0) & (rel < blk)
    lane = lax.broadcasted_iota(jnp.int32, (blk, LANES), 1)
    low = lane < ATT_DIM
    zero = jnp.zeros((), BF16)

    def keys_values(t, lanes):
        own = kvc_ref[t * blk:(t + 1) * blk, lanes]
        prev = kvp_ref[:, lanes] if t == 0 else kvc_ref[(t - 1) * blk:t * blk, lanes]
        return jnp.concatenate([prev, own], axis=0)

    units = [(t, g) for t in range(nblk) for g in range(ATT_KV_HEADS)]
    pairs = range(ATT_GROUP // 2)
    s = []
    for t, g in units:
        first_key = jnp.where(n > 0, 0, blk) if t == 0 else 0
        bias1 = jnp.where(band & (ci >= first_key), 0.0, NEG_BIG)
        bias = jnp.concatenate([bias1] * ATT_GROUP, axis=0)
        kd = keys_values(t, slice(g * LANES, (g + 1) * LANES))
        qs = []
        for pr in pairs:
            col = (g * len(pairs) + pr) * LANES
            qp = q_ref[t * blk:(t + 1) * blk, col:col + LANES]
            qs += [jnp.where(low, qp, zero), jnp.where(low, zero, qp)]
        s.append(_dot_nt(jnp.concatenate(qs, axis=0), kd) + bias)

    p, denom = [], []
    for u, (t, g) in enumerate(units):
        sink = jnp.concatenate(
            [jnp.broadcast_to(sink_ref[g * ATT_GROUP + h:g * ATT_GROUP + h + 1, :], (blk, LANES))
             for h in range(ATT_GROUP)], axis=0) * LOG2E
        m = jnp.max(jnp.maximum(s[u][:, :blk], s[u][:, blk:]), axis=-1, keepdims=True)
        m = jnp.maximum(jnp.broadcast_to(m, (ATT_GROUP * blk, LANES)), sink)
        w = jnp.exp2(s[u] - jnp.concatenate([m, m], axis=1))
        total = jnp.sum(w[:, :blk] + w[:, blk:], axis=-1, keepdims=True)
        denom.append(jnp.broadcast_to(total, (ATT_GROUP * blk, LANES)) + jnp.exp2(sink - m))
        p.append(w.astype(BF16))

    o = []
    for u, (t, g) in enumerate(units):
        vd = keys_values(t, slice(2 * KV_WIDTH + g * LANES, 2 * KV_WIDTH + (g + 1) * LANES))
        o.append(_dot(p[u], vd) / denom[u])

    for u, (t, g) in enumerate(units):
        for pr in pairs:
            out = jnp.where(low, o[u][2 * pr * blk:(2 * pr + 1) * blk],
                            o[u][(2 * pr + 1) * blk:(2 * pr + 2) * blk])
            col = (g * len(pairs) + pr) * LANES
            o_ref[t * blk:(t + 1) * blk, col:col + LANES] = out.astype(o_ref.dtype)


def _side_cast_specs(side_weights, steps):
    specs, shapes = [], []
    for w in side_weights:
        slab = w.shape[0] // steps
        assert slab * steps == w.shape[0] and slab % 16 == 0
        specs.append(pl.BlockSpec((slab, w.shape[1]), lambda n: (n, 0)))
        shapes.append(jax.ShapeDtypeStruct(w.shape, BF16))
    return specs, shapes


def _side_cast(side_in, side_out):
    for src, dst in zip(side_in, side_out):
        dst[...] = src[...].astype(BF16)


ATT_BLOCKS_PER_STEP = 2


def _attn(aq, akv, sinks_b, side_weights):
    blk, nblk = WINDOW, ATT_BLOCKS_PER_STEP
    s = aq.shape[0]
    steps = s // (blk * nblk)
    side_specs, side_shapes = _side_cast_specs(side_weights, steps)
    outs = pl.pallas_call(
        _attn_kernel,
        out_shape=(jax.ShapeDtypeStruct((s, ATT_WIDTH), BF16), *side_shapes),
        grid=(steps,),
        in_specs=[pl.BlockSpec((nblk * blk, ATT_WIDTH), lambda n: (n, 0)),
                  pl.BlockSpec((blk, 4 * KV_WIDTH), lambda n: (jnp.maximum(nblk * n - 1, 0), 0)),
                  pl.BlockSpec((nblk * blk, 4 * KV_WIDTH), lambda n: (n, 0)),
                  pl.BlockSpec((ATT_HEADS, LANES), lambda n: (0, 0)), *side_specs],
        out_specs=(pl.BlockSpec((nblk * blk, ATT_WIDTH), lambda n: (n, 0)), *side_specs),
        compiler_params=pltpu.CompilerParams(
            dimension_semantics=("parallel",), vmem_limit_bytes=VMEM_LIMIT),
        name="swa",
    )(aq, akv, akv, sinks_b, *side_weights)
    return outs[0], outs[1:]


def _layer_norm(r, gain, bias):
    mu = jnp.mean(r, axis=-1, keepdims=True)
    d = r - mu
    var = jnp.mean(d * d, axis=-1, keepdims=True)
    return d * lax.rsqrt(var + LN_EPS) * gain + bias


MERGE_TM = 256
MERGE_RS = 128


def _merge_kernel(oa_ref, ob_ref, ga_ref, gb_ref, x_ref, wa_ref, wb_ref, wo_ref, g_ref, b_ref,
                  *refs):
    n_side = (len(refs) - 2) // 2
    side_in, (x1_ref, x1b_ref), side_out = refs[:n_side], refs[n_side:n_side + 2], refs[n_side + 2:]
    _side_cast(side_in, side_out)
    for r in range(MERGE_TM // MERGE_RS):
        rows = slice(r * MERGE_RS, (r + 1) * MERGE_RS)
        ya = _dot(oa_ref[rows, :], wa_ref[...])
        yb = _dot(ob_ref[rows, :], wb_ref[...])
        mixed = (jax.nn.sigmoid(ga_ref[rows, :].astype(F32)) * ya
                 + jax.nn.sigmoid(gb_ref[rows, :].astype(F32)) * yb)
        z = _dot(mixed.astype(BF16), wo_ref[...])
        x1 = _layer_norm(ALPHA * x_ref[rows, :] + z, g_ref[...], b_ref[...])
        x1_ref[rows, :] = x1
        x1b_ref[rows, :] = x1.astype(BF16)


def _merge(oa, ob, gates, x, wa, wb, wo, g1, b1, side_weights):
    tm = MERGE_TM
    s = x.shape[0]
    row = lambda i: (i, 0)
    const = lambda i: (0, 0)
    resident = dict(pipeline_mode=pl.Buffered(1))
    side_specs, side_shapes = _side_cast_specs(side_weights, s // tm)
    outs = pl.pallas_call(
        _merge_kernel,
        out_shape=(jax.ShapeDtypeStruct((s, D_MODEL), F32), jax.ShapeDtypeStruct((s, D_MODEL), BF16),
                   *side_shapes),
        grid=(s // tm,),
        in_specs=[pl.BlockSpec((tm, HG_WIDTH), row), pl.BlockSpec((tm, ATT_WIDTH), row),
                  pl.BlockSpec((tm, D_MODEL), row), pl.BlockSpec((tm, D_MODEL), lambda i: (i, 1)),
                  pl.BlockSpec((tm, D_MODEL), row),
                  pl.BlockSpec((HG_WIDTH, D_MODEL), const, **resident),
                  pl.BlockSpec((ATT_WIDTH, D_MODEL), const, **resident),
                  pl.BlockSpec((D_MODEL, D_MODEL), const, **resident),
                  pl.BlockSpec((1, D_MODEL), const), pl.BlockSpec((1, D_MODEL), const),
                  *side_specs],
        out_specs=(pl.BlockSpec((tm, D_MODEL), row), pl.BlockSpec((tm, D_MODEL), row),
                   *side_specs),
        compiler_params=pltpu.CompilerParams(
            dimension_semantics=("parallel",), vmem_limit_bytes=VMEM_LIMIT),
        name="merge_ln1",
    )(oa, ob, gates, gates, x, wa, wb, wo, g1, b1, *side_weights)
    return outs[0], outs[1], outs[2:]


FFN_TM = 1024
FFN_TF = 512
FFN_RS = 256


def _ffn_kernel(xb_ref, x_ref, w1_ref, w2_ref, g_ref, b_ref, o_ref):
    acc_ref = o_ref
    j = pl.program_id(1)
    last = pl.num_programs(1) - 1

    blocks = [slice(r * FFN_RS, (r + 1) * FFN_RS) for r in range(FFN_TM // FFN_RS)]

    def partials():
        hs = [jnp.maximum(_dot(xb_ref[rows, :], w1_ref[...]), 0.0) for rows in blocks]
        return [_dot((h * h).astype(BF16), w2_ref[...]) for h in hs]

    @pl.when(j == 0)
    def _():
        for rows, part in zip(blocks, partials()):
            acc_ref[rows, :] = part

    @pl.when((j > 0) & (j < last))
    def _():
        for rows, part in zip(blocks, partials()):
            acc_ref[rows, :] += part

    @pl.when(j == last)
    def _():
        for rows, part in zip(blocks, partials()):
            y = ALPHA * x_ref[rows, :] + acc_ref[rows, :] + part
            o_ref[rows, :] = _layer_norm(y, g_ref[...], b_ref[...])


def _ffn(x1b, x1, w1, w2, g2, b2):
    tm, tf = FFN_TM, FFN_TF
    assert D_FF // tf >= 2
    s = x1.shape[0]
    return pl.pallas_call(
        _ffn_kernel,
        out_shape=jax.ShapeDtypeStruct((s, D_MODEL), F32),
        grid=(s // tm, D_FF // tf),
        in_specs=[pl.BlockSpec((tm, D_MODEL), lambda i, j: (i, 0)),
                  pl.BlockSpec((tm, D_MODEL), lambda i, j: (i, 0)),
                  pl.BlockSpec((D_MODEL, tf), lambda i, j: (0, j)),
                  pl.BlockSpec((tf, D_MODEL), lambda i, j: (j, 0)),
                  pl.BlockSpec((1, D_MODEL), lambda i, j: (0, 0)),
                  pl.BlockSpec((1, D_MODEL), lambda i, j: (0, 0))],
        out_specs=pl.BlockSpec((tm, D_MODEL), lambda i, j: (i, 0)),
        compiler_params=pltpu.CompilerParams(
            dimension_semantics=("parallel", "arbitrary"), vmem_limit_bytes=VMEM_LIMIT),
        name="ffn_ln2",
    )(x1b, x1, w1, w2, g2, b2)


def _rope_tables(seq):
    half = ATT_DIM // 2
    inv = ROPE_THETA ** (-np.arange(half, dtype=np.float64) / half)
    ang = np.arange(seq, dtype=np.float64)[:, None] * inv[None, :]
    cos, sin = np.cos(ang), np.sin(ang)
    cos_t = np.concatenate([cos, cos, cos, cos], axis=1).astype(np.float32)
    sin_t = np.concatenate([-sin, sin, -sin, sin], axis=1).astype(np.float32)
    return jnp.asarray(cos_t), jnp.asarray(sin_t)


def kernel(x, w_in, hg_lb_logits, hg_norm_gain, attn_sinks, w_branch_a, w_branch_b, w_out,
           ln1_gain, ln1_bias, w_ff1, w_ff2, ln2_gain, ln2_bias):
    b, s, d = x.shape
    assert (b, s, d) == (1, SEQ, D_MODEL) and w_in.shape == (1, D_MODEL, D_IN)
    x2 = x.reshape(s, d)
    cos_t, sin_t = _rope_tables(s)
    oa, aq, akv, gates = _proj(x2, w_in[0].astype(BF16), hg_lb_logits, cos_t, sin_t, hg_norm_gain)
    sinks_b = jnp.broadcast_to(attn_sinks[0][:, None], (ATT_HEADS, LANES))
    ob, (wa, wb, wo) = _attn(aq, akv, sinks_b, (w_branch_a[0], w_branch_b[0], w_out[0]))
    x1, x1b, (w1, w2) = _merge(oa, ob, gates, x2, wa, wb, wo, ln1_gain, ln1_bias,
                               (w_ff1[0], w_ff2[0]))
    out = _ffn(x1b, x1, w1, w2, ln2_gain, ln2_bias)
    return out.reshape(b, s, d)
```

```python
import math

import jax
import jax.numpy as jnp
import numpy as np
from jax import lax
from jax.experimental import pallas as pl
from jax.experimental.pallas import tpu as pltpu

F32 = jnp.float32
BF16 = jnp.bfloat16

D_MODEL = 2048
SEQ = 8192
HG_HEADS = 8
HG_DIM = 128
HG_WIDTH = HG_HEADS * HG_DIM
ATT_HEADS = 16
ATT_KV_HEADS = 4
ATT_DIM = 64
ATT_GROUP = ATT_HEADS // ATT_KV_HEADS
ATT_WIDTH = ATT_HEADS * ATT_DIM
KV_WIDTH = ATT_KV_HEADS * ATT_DIM
WINDOW = 128
ROPE_THETA = 10000.0
D_FF = 4 * D_MODEL
D_IN = 4 * HG_WIDTH + ATT_WIDTH + 2 * KV_WIDTH + 2 * D_MODEL
ALPHA = 2.0 ** 0.25
LN_EPS = 1e-5
RMS_EPS = 1e-6
LOG2E = math.log2(math.e)

LANES = 128
HG_CHUNK = 64
HG_SUB = 8
NEG_BIG = -1e30

VMEM_LIMIT = 56 * 1024 * 1024
PROJ_VMEM_LIMIT = 60 * 1024 * 1024


def _dot(a, b):
    return jnp.dot(a, b, preferred_element_type=F32)


def _dot_nt(a, b):
    return lax.dot_general(a, b, (((1,), (1,)), ((), ())), preferred_element_type=F32)


def _dot_tn(a, b):
    return lax.dot_general(a, b, (((0,), (0,)), ((), ())), preferred_element_type=F32)


def _hgrn_consts():
    C, SB = HG_CHUNK, HG_SUB
    ri = lax.broadcasted_iota(jnp.int32, (C, C), 0)
    ci = lax.broadcasted_iota(jnp.int32, (C, C), 1)
    tri = (ri >= ci).astype(BF16)
    row = lax.broadcasted_iota(jnp.int32, (C, 1), 0)
    sub_row = lax.broadcasted_iota(jnp.int32, (SB, 1), 0)
    levels = []
    half = C // 2
    while half >= SB:
        pair = ((ri // (2 * half)) == (ci // (2 * half))) & ((ri % (2 * half)) >= half) \
            & ((ci % (2 * half)) < half)
        levels.append((half, (row % (2 * half)) >= half, jnp.where(pair, 1.0, 0.0)))
        half //= 2
    return tri, sub_row, levels


class _HgrnChunk:
    WEIGHTS = dict(cumsum=8, diag=95, state=20, level=15, off_values=3, finish=40)
    TOTAL_WEIGHT = HG_HEADS * (8 + 95 + 20 + 3 * 15 + 3 + 40)

    def __init__(self, consts, load, gain_ref, o_ref, r0, state_ref, c_scr, v_scr):
        self.tri, self.sub_row, self.levels = consts
        assert len(self.levels) == 3
        self.load, self.gain_ref, self.o_ref, self.r0 = load, gain_ref, o_ref, r0
        self.state_ref, self.c_scr, self.v_scr = state_ref, c_scr, v_scr
        hs = range(HG_HEADS)
        self.qf = [load('q', h).astype(F32) for h in hs]
        self.kk = [load('k', h).astype(F32) for h in hs]
        self.vb = [load('v', h) for h in hs]
        self.b, self.o, self.r = [], [], []
        self.s_off = [None] * HG_HEADS

    def cumsum(self):
        for h in range(HG_HEADS):
            lf = self.load('lf', h)
            hi = lf.astype(BF16)
            lo = (lf - hi.astype(F32)).astype(BF16)
            both = _dot(self.tri, jnp.concatenate([hi, lo], axis=1))
            self.b.append((both[:, :HG_DIM] + both[:, HG_DIM:]) * LOG2E)
            yield self.WEIGHTS['cumsum']

    def state(self):
        C = HG_CHUNK
        for h in range(HG_HEADS):
            b, st = self.b[h], self.state_ref[h]
            self.o.append(_dot_nt((self.qf[h] * jnp.exp2(b)).astype(BF16), st.astype(BF16)))
            b_last = b[C - 1:C, :]
            kd = self.kk[h] * jnp.exp2(b_last - b)
            self.state_ref[h] = st * jnp.exp2(b_last) + _dot_tn(self.vb[h], kd.astype(BF16))
            yield self.WEIGHTS['state']

    def level(self, k):
        lvl, second_half, pair = self.levels[k]
        for h in range(HG_HEADS):
            b = self.b[h]
            ref = jnp.concatenate(
                [jnp.broadcast_to(b[base + lvl:base + lvl + 1, :], (2 * lvl, HG_DIM))
                 for base in range(0, HG_CHUNK, 2 * lvl)], axis=0)
            e = jnp.exp2(-jnp.abs(b - ref))
            qs = jnp.where(second_half, self.qf[h] * e, 0.0)
            ks = jnp.where(second_half, 0.0, self.kk[h] * e)
            s = _dot_nt(qs.astype(BF16), ks.astype(BF16)) * pair
            self.s_off[h] = s if self.s_off[h] is None else self.s_off[h] + s
            yield self.WEIGHTS['level']

    def off_values(self):
        for h in range(HG_HEADS):
            self.o[h] = self.o[h] + _dot(self.s_off[h].astype(BF16), self.vb[h])
            yield self.WEIGHTS['off_values']

    def diag(self):
        C, SB = HG_CHUNK, HG_SUB
        for h in range(HG_HEADS):
            b = self.b[h]
            self.c_scr[h] = b - jnp.log2(jnp.maximum(self.kk[h], 0.0))
            self.v_scr[h] = self.vb[h].astype(F32)
            scores = []
            for i in range(C // SB):
                base = i * SB
                bi = b[base:base + SB, :]
                qi = self.qf[h][base:base + SB, :]
                for j in range(SB):
                    cj = self.c_scr[h, base + j:base + j + 1, :]
                    p = qi * jnp.exp2(jnp.where(self.sub_row >= j, bi - cj, NEG_BIG))
                    scores.append(jnp.sum(p, axis=-1, keepdims=True))
            self.r.append(scores)
            yield self.WEIGHTS['diag']

    def finish(self):
        C, SB = HG_CHUNK, HG_SUB
        for h in range(HG_HEADS):
            sl = slice(h * HG_DIM, (h + 1) * HG_DIM)
            diag = []
            for i in range(C // SB):
                acc = None
                for j in range(SB):
                    term = self.r[h][i * SB + j] * self.v_scr[h, i * SB + j:i * SB + j + 1, :]
                    acc = term if acc is None else acc + term
                diag.append(acc)
            out = self.o[h] + jnp.concatenate(diag, axis=0)
            ms = jnp.mean(out * out, axis=-1, keepdims=True)
            out = out * lax.rsqrt(ms + RMS_EPS) * self.gain_ref[:, sl]
            out = out * self.load('g', h).astype(F32)
            self.o_ref[pl.ds(self.r0, C), sl] = out.astype(self.o_ref.dtype)
            yield self.WEIGHTS['finish']


PROJ_TM = 1024
PROJ_TN = 512
PROJ_RS = 128
PROJ_EDGES = tuple(c // PROJ_TN for c in (
    0, HG_WIDTH, 2 * HG_WIDTH, 3 * HG_WIDTH, 4 * HG_WIDTH, 4 * HG_WIDTH + ATT_WIDTH,
    4 * HG_WIDTH + ATT_WIDTH + 2 * KV_WIDTH, D_IN))
HGRN_SECTION = {'q': 0, 'k': 1, 'v': 2, 'g': 3}
HGRN_CHUNKS_PER_STEP = PROJ_TM // HG_CHUNK // (PROJ_EDGES[7] - PROJ_EDGES[6])
Q_SCALE = ATT_DIM ** -0.5 * LOG2E


def _rope(t, cos, sin_signed):
    w = t.shape[1]
    reps = w // cos.shape[1]
    cos_t = jnp.concatenate([cos] * reps, axis=1) if reps > 1 else cos
    sin_t = jnp.concatenate([sin_signed] * reps, axis=1) if reps > 1 else sin_signed
    lane = lax.broadcasted_iota(jnp.int32, t.shape, 1)
    first_half = (lane % ATT_DIM) < (ATT_DIM // 2)
    rot = jnp.where(first_half, pltpu.roll(t, w - ATT_DIM // 2, 1), pltpu.roll(t, ATT_DIM // 2, 1))
    return t * cos_t + rot * sin_t


def _dup_heads(t):
    lane = lax.broadcasted_iota(jnp.int32, (t.shape[0], LANES), 1)
    low = lane < ATT_DIM
    out = []
    for g in range(ATT_KV_HEADS):
        src = t[:, LANES * (g // 2):LANES * (g // 2 + 1)]
        swapped = pltpu.roll(src, ATT_DIM, 1)
        out.append(jnp.where(low, src, swapped) if g % 2 == 0 else jnp.where(low, swapped, src))
    return jnp.concatenate(out, axis=1)


def _proj_kernel(x_ref, w_ref, lbl_ref, cos_ref, sin_ref, gain_ref, *refs):
    n_side = len(PROJ_SIDE_CASTS)
    side_in = refs[:n_side]
    oa_ref, oaq_ref, okv_ref, og_ref = refs[n_side:n_side + 4]
    side_out = refs[n_side + 4:2 * n_side + 4]
    xb_ref, hp_ref, lf_ref, state_ref, c_scr, v_scr = refs[2 * n_side + 4:]
    i = pl.program_id(0)
    j = pl.program_id(1)
    e = PROJ_EDGES
    tiles_per_section = HG_WIDTH // PROJ_TN
    heads_per_tile = PROJ_TN // HG_DIM

    @pl.when(j == 0)
    def _():
        xb_ref[...] = x_ref[...].astype(BF16)

    @pl.when((i == 0) & (j == 0))
    def _():
        state_ref[...] = jnp.zeros_like(state_ref)

    def segment(s, interleave=None, rs=PROJ_RS):
        def deco(epilogue):
            @pl.when((j >= e[s]) & (j < e[s + 1]))
            def _():
                _side_cast(side_in, side_out)
                n_sub = PROJ_TM // rs
                todo = iter(range(n_sub))

                def sub_block():
                    r = next(todo)
                    rows = slice(r * rs, (r + 1) * rs)
                    epilogue(rows, _dot(xb_ref[rows, :], w_ref[...]))

                if interleave is None:
                    for _ in range(n_sub):
                        sub_block()
                else:
                    interleave(sub_block, n_sub)
                    assert next(todo, None) is None
        return deco

    @segment(0)
    def _(rows, acc):
        hp_ref[j, rows, :] = (acc * jax.nn.sigmoid(acc)).astype(BF16)

    @segment(1)
    def _(rows, acc):
        lbl = lbl_ref[...]
        lexp = jnp.exp(lbl - jnp.max(lbl, axis=0, keepdims=True))
        lb = lexp[0:1, :] / jnp.sum(lexp, axis=0, keepdims=True)
        f = lb + (1.0 - lb) * jax.nn.sigmoid(acc)
        lf_ref[j - e[1], rows, :] = jnp.log(f)
        hp_ref[j, rows, :] = (1.0 - f).astype(BF16)

    @segment(2)
    def _(rows, acc):
        hp_ref[j, rows, :] = acc.astype(BF16)

    @segment(3)
    def _(rows, acc):
        hp_ref[j, rows, :] = (acc * jax.nn.sigmoid(acc)).astype(BF16)

    @segment(4)
    def _(rows, acc):
        oaq_ref[rows, :] = (_rope(acc, cos_ref[rows, :], sin_ref[rows, :]) * Q_SCALE).astype(BF16)

    @segment(5)
    def _(rows, acc):
        k = _rope(acc[:, :KV_WIDTH], cos_ref[rows, :], sin_ref[rows, :])
        okv_ref[rows, :] = jnp.concatenate(
            [_dup_heads(k), _dup_heads(acc[:, KV_WIDTH:])], axis=1).astype(BF16)

    def hgrn_chunks(sub_block, n_sub):
        consts = _hgrn_consts()
        chunks = []
        for u in range(HGRN_CHUNKS_PER_STEP):
            r0 = pl.multiple_of(((j - e[6]) * HGRN_CHUNKS_PER_STEP + u) * HG_CHUNK, HG_CHUNK)

            def load(name, h, r0=r0):
                lanes = slice((h % heads_per_tile) * HG_DIM, (h % heads_per_tile + 1) * HG_DIM)
                if name == 'lf':
                    return lf_ref[h // heads_per_tile, pl.ds(r0, HG_CHUNK), lanes]
                tile = HGRN_SECTION[name] * tiles_per_section + h // heads_per_tile
                return hp_ref[tile, pl.ds(r0, HG_CHUNK), lanes]

            chunks.append(_HgrnChunk(consts, load, gain_ref, oa_ref, r0, state_ref,
                                     c_scr.at[u], v_scr.at[u]))

        def stages():
            for c in chunks:
                yield from c.cumsum()
            for c in chunks:
                yield from c.diag()
            for c in chunks:
                yield from c.state()
                for k in range(3):
                    yield from c.level(k)
                yield from c.off_values()
                yield from c.finish()

        todo, budget = stages(), 0.0
        for _ in range(n_sub):
            budget += len(chunks) * _HgrnChunk.TOTAL_WEIGHT / n_sub
            while budget > 0:
                weight = next(todo, None)
                if weight is None:
                    break
                budget -= weight
            sub_block()
        for _ in todo:
            pass

    @segment(6, interleave=hgrn_chunks, rs=PROJ_RS * 2)
    def _(rows, acc):
        og_ref[rows, :] = acc.astype(BF16)


PROJ_SIDE_CASTS = (128, 128)


def _proj(x, wb, lb_logits, cos, sin_signed, gain, side_weights):
    tm, tn = PROJ_TM, PROJ_TN
    m, k = x.shape
    e = PROJ_EDGES
    nj = D_IN // tn
    assert e[6] - e[5] == 1 and 2 * KV_WIDTH == tn
    assert (tm // HG_CHUNK) % (e[7] - e[6]) == 0

    def span_spec(s, width=tn):
        lo, hi = e[s], e[s + 1]
        return pl.BlockSpec((tm, width), lambda i, j: (i, jnp.clip(j - lo, 0, hi - lo - 1)))

    def sds(width, dtype):
        return jax.ShapeDtypeStruct((m, width), dtype)

    side_specs, side_shapes = [], []
    for w, steps in zip(side_weights, PROJ_SIDE_CASTS):
        slab = w.shape[0] // steps
        assert slab * steps == w.shape[0] and slab % 16 == 0 and steps <= (m // tm) * nj
        side_specs.append(pl.BlockSpec(
            (slab, w.shape[1]), lambda i, j, steps=steps: (jnp.minimum(i * nj + j, steps - 1), 0)))
        side_shapes.append(jax.ShapeDtypeStruct(w.shape, BF16))

    row_tile = lambda i, j: (i, 0)
    outs = pl.pallas_call(
        _proj_kernel,
        out_shape=(sds(HG_WIDTH, BF16), sds(ATT_WIDTH, BF16), sds(4 * KV_WIDTH, BF16),
                   sds(2 * D_MODEL, BF16), *side_shapes),
        grid=(m // tm, nj),
        in_specs=[pl.BlockSpec((tm, k), row_tile),
                  pl.BlockSpec((k, tn), lambda i, j: (0, j)),
                  pl.BlockSpec((lb_logits.shape[0], tn),
                               lambda i, j: (0, jnp.clip(j - e[1], 0, e[2] - e[1] - 1))),
                  pl.BlockSpec((tm, LANES), row_tile),
                  pl.BlockSpec((tm, LANES), row_tile),
                  pl.BlockSpec((1, HG_WIDTH), lambda i, j: (0, 0)), *side_specs],
        out_specs=(pl.BlockSpec((tm, HG_WIDTH), row_tile), span_spec(4),
                   pl.BlockSpec((tm, 4 * KV_WIDTH), row_tile), span_spec(6), *side_specs),
        scratch_shapes=[pltpu.VMEM((tm, k), BF16),
                        pltpu.VMEM((4 * HG_WIDTH // tn, tm, tn), BF16),
                        pltpu.VMEM((HG_WIDTH // tn, tm, tn), F32),
                        pltpu.VMEM((HG_HEADS, HG_DIM, HG_DIM), F32),
                        pltpu.VMEM((HGRN_CHUNKS_PER_STEP, HG_HEADS, HG_CHUNK, HG_DIM), F32),
                        pltpu.VMEM((HGRN_CHUNKS_PER_STEP, HG_HEADS, HG_CHUNK, HG_DIM), F32)],
        compiler_params=pltpu.CompilerParams(
            dimension_semantics=("arbitrary", "arbitrary"), vmem_limit_bytes=PROJ_VMEM_LIMIT),
        name="proj_hgrn2",
    )(x, wb, lb_logits, cos, sin_signed, gain, *side_weights)
    return outs[0], outs[1], outs[2], outs[3], outs[4:]


def _attn_kernel(q_ref, kvp_ref, kvc_ref, sink_ref, *refs):
    n_side = (len(refs) - 1) // 2
    side_in, o_ref, side_out = refs[:n_side], refs[n_side], refs[n_side + 1:]
    _side_cast(side_in, side_out)
    n = pl.program_id(0)
    blk = WINDOW
    nblk = q_ref.shape[0] // blk
    qi = lax.broadcasted_iota(jnp.int32, (blk, 2 * blk), 0)
    ci = lax.broadcasted_iota(jnp.int32, (blk, 2 * blk), 1)
    rel = qi + blk - ci
    band = (rel >= 0) & (rel < blk)
    lane = lax.broadcasted_iota(jnp.int32, (blk, LANES), 1)
    low = lane < ATT_DIM
    zero = jnp.zeros((), BF16)

    def keys_values(t, lanes):
        own = kvc_ref[t * blk:(t + 1) * blk, lanes]
        prev = kvp_ref[:, lanes] if t == 0 else kvc_ref[(t - 1) * blk:t * blk, lanes]
        return jnp.concatenate([prev, own], axis=0)

    units = [(t, g) for t in range(nblk) for g in range(ATT_KV_HEADS)]
    pairs = range(ATT_GROUP // 2)
    s = []
    for t, g in units:
        first_key = jnp.where(n > 0, 0, blk) if t == 0 else 0
        bias1 = jnp.where(band & (ci >= first_key), 0.0, NEG_BIG)
        bias = jnp.concatenate([bias1] * ATT_GROUP, axis=0)
        kd = keys_values(t, slice(g * LANES, (g + 1) * LANES))
        qs = []
        for pr in pairs:
            col = (g * len(pairs) + pr) * LANES
            qp = q_ref[t * blk:(t + 1) * blk, col:col + LANES]
            qs += [jnp.where(low, qp, zero), jnp.where(low, zero, qp)]
        s.append(_dot_nt(jnp.concatenate(qs, axis=0), kd) + bias)

    p, denom = [], []
    for u, (t, g) in enumerate(units):
        sink = jnp.concatenate(
            [jnp.broadcast_to(sink_ref[g * ATT_GROUP + h:g * ATT_GROUP + h + 1, :], (blk, LANES))
             for h in range(ATT_GROUP)], axis=0) * LOG2E
        m = jnp.max(jnp.maximum(s[u][:, :blk], s[u][:, blk:]), axis=-1, keepdims=True)
        m = jnp.maximum(jnp.broadcast_to(m, (ATT_GROUP * blk, LANES)), sink)
        w = jnp.exp2(s[u] - jnp.concatenate([m, m], axis=1))
        total = jnp.sum(w[:, :blk] + w[:, blk:], axis=-1, keepdims=True)
        denom.append(jnp.broadcast_to(total, (ATT_GROUP * blk, LANES)) + jnp.exp2(sink - m))
        p.append(w.astype(BF16))

    o = []
    for u, (t, g) in enumerate(units):
        vd = keys_values(t, slice(2 * KV_WIDTH + g * LANES, 2 * KV_WIDTH + (g + 1) * LANES))
        o.append(_dot(p[u], vd) / denom[u])

    for u, (t, g) in enumerate(units):
        for pr in pairs:
            out = jnp.where(low, o[u][2 * pr * blk:(2 * pr + 1) * blk],
                            o[u][(2 * pr + 1) * blk:(2 * pr + 2) * blk])
            col = (g * len(pairs) + pr) * LANES
            o_ref[t * blk:(t + 1) * blk, col:col + LANES] = out.astype(o_ref.dtype)


def _side_cast_specs(side_weights, steps):
    specs, shapes = [], []
    for w in side_weights:
        slab = w.shape[0] // steps
        assert slab * steps == w.shape[0] and slab % 16 == 0
        specs.append(pl.BlockSpec((slab, w.shape[1]), lambda n: (n, 0)))
        shapes.append(jax.ShapeDtypeStruct(w.shape, BF16))
    return specs, shapes


def _side_cast(side_in, side_out):
    for src, dst in zip(side_in, side_out):
        dst[...] = src[...].astype(BF16)


ATT_BLOCKS_PER_STEP = 4


def _attn(aq, akv, sinks_b, side_weights):
    blk, nblk = WINDOW, ATT_BLOCKS_PER_STEP
    s = aq.shape[0]
    steps = s // (blk * nblk)
    side_specs, side_shapes = _side_cast_specs(side_weights, steps)
    outs = pl.pallas_call(
        _attn_kernel,
        out_shape=(jax.ShapeDtypeStruct((s, ATT_WIDTH), BF16), *side_shapes),
        grid=(steps,),
        in_specs=[pl.BlockSpec((nblk * blk, ATT_WIDTH), lambda n: (n, 0)),
                  pl.BlockSpec((blk, 4 * KV_WIDTH), lambda n: (jnp.maximum(nblk * n - 1, 0), 0)),
                  pl.BlockSpec((nblk * blk, 4 * KV_WIDTH), lambda n: (n, 0)),
                  pl.BlockSpec((ATT_HEADS, LANES), lambda n: (0, 0)), *side_specs],
        out_specs=(pl.BlockSpec((nblk * blk, ATT_WIDTH), lambda n: (n, 0)), *side_specs),
        compiler_params=pltpu.CompilerParams(
            dimension_semantics=("parallel",), vmem_limit_bytes=VMEM_LIMIT),
        name="swa",
    )(aq, akv, akv, sinks_b, *side_weights)
    return outs[0], outs[1:]


def _layer_norm(r, gain, bias):
    mu = jnp.mean(r, axis=-1, keepdims=True)
    d = r - mu
    var = jnp.mean(d * d, axis=-1, keepdims=True)
    return d * lax.rsqrt(var + LN_EPS) * gain + bias


MERGE_TM = 512
MERGE_RS = 128


def _merge_kernel(oa_ref, ob_ref, ga_ref, gb_ref, x_ref, wa_ref, wb_ref, wo_ref, g_ref, b_ref,
                  x1_ref):
    for r in range(MERGE_TM // MERGE_RS):
        rows = slice(r * MERGE_RS, (r + 1) * MERGE_RS)
        ya = _dot(oa_ref[rows, :], wa_ref[...])
        yb = _dot(ob_ref[rows, :], wb_ref[...])
        mixed = (jax.nn.sigmoid(ga_ref[rows, :].astype(F32)) * ya
                 + jax.nn.sigmoid(gb_ref[rows, :].astype(F32)) * yb)
        z = _dot(mixed.astype(BF16), wo_ref[...])
        x1_ref[rows, :] = _layer_norm(ALPHA * x_ref[rows, :] + z, g_ref[...], b_ref[...])


def _merge(oa, ob, gates, x, wa, wb, wo, g1, b1):
    tm = MERGE_TM
    s = x.shape[0]
    row = lambda i: (i, 0)
    const = lambda i: (0, 0)
    resident = dict(pipeline_mode=pl.Buffered(1))
    return pl.pallas_call(
        _merge_kernel,
        out_shape=jax.ShapeDtypeStruct((s, D_MODEL), F32),
        grid=(s // tm,),
        in_specs=[pl.BlockSpec((tm, HG_WIDTH), row), pl.BlockSpec((tm, ATT_WIDTH), row),
                  pl.BlockSpec((tm, D_MODEL), row), pl.BlockSpec((tm, D_MODEL), lambda i: (i, 1)),
                  pl.BlockSpec((tm, D_MODEL), row),
                  pl.BlockSpec((HG_WIDTH, D_MODEL), const, **resident),
                  pl.BlockSpec((ATT_WIDTH, D_MODEL), const, **resident),
                  pl.BlockSpec((D_MODEL, D_MODEL), const, **resident),
                  pl.BlockSpec((1, D_MODEL), const), pl.BlockSpec((1, D_MODEL), const)],
        out_specs=pl.BlockSpec((tm, D_MODEL), row),
        compiler_params=pltpu.CompilerParams(
            dimension_semantics=("parallel",), vmem_limit_bytes=VMEM_LIMIT),
        name="merge_ln1",
    )(oa, ob, gates, gates, x, wa, wb, wo, g1, b1)


FFN_TM = 1024
FFN_TF = 512
FFN_RS = 256


def _ffn_kernel(x_ref, w1_ref, w2_ref, g_ref, b_ref, o_ref, xb_ref):
    acc_ref = o_ref
    j = pl.program_id(1)
    last = pl.num_programs(1) - 1

    @pl.when(j == 0)
    def _():
        xb_ref[...] = x_ref[...].astype(BF16)

    blocks = [slice(r * FFN_RS, (r + 1) * FFN_RS) for r in range(FFN_TM // FFN_RS)]

    def partials():
        hs = [jnp.maximum(_dot(xb_ref[rows, :], w1_ref[...]), 0.0) for rows in blocks]
        return [_dot((h * h).astype(BF16), w2_ref[...]) for h in hs]

    @pl.when(j == 0)
    def _():
        for rows, part in zip(blocks, partials()):
            acc_ref[rows, :] = part

    @pl.when((j > 0) & (j < last))
    def _():
        for rows, part in zip(blocks, partials()):
            acc_ref[rows, :] += part

    @pl.when(j == last)
    def _():
        for rows, part in zip(blocks, partials()):
            y = ALPHA * x_ref[rows, :] + acc_ref[rows, :] + part
            o_ref[rows, :] = _layer_norm(y, g_ref[...], b_ref[...])


def _ffn(x1, w1, w2, g2, b2):
    tm, tf = FFN_TM, FFN_TF
    assert D_FF // tf >= 2
    s = x1.shape[0]
    return pl.pallas_call(
        _ffn_kernel,
        out_shape=jax.ShapeDtypeStruct((s, D_MODEL), F32),
        grid=(s // tm, D_FF // tf),
        in_specs=[pl.BlockSpec((tm, D_MODEL), lambda i, j: (i, 0)),
                  pl.BlockSpec((D_MODEL, tf), lambda i, j: (0, j)),
                  pl.BlockSpec((tf, D_MODEL), lambda i, j: (j, 0)),
                  pl.BlockSpec((1, D_MODEL), lambda i, j: (0, 0)),
                  pl.BlockSpec((1, D_MODEL), lambda i, j: (0, 0))],
        out_specs=pl.BlockSpec((tm, D_MODEL), lambda i, j: (i, 0)),
        scratch_shapes=[pltpu.VMEM((tm, D_MODEL), BF16)],
        compiler_params=pltpu.CompilerParams(
            dimension_semantics=("parallel", "arbitrary"), vmem_limit_bytes=VMEM_LIMIT),
        name="ffn_ln2",
    )(x1, w1, w2, g2, b2)


def _rope_tables(seq):
    half = ATT_DIM // 2
    inv = ROPE_THETA ** (-np.arange(half, dtype=np.float64) / half)
    ang = np.arange(seq, dtype=np.float64)[:, None] * inv[None, :]
    cos, sin = np.cos(ang), np.sin(ang)
    cos_t = np.concatenate([cos, cos, cos, cos], axis=1).astype(np.float32)
    sin_t = np.concatenate([-sin, sin, -sin, sin], axis=1).astype(np.float32)
    return jnp.asarray(cos_t), jnp.asarray(sin_t)


def kernel(x, w_in, hg_lb_logits, hg_norm_gain, attn_sinks, w_branch_a, w_branch_b, w_out,
           ln1_gain, ln1_bias, w_ff1, w_ff2, ln2_gain, ln2_bias):
    b, s, d = x.shape
    assert (b, s, d) == (1, SEQ, D_MODEL) and w_in.shape == (1, D_MODEL, D_IN)
    x2 = x.reshape(s, d)
    cos_t, sin_t = _rope_tables(s)
    oa, aq, akv, gates, (w1, w2) = _proj(
        x2, w_in[0].astype(BF16), hg_lb_logits, cos_t, sin_t, hg_norm_gain, (w_ff1[0], w_ff2[0]))
    sinks_b = jnp.broadcast_to(attn_sinks[0][:, None], (ATT_HEADS, LANES))
    ob, (wa, wb, wo) = _attn(aq, akv, sinks_b, (w_branch_a[0], w_branch_b[0], w_out[0]))
    x1 = _merge(oa, ob, gates, x2, wa, wb, wo, ln1_gain, ln1_bias)
    out = _ffn(x1, w1, w2, ln2_gain, ln2_bias)
    return out.reshape(b, s, d)
```

```python
import math

import jax
import jax.numpy as jnp
import numpy as np
from jax import lax
from jax.experimental import pallas as pl
from jax.experimental.pallas import tpu as pltpu

F32 = jnp.float32
BF16 = jnp.bfloat16

D_MODEL = 2048
SEQ = 8192
HG_HEADS = 8
HG_DIM = 128
HG_WIDTH = HG_HEADS * HG_DIM
ATT_HEADS = 16
ATT_KV_HEADS = 4
ATT_DIM = 64
ATT_GROUP = ATT_HEADS // ATT_KV_HEADS
ATT_WIDTH = ATT_HEADS * ATT_DIM
KV_WIDTH = ATT_KV_HEADS * ATT_DIM
WINDOW = 128
ROPE_THETA = 10000.0
D_FF = 4 * D_MODEL
D_IN = 4 * HG_WIDTH + ATT_WIDTH + 2 * KV_WIDTH + 2 * D_MODEL
ALPHA = 2.0 ** 0.25
LN_EPS = 1e-5
RMS_EPS = 1e-6
LOG2E = math.log2(math.e)

LANES = 128
HG_CHUNK = 128
HG_SUB = 8
NEG_BIG = -1e30

VMEM_LIMIT = 56 * 1024 * 1024
PROJ_VMEM_LIMIT = 60 * 1024 * 1024


def _dot(a, b):
    return jnp.dot(a, b, preferred_element_type=F32)


def _dot_nt(a, b):
    return lax.dot_general(a, b, (((1,), (1,)), ((), ())), preferred_element_type=F32)


def _dot_tn(a, b):
    return lax.dot_general(a, b, (((0,), (0,)), ((), ())), preferred_element_type=F32)


def _hgrn_consts():
    C, SB = HG_CHUNK, HG_SUB
    ri = lax.broadcasted_iota(jnp.int32, (C, C), 0)
    ci = lax.broadcasted_iota(jnp.int32, (C, C), 1)
    tri = (ri >= ci).astype(BF16)
    row = lax.broadcasted_iota(jnp.int32, (C, 1), 0)
    sub_row = lax.broadcasted_iota(jnp.int32, (SB, 1), 0)
    levels = []
    half = C // 2
    while half >= SB:
        pair = ((ri // (2 * half)) == (ci // (2 * half))) & ((ri % (2 * half)) >= half) \
            & ((ci % (2 * half)) < half)
        levels.append((half, (row % (2 * half)) >= half, jnp.where(pair, 1.0, 0.0)))
        half //= 2
    return tri, sub_row, levels


class _HgrnChunk:
    WEIGHTS = dict(cumsum=8, diag=95, state=20, level=15, off_values=3, finish=40)
    N_LEVELS = (HG_CHUNK // HG_SUB).bit_length() - 1
    TOTAL_WEIGHT = HG_HEADS * (8 + 95 + 20 + N_LEVELS * 15 + 3 + 40)

    def __init__(self, consts, load, gain_ref, o_ref, r0, state_ref, c_scr, v_scr):
        self.tri, self.sub_row, self.levels = consts
        assert len(self.levels) == self.N_LEVELS
        self.load, self.gain_ref, self.o_ref, self.r0 = load, gain_ref, o_ref, r0
        self.state_ref, self.c_scr, self.v_scr = state_ref, c_scr, v_scr
        hs = range(HG_HEADS)
        self.qf = [load('q', h).astype(F32) for h in hs]
        self.kk = [load('k', h).astype(F32) for h in hs]
        self.vb = [load('v', h) for h in hs]
        self.b, self.o, self.r = [], [], []
        self.s_off = [None] * HG_HEADS

    def cumsum(self):
        for h in range(HG_HEADS):
            lf = self.load('lf', h)
            hi = lf.astype(BF16)
            lo = (lf - hi.astype(F32)).astype(BF16)
            both = _dot(self.tri, jnp.concatenate([hi, lo], axis=1))
            self.b.append((both[:, :HG_DIM] + both[:, HG_DIM:]) * LOG2E)
            yield self.WEIGHTS['cumsum']

    def state(self):
        C = HG_CHUNK
        for h in range(HG_HEADS):
            b, st = self.b[h], self.state_ref[h]
            self.o.append(_dot_nt((self.qf[h] * jnp.exp2(b)).astype(BF16), st.astype(BF16)))
            b_last = b[C - 1:C, :]
            kd = self.kk[h] * jnp.exp2(b_last - b)
            self.state_ref[h] = st * jnp.exp2(b_last) + _dot_tn(self.vb[h], kd.astype(BF16))
            yield self.WEIGHTS['state']

    def level(self, k):
        lvl, second_half, pair = self.levels[k]
        for h in range(HG_HEADS):
            b = self.b[h]
            ref = jnp.concatenate(
                [jnp.broadcast_to(b[base + lvl:base + lvl + 1, :], (2 * lvl, HG_DIM))
                 for base in range(0, HG_CHUNK, 2 * lvl)], axis=0)
            e = jnp.exp2(-jnp.abs(b - ref))
            qs = jnp.where(second_half, self.qf[h] * e, 0.0)
            ks = jnp.where(second_half, 0.0, self.kk[h] * e)
            s = _dot_nt(qs.astype(BF16), ks.astype(BF16)) * pair
            self.s_off[h] = s if self.s_off[h] is None else self.s_off[h] + s
            yield self.WEIGHTS['level']

    def off_values(self):
        for h in range(HG_HEADS):
            self.o[h] = self.o[h] + _dot(self.s_off[h].astype(BF16), self.vb[h])
            yield self.WEIGHTS['off_values']

    def diag(self):
        C, SB = HG_CHUNK, HG_SUB
        for h in range(HG_HEADS):
            b = self.b[h]
            self.c_scr[h] = b - jnp.log2(jnp.maximum(self.kk[h], 0.0))
            self.v_scr[h] = self.vb[h].astype(F32)
            scores = []
            for i in range(C // SB):
                base = i * SB
                bi = b[base:base + SB, :]
                qi = self.qf[h][base:base + SB, :]
                for j in range(SB):
                    cj = self.c_scr[h, base + j:base + j + 1, :]
                    p = qi * jnp.exp2(jnp.where(self.sub_row >= j, bi - cj, NEG_BIG))
                    scores.append(jnp.sum(p, axis=-1, keepdims=True))
            self.r.append(scores)
            yield self.WEIGHTS['diag']

    def finish(self):
        C, SB = HG_CHUNK, HG_SUB
        for h in range(HG_HEADS):
            sl = slice(h * HG_DIM, (h + 1) * HG_DIM)
            diag = []
            for i in range(C // SB):
                acc = None
                for j in range(SB):
                    term = self.r[h][i * SB + j] * self.v_scr[h, i * SB + j:i * SB + j + 1, :]
                    acc = term if acc is None else acc + term
                diag.append(acc)
            out = self.o[h] + jnp.concatenate(diag, axis=0)
            ms = jnp.mean(out * out, axis=-1, keepdims=True)
            out = out * lax.rsqrt(ms + RMS_EPS) * self.gain_ref[:, sl]
            out = out * self.load('g', h).astype(F32)
            self.o_ref[pl.ds(self.r0, C), sl] = out.astype(self.o_ref.dtype)
            yield self.WEIGHTS['finish']


PROJ_TM = 1024
PROJ_TN = 512
PROJ_RS = 128
PROJ_EDGES = tuple(c // PROJ_TN for c in (
    0, HG_WIDTH, 2 * HG_WIDTH, 3 * HG_WIDTH, 4 * HG_WIDTH, 4 * HG_WIDTH + ATT_WIDTH,
    4 * HG_WIDTH + ATT_WIDTH + 2 * KV_WIDTH, D_IN))
HGRN_SECTION = {'q': 0, 'k': 1, 'v': 2, 'g': 3}
HGRN_CHUNKS_PER_STEP = PROJ_TM // HG_CHUNK // (PROJ_EDGES[7] - PROJ_EDGES[6])
Q_SCALE = ATT_DIM ** -0.5 * LOG2E


def _rope(t, cos, sin_signed):
    w = t.shape[1]
    reps = w // cos.shape[1]
    cos_t = jnp.concatenate([cos] * reps, axis=1) if reps > 1 else cos
    sin_t = jnp.concatenate([sin_signed] * reps, axis=1) if reps > 1 else sin_signed
    lane = lax.broadcasted_iota(jnp.int32, t.shape, 1)
    first_half = (lane % ATT_DIM) < (ATT_DIM // 2)
    rot = jnp.where(first_half, pltpu.roll(t, w - ATT_DIM // 2, 1), pltpu.roll(t, ATT_DIM // 2, 1))
    return t * cos_t + rot * sin_t


def _dup_heads(t):
    lane = lax.broadcasted_iota(jnp.int32, (t.shape[0], LANES), 1)
    low = lane < ATT_DIM
    out = []
    for g in range(ATT_KV_HEADS):
        src = t[:, LANES * (g // 2):LANES * (g // 2 + 1)]
        swapped = pltpu.roll(src, ATT_DIM, 1)
        out.append(jnp.where(low, src, swapped) if g % 2 == 0 else jnp.where(low, swapped, src))
    return jnp.concatenate(out, axis=1)


def _proj_kernel(x_ref, w_ref, lbl_ref, cos_ref, sin_ref, gain_ref, *refs):
    n_side = len(PROJ_SIDE_CASTS)
    side_in = refs[:n_side]
    oa_ref, oaq_ref, okv_ref, og_ref = refs[n_side:n_side + 4]
    side_out = refs[n_side + 4:2 * n_side + 4]
    xb_ref, hp_ref, lf_ref, state_ref, c_scr, v_scr = refs[2 * n_side + 4:]
    i = pl.program_id(0)
    j = pl.program_id(1)
    e = PROJ_EDGES
    tiles_per_section = HG_WIDTH // PROJ_TN
    heads_per_tile = PROJ_TN // HG_DIM

    @pl.when(j == 0)
    def _():
        xb_ref[...] = x_ref[...].astype(BF16)

    @pl.when((i == 0) & (j == 0))
    def _():
        state_ref[...] = jnp.zeros_like(state_ref)

    def segment(s, interleave=None, rs=PROJ_RS):
        def deco(epilogue):
            @pl.when((j >= e[s]) & (j < e[s + 1]))
            def _():
                _side_cast(side_in, side_out)
                n_sub = PROJ_TM // rs
                todo = iter(range(n_sub))

                def sub_block():
                    r = next(todo)
                    rows = slice(r * rs, (r + 1) * rs)
                    epilogue(rows, _dot(xb_ref[rows, :], w_ref[...]))

                if interleave is None:
                    for _ in range(n_sub):
                        sub_block()
                else:
                    interleave(sub_block, n_sub)
                    assert next(todo, None) is None
        return deco

    @segment(0)
    def _(rows, acc):
        hp_ref[j, rows, :] = (acc * jax.nn.sigmoid(acc)).astype(BF16)

    @segment(1)
    def _(rows, acc):
        lbl = lbl_ref[...]
        lexp = jnp.exp(lbl - jnp.max(lbl, axis=0, keepdims=True))
        lb = lexp[0:1, :] / jnp.sum(lexp, axis=0, keepdims=True)
        f = lb + (1.0 - lb) * jax.nn.sigmoid(acc)
        lf_ref[j - e[1], rows, :] = jnp.log(f)
        hp_ref[j, rows, :] = (1.0 - f).astype(BF16)

    @segment(2)
    def _(rows, acc):
        hp_ref[j, rows, :] = acc.astype(BF16)

    @segment(3)
    def _(rows, acc):
        hp_ref[j, rows, :] = (acc * jax.nn.sigmoid(acc)).astype(BF16)

    @segment(4)
    def _(rows, acc):
        oaq_ref[rows, :] = (_rope(acc, cos_ref[rows, :], sin_ref[rows, :]) * Q_SCALE).astype(BF16)

    @segment(5)
    def _(rows, acc):
        k = _rope(acc[:, :KV_WIDTH], cos_ref[rows, :], sin_ref[rows, :])
        okv_ref[rows, :] = jnp.concatenate(
            [_dup_heads(k), _dup_heads(acc[:, KV_WIDTH:])], axis=1).astype(BF16)

    def hgrn_chunks(sub_block, n_sub):
        consts = _hgrn_consts()
        chunks = []
        for u in range(HGRN_CHUNKS_PER_STEP):
            r0 = pl.multiple_of(((j - e[6]) * HGRN_CHUNKS_PER_STEP + u) * HG_CHUNK, HG_CHUNK)

            def load(name, h, r0=r0):
                lanes = slice((h % heads_per_tile) * HG_DIM, (h % heads_per_tile + 1) * HG_DIM)
                if name == 'lf':
                    return lf_ref[h // heads_per_tile, pl.ds(r0, HG_CHUNK), lanes]
                tile = HGRN_SECTION[name] * tiles_per_section + h // heads_per_tile
                return hp_ref[tile, pl.ds(r0, HG_CHUNK), lanes]

            chunks.append(_HgrnChunk(consts, load, gain_ref, oa_ref, r0, state_ref,
                                     c_scr.at[u], v_scr.at[u]))

        def stages():
            for c in chunks:
                yield from c.cumsum()
            for c in chunks:
                yield from c.diag()
            for c in chunks:
                yield from c.state()
                for k in range(_HgrnChunk.N_LEVELS):
                    yield from c.level(k)
                yield from c.off_values()
                yield from c.finish()

        todo, budget = stages(), 0.0
        for _ in range(n_sub):
            budget += len(chunks) * _HgrnChunk.TOTAL_WEIGHT / n_sub
            while budget > 0:
                weight = next(todo, None)
                if weight is None:
                    break
                budget -= weight
            sub_block()
        for _ in todo:
            pass

    @segment(6, interleave=hgrn_chunks, rs=PROJ_RS * 2)
    def _(rows, acc):
        og_ref[rows, :] = acc.astype(BF16)


PROJ_SIDE_CASTS = (128, 128)


def _proj(x, wb, lb_logits, cos, sin_signed, gain, side_weights):
    tm, tn = PROJ_TM, PROJ_TN
    m, k = x.shape
    e = PROJ_EDGES
    nj = D_IN // tn
    assert e[6] - e[5] == 1 and 2 * KV_WIDTH == tn
    assert (tm // HG_CHUNK) % (e[7] - e[6]) == 0

    def span_spec(s, width=tn):
        lo, hi = e[s], e[s + 1]
        return pl.BlockSpec((tm, width), lambda i, j: (i, jnp.clip(j - lo, 0, hi - lo - 1)))

    def sds(width, dtype):
        return jax.ShapeDtypeStruct((m, width), dtype)

    side_specs, side_shapes = [], []
    for w, steps in zip(side_weights, PROJ_SIDE_CASTS):
        slab = w.shape[0] // steps
        assert slab * steps == w.shape[0] and slab % 16 == 0 and steps <= (m // tm) * nj
        side_specs.append(pl.BlockSpec(
            (slab, w.shape[1]), lambda i, j, steps=steps: (jnp.minimum(i * nj + j, steps - 1), 0)))
        side_shapes.append(jax.ShapeDtypeStruct(w.shape, BF16))

    row_tile = lambda i, j: (i, 0)
    outs = pl.pallas_call(
        _proj_kernel,
        out_shape=(sds(HG_WIDTH, BF16), sds(ATT_WIDTH, BF16), sds(4 * KV_WIDTH, BF16),
                   sds(2 * D_MODEL, BF16), *side_shapes),
        grid=(m // tm, nj),
        in_specs=[pl.BlockSpec((tm, k), row_tile),
                  pl.BlockSpec((k, tn), lambda i, j: (0, j)),
                  pl.BlockSpec((lb_logits.shape[0], tn),
                               lambda i, j: (0, jnp.clip(j - e[1], 0, e[2] - e[1] - 1))),
                  pl.BlockSpec((tm, LANES), row_tile),
                  pl.BlockSpec((tm, LANES), row_tile),
                  pl.BlockSpec((1, HG_WIDTH), lambda i, j: (0, 0)), *side_specs],
        out_specs=(pl.BlockSpec((tm, HG_WIDTH), row_tile), span_spec(4),
                   pl.BlockSpec((tm, 4 * KV_WIDTH), row_tile), span_spec(6), *side_specs),
        scratch_shapes=[pltpu.VMEM((tm, k), BF16),
                        pltpu.VMEM((4 * HG_WIDTH // tn, tm, tn), BF16),
                        pltpu.VMEM((HG_WIDTH // tn, tm, tn), F32),
                        pltpu.VMEM((HG_HEADS, HG_DIM, HG_DIM), F32),
                        pltpu.VMEM((HGRN_CHUNKS_PER_STEP, HG_HEADS, HG_CHUNK, HG_DIM), F32),
                        pltpu.VMEM((HGRN_CHUNKS_PER_STEP, HG_HEADS, HG_CHUNK, HG_DIM), F32)],
        compiler_params=pltpu.CompilerParams(
            dimension_semantics=("arbitrary", "arbitrary"), vmem_limit_bytes=PROJ_VMEM_LIMIT),
        name="proj_hgrn2",
    )(x, wb, lb_logits, cos, sin_signed, gain, *side_weights)
    return outs[0], outs[1], outs[2], outs[3], outs[4:]


def _attn_kernel(q_ref, kvp_ref, kvc_ref, sink_ref, *refs):
    n_side = (len(refs) - 1) // 2
    side_in, o_ref, side_out = refs[:n_side], refs[n_side], refs[n_side + 1:]
    _side_cast(side_in, side_out)
    n = pl.program_id(0)
    blk = WINDOW
    nblk = q_ref.shape[0] // blk
    qi = lax.broadcasted_iota(jnp.int32, (blk, 2 * blk), 0)
    ci = lax.broadcasted_iota(jnp.int32, (blk, 2 * blk), 1)
    rel = qi + blk - ci
    band = (rel >= 0) & (rel < blk)
    lane = lax.broadcasted_iota(jnp.int32, (blk, LANES), 1)
    low = lane < ATT_DIM
    zero = jnp.zeros((), BF16)

    def keys_values(t, lanes):
        own = kvc_ref[t * blk:(t + 1) * blk, lanes]
        prev = kvp_ref[:, lanes] if t == 0 else kvc_ref[(t - 1) * blk:t * blk, lanes]
        return jnp.concatenate([prev, own], axis=0)

    units = [(t, g) for t in range(nblk) for g in range(ATT_KV_HEADS)]
    pairs = range(ATT_GROUP // 2)
    s = []
    for t, g in units:
        first_key = jnp.where(n > 0, 0, blk) if t == 0 else 0
        bias1 = jnp.where(band & (ci >= first_key), 0.0, NEG_BIG)
        bias = jnp.concatenate([bias1] * ATT_GROUP, axis=0)
        kd = keys_values(t, slice(g * LANES, (g + 1) * LANES))
        qs = []
        for pr in pairs:
            col = (g * len(pairs) + pr) * LANES
            qp = q_ref[t * blk:(t + 1) * blk, col:col + LANES]
            qs += [jnp.where(low, qp, zero), jnp.where(low, zero, qp)]
        s.append(_dot_nt(jnp.concatenate(qs, axis=0), kd) + bias)

    p, denom = [], []
    for u, (t, g) in enumerate(units):
        sink = jnp.concatenate(
            [jnp.broadcast_to(sink_ref[g * ATT_GROUP + h:g * ATT_GROUP + h + 1, :], (blk, LANES))
             for h in range(ATT_GROUP)], axis=0) * LOG2E
        m = jnp.max(jnp.maximum(s[u][:, :blk], s[u][:, blk:]), axis=-1, keepdims=True)
        m = jnp.maximum(jnp.broadcast_to(m, (ATT_GROUP * blk, LANES)), sink)
        w = jnp.exp2(s[u] - jnp.concatenate([m, m], axis=1))
        total = jnp.sum(w[:, :blk] + w[:, blk:], axis=-1, keepdims=True)
        denom.append(jnp.broadcast_to(total, (ATT_GROUP * blk, LANES)) + jnp.exp2(sink - m))
        p.append(w.astype(BF16))

    o = []
    for u, (t, g) in enumerate(units):
        vd = keys_values(t, slice(2 * KV_WIDTH + g * LANES, 2 * KV_WIDTH + (g + 1) * LANES))
        o.append(_dot(p[u], vd) / denom[u])

    for u, (t, g) in enumerate(units):
        for pr in pairs:
            out = jnp.where(low, o[u][2 * pr * blk:(2 * pr + 1) * blk],
                            o[u][(2 * pr + 1) * blk:(2 * pr + 2) * blk])
            col = (g * len(pairs) + pr) * LANES
            o_ref[t * blk:(t + 1) * blk, col:col + LANES] = out.astype(o_ref.dtype)


def _side_cast_specs(side_weights, steps):
    specs, shapes = [], []
    for w in side_weights:
        slab = w.shape[0] // steps
        assert slab * steps == w.shape[0] and slab % 16 == 0
        specs.append(pl.BlockSpec((slab, w.shape[1]), lambda n: (n, 0)))
        shapes.append(jax.ShapeDtypeStruct(w.shape, BF16))
    return specs, shapes


def _side_cast(side_in, side_out):
    for src, dst in zip(side_in, side_out):
        dst[...] = src[...].astype(BF16)


ATT_BLOCKS_PER_STEP = 4


def _attn(aq, akv, sinks_b, side_weights):
    blk, nblk = WINDOW, ATT_BLOCKS_PER_STEP
    s = aq.shape[0]
    steps = s // (blk * nblk)
    side_specs, side_shapes = _side_cast_specs(side_weights, steps)
    outs = pl.pallas_call(
        _attn_kernel,
        out_shape=(jax.ShapeDtypeStruct((s, ATT_WIDTH), BF16), *side_shapes),
        grid=(steps,),
        in_specs=[pl.BlockSpec((nblk * blk, ATT_WIDTH), lambda n: (n, 0)),
                  pl.BlockSpec((blk, 4 * KV_WIDTH), lambda n: (jnp.maximum(nblk * n - 1, 0), 0)),
                  pl.BlockSpec((nblk * blk, 4 * KV_WIDTH), lambda n: (n, 0)),
                  pl.BlockSpec((ATT_HEADS, LANES), lambda n: (0, 0)), *side_specs],
        out_specs=(pl.BlockSpec((nblk * blk, ATT_WIDTH), lambda n: (n, 0)), *side_specs),
        compiler_params=pltpu.CompilerParams(
            dimension_semantics=("parallel",), vmem_limit_bytes=VMEM_LIMIT),
        name="swa",
    )(aq, akv, akv, sinks_b, *side_weights)
    return outs[0], outs[1:]


def _layer_norm(r, gain, bias):
    mu = jnp.mean(r, axis=-1, keepdims=True)
    d = r - mu
    var = jnp.mean(d * d, axis=-1, keepdims=True)
    return d * lax.rsqrt(var + LN_EPS) * gain + bias


MERGE_TM = 512
MERGE_RS = 128


def _merge_kernel(oa_ref, ob_ref, ga_ref, gb_ref, x_ref, wa_ref, wb_ref, wo_ref, g_ref, b_ref,
                  x1_ref):
    for r in range(MERGE_TM // MERGE_RS):
        rows = slice(r * MERGE_RS, (r + 1) * MERGE_RS)
        ya = _dot(oa_ref[rows, :], wa_ref[...])
        yb = _dot(ob_ref[rows, :], wb_ref[...])
        mixed = (jax.nn.sigmoid(ga_ref[rows, :].astype(F32)) * ya
                 + jax.nn.sigmoid(gb_ref[rows, :].astype(F32)) * yb)
        z = _dot(mixed.astype(BF16), wo_ref[...])
        x1_ref[rows, :] = _layer_norm(ALPHA * x_ref[rows, :] + z, g_ref[...], b_ref[...])


def _merge(oa, ob, gates, x, wa, wb, wo, g1, b1):
    tm = MERGE_TM
    s = x.shape[0]
    row = lambda i: (i, 0)
    const = lambda i: (0, 0)
    resident = dict(pipeline_mode=pl.Buffered(1))
    return pl.pallas_call(
        _merge_kernel,
        out_shape=jax.ShapeDtypeStruct((s, D_MODEL), F32),
        grid=(s // tm,),
        in_specs=[pl.BlockSpec((tm, HG_WIDTH), row), pl.BlockSpec((tm, ATT_WIDTH), row),
                  pl.BlockSpec((tm, D_MODEL), row), pl.BlockSpec((tm, D_MODEL), lambda i: (i, 1)),
                  pl.BlockSpec((tm, D_MODEL), row),
                  pl.BlockSpec((HG_WIDTH, D_MODEL), const, **resident),
                  pl.BlockSpec((ATT_WIDTH, D_MODEL), const, **resident),
                  pl.BlockSpec((D_MODEL, D_MODEL), const, **resident),
                  pl.BlockSpec((1, D_MODEL), const), pl.BlockSpec((1, D_MODEL), const)],
        out_specs=pl.BlockSpec((tm, D_MODEL), row),
        compiler_params=pltpu.CompilerParams(
            dimension_semantics=("parallel",), vmem_limit_bytes=VMEM_LIMIT),
        name="merge_ln1",
    )(oa, ob, gates, gates, x, wa, wb, wo, g1, b1)


FFN_TM = 1024
FFN_TF = 512
FFN_RS = 256


def _ffn_kernel(x_ref, w1_ref, w2_ref, g_ref, b_ref, o_ref, xb_ref):
    acc_ref = o_ref
    j = pl.program_id(1)
    last = pl.num_programs(1) - 1

    @pl.when(j == 0)
    def _():
        xb_ref[...] = x_ref[...].astype(BF16)

    blocks = [slice(r * FFN_RS, (r + 1) * FFN_RS) for r in range(FFN_TM // FFN_RS)]

    def partials():
        hs = [jnp.maximum(_dot(xb_ref[rows, :], w1_ref[...]), 0.0) for rows in blocks]
        return [_dot((h * h).astype(BF16), w2_ref[...]) for h in hs]

    @pl.when(j == 0)
    def _():
        for rows, part in zip(blocks, partials()):
            acc_ref[rows, :] = part

    @pl.when((j > 0) & (j < last))
    def _():
        for rows, part in zip(blocks, partials()):
            acc_ref[rows, :] += part

    @pl.when(j == last)
    def _():
        for rows, part in zip(blocks, partials()):
            y = ALPHA * x_ref[rows, :] + acc_ref[rows, :] + part
            o_ref[rows, :] = _layer_norm(y, g_ref[...], b_ref[...])


def _ffn(x1, w1, w2, g2, b2):
    tm, tf = FFN_TM, FFN_TF
    assert D_FF // tf >= 2
    s = x1.shape[0]
    return pl.pallas_call(
        _ffn_kernel,
        out_shape=jax.ShapeDtypeStruct((s, D_MODEL), F32),
        grid=(s // tm, D_FF // tf),
        in_specs=[pl.BlockSpec((tm, D_MODEL), lambda i, j: (i, 0)),
                  pl.BlockSpec((D_MODEL, tf), lambda i, j: (0, j)),
                  pl.BlockSpec((tf, D_MODEL), lambda i, j: (j, 0)),
                  pl.BlockSpec((1, D_MODEL), lambda i, j: (0, 0)),
                  pl.BlockSpec((1, D_MODEL), lambda i, j: (0, 0))],
        out_specs=pl.BlockSpec((tm, D_MODEL), lambda i, j: (i, 0)),
        scratch_shapes=[pltpu.VMEM((tm, D_MODEL), BF16)],
        compiler_params=pltpu.CompilerParams(
            dimension_semantics=("parallel", "arbitrary"), vmem_limit_bytes=VMEM_LIMIT),
        name="ffn_ln2",
    )(x1, w1, w2, g2, b2)


def _rope_tables(seq):
    half = ATT_DIM // 2
    inv = ROPE_THETA ** (-np.arange(half, dtype=np.float64) / half)
    ang = np.arange(seq, dtype=np.float64)[:, None] * inv[None, :]
    cos, sin = np.cos(ang), np.sin(ang)
    cos_t = np.concatenate([cos, cos, cos, cos], axis=1).astype(np.float32)
    sin_t = np.concatenate([-sin, sin, -sin, sin], axis=1).astype(np.float32)
    return jnp.asarray(cos_t), jnp.asarray(sin_t)


def kernel(x, w_in, hg_lb_logits, hg_norm_gain, attn_sinks, w_branch_a, w_branch_b, w_out,
           ln1_gain, ln1_bias, w_ff1, w_ff2, ln2_gain, ln2_bias):
    b, s, d = x.shape
    assert (b, s, d) == (1, SEQ, D_MODEL) and w_in.shape == (1, D_MODEL, D_IN)
    x2 = x.reshape(s, d)
    cos_t, sin_t = _rope_tables(s)
    oa, aq, akv, gates, (w1, w2) = _proj(
        x2, w_in[0].astype(BF16), hg_lb_logits, cos_t, sin_t, hg_norm_gain, (w_ff1[0], w_ff2[0]))
    sinks_b = jnp.broadcast_to(attn_sinks[0][:, None], (ATT_HEADS, LANES))
    ob, (wa, wb, wo) = _attn(aq, akv, sinks_b, (w_branch_a[0], w_branch_b[0], w_out[0]))
    x1 = _merge(oa, ob, gates, x2, wa, wb, wo, ln1_gain, ln1_bias)
    out = _ffn(x1, w1, w2, ln2_gain, ln2_bias)
    return out.reshape(b, s, d)
```

```python
import math

import jax
import jax.numpy as jnp
import numpy as np
from jax import lax
from jax.experimental import pallas as pl
from jax.experimental.pallas import tpu as pltpu

F32 = jnp.float32
BF16 = jnp.bfloat16

D_MODEL = 2048
SEQ = 8192
HG_HEADS = 8
HG_DIM = 128
HG_WIDTH = HG_HEADS * HG_DIM
ATT_HEADS = 16
ATT_KV_HEADS = 4
ATT_DIM = 64
ATT_GROUP = ATT_HEADS // ATT_KV_HEADS
ATT_WIDTH = ATT_HEADS * ATT_DIM
KV_WIDTH = ATT_KV_HEADS * ATT_DIM
WINDOW = 128
ROPE_THETA = 10000.0
D_FF = 4 * D_MODEL
D_IN = 4 * HG_WIDTH + ATT_WIDTH + 2 * KV_WIDTH + 2 * D_MODEL
ALPHA = 2.0 ** 0.25
LN_EPS = 1e-5
RMS_EPS = 1e-6
LOG2E = math.log2(math.e)

LANES = 128
HG_CHUNK = 128
HG_SUB = 8
NEG_BIG = -1e30

VMEM_LIMIT = 56 * 1024 * 1024
PROJ_VMEM_LIMIT = 60 * 1024 * 1024


def _dot(a, b):
    return jnp.dot(a, b, preferred_element_type=F32)


def _dot_nt(a, b):
    return lax.dot_general(a, b, (((1,), (1,)), ((), ())), preferred_element_type=F32)


def _dot_tn(a, b):
    return lax.dot_general(a, b, (((0,), (0,)), ((), ())), preferred_element_type=F32)


def _hgrn_consts():
    C, SB = HG_CHUNK, HG_SUB
    ri = lax.broadcasted_iota(jnp.int32, (C, C), 0)
    ci = lax.broadcasted_iota(jnp.int32, (C, C), 1)
    tri = (ri >= ci).astype(BF16)
    row = lax.broadcasted_iota(jnp.int32, (C, 1), 0)
    sub_row = lax.broadcasted_iota(jnp.int32, (SB, 1), 0)
    levels = []
    half = C // 2
    while half >= SB:
        pair = ((ri // (2 * half)) == (ci // (2 * half))) & ((ri % (2 * half)) >= half) \
            & ((ci % (2 * half)) < half)
        levels.append((half, (row % (2 * half)) >= half, jnp.where(pair, 1.0, 0.0)))
        half //= 2
    return tri, sub_row, levels


class _HgrnChunk:
    WEIGHTS = dict(cumsum=8, diag=95, state=20, level=15, off_values=3, finish=40)
    N_LEVELS = (HG_CHUNK // HG_SUB).bit_length() - 1
    TOTAL_WEIGHT = HG_HEADS * (8 + 95 + 20 + N_LEVELS * 15 + 3 + 40)

    def __init__(self, consts, load, gain_ref, o_ref, r0, state_ref, c_scr, v_scr):
        self.tri, self.sub_row, self.levels = consts
        assert len(self.levels) == self.N_LEVELS
        self.load, self.gain_ref, self.o_ref, self.r0 = load, gain_ref, o_ref, r0
        self.state_ref, self.c_scr, self.v_scr = state_ref, c_scr, v_scr
        hs = range(HG_HEADS)
        self.qf = [load('q', h).astype(F32) for h in hs]
        self.kk = [load('k', h).astype(F32) for h in hs]
        self.vb = [load('v', h) for h in hs]
        self.b, self.o, self.r = [], [], []
        self.s_off = [None] * HG_HEADS

    def cumsum(self):
        for h in range(HG_HEADS):
            lf = self.load('lf', h)
            hi = lf.astype(BF16)
            lo = (lf - hi.astype(F32)).astype(BF16)
            both = _dot(self.tri, jnp.concatenate([hi, lo], axis=1))
            self.b.append((both[:, :HG_DIM] + both[:, HG_DIM:]) * LOG2E)
            yield self.WEIGHTS['cumsum']

    def state(self):
        C = HG_CHUNK
        for h in range(HG_HEADS):
            b, st = self.b[h], self.state_ref[h]
            vt = self.vb[h].T
            self.o.append(((self.qf[h] * jnp.exp2(b)).astype(BF16),
                           jnp.concatenate([st.astype(BF16), vt], axis=1)))
            b_last = b[C - 1:C, :]
            kd = self.kk[h] * jnp.exp2(b_last - b)
            self.state_ref[h] = st * jnp.exp2(b_last) + _dot(vt, kd.astype(BF16))
            yield self.WEIGHTS['state']

    def level(self, k):
        lvl, second_half, pair = self.levels[k]
        for h in range(HG_HEADS):
            b = self.b[h]
            ref = jnp.concatenate(
                [jnp.broadcast_to(b[base + lvl:base + lvl + 1, :], (2 * lvl, HG_DIM))
                 for base in range(0, HG_CHUNK, 2 * lvl)], axis=0)
            e = jnp.exp2(-jnp.abs(b - ref))
            qs = jnp.where(second_half, self.qf[h] * e, 0.0)
            ks = jnp.where(second_half, 0.0, self.kk[h] * e)
            s = _dot_nt(qs.astype(BF16), ks.astype(BF16)) * pair
            self.s_off[h] = s if self.s_off[h] is None else self.s_off[h] + s
            yield self.WEIGHTS['level']

    def off_values(self):
        for h in range(HG_HEADS):
            qe, state_and_vt = self.o[h]
            lhs = jnp.concatenate([qe, self.s_off[h].astype(BF16)], axis=1)
            self.o[h] = _dot_nt(lhs, state_and_vt)
            yield self.WEIGHTS['off_values']

    def diag(self):
        C, SB = HG_CHUNK, HG_SUB
        for h in range(HG_HEADS):
            b = self.b[h]
            self.c_scr[h] = b - jnp.log2(jnp.maximum(self.kk[h], 0.0))
            self.v_scr[h] = self.vb[h].astype(F32)
            scores = []
            for i in range(C // SB):
                base = i * SB
                bi = b[base:base + SB, :]
                qi = self.qf[h][base:base + SB, :]
                for j in range(SB):
                    cj = self.c_scr[h, base + j:base + j + 1, :]
                    p = qi * jnp.exp2(jnp.where(self.sub_row >= j, bi - cj, NEG_BIG))
                    scores.append(jnp.sum(p, axis=-1, keepdims=True))
            self.r.append(scores)
            yield self.WEIGHTS['diag']

    def finish(self):
        C, SB = HG_CHUNK, HG_SUB
        for h in range(HG_HEADS):
            sl = slice(h * HG_DIM, (h + 1) * HG_DIM)
            diag = []
            for i in range(C // SB):
                acc = None
                for j in range(SB):
                    term = self.r[h][i * SB + j] * self.v_scr[h, i * SB + j:i * SB + j + 1, :]
                    acc = term if acc is None else acc + term
                diag.append(acc)
            out = self.o[h] + jnp.concatenate(diag, axis=0)
            ms = jnp.mean(out * out, axis=-1, keepdims=True)
            out = out * lax.rsqrt(ms + RMS_EPS) * self.gain_ref[:, sl]
            out = out * self.load('g', h).astype(F32)
            self.o_ref[pl.ds(self.r0, C), sl] = out.astype(self.o_ref.dtype)
            yield self.WEIGHTS['finish']


PROJ_TM = 1024
PROJ_TN = 512
PROJ_RS = 128
PROJ_EDGES = tuple(c // PROJ_TN for c in (
    0, HG_WIDTH, 2 * HG_WIDTH, 3 * HG_WIDTH, 4 * HG_WIDTH, 4 * HG_WIDTH + ATT_WIDTH,
    4 * HG_WIDTH + ATT_WIDTH + 2 * KV_WIDTH, D_IN))
HGRN_SECTION = {'q': 0, 'k': 1, 'v': 2, 'g': 3}
HGRN_CHUNKS_PER_STEP = PROJ_TM // HG_CHUNK // (PROJ_EDGES[7] - PROJ_EDGES[6])
Q_SCALE = ATT_DIM ** -0.5 * LOG2E


def _rope(t, cos, sin_signed):
    w = t.shape[1]
    reps = w // cos.shape[1]
    cos_t = jnp.concatenate([cos] * reps, axis=1) if reps > 1 else cos
    sin_t = jnp.concatenate([sin_signed] * reps, axis=1) if reps > 1 else sin_signed
    lane = lax.broadcasted_iota(jnp.int32, t.shape, 1)
    first_half = (lane % ATT_DIM) < (ATT_DIM // 2)
    rot = jnp.where(first_half, pltpu.roll(t, w - ATT_DIM // 2, 1), pltpu.roll(t, ATT_DIM // 2, 1))
    return t * cos_t + rot * sin_t


def _dup_heads(t):
    lane = lax.broadcasted_iota(jnp.int32, (t.shape[0], LANES), 1)
    low = lane < ATT_DIM
    out = []
    for g in range(ATT_KV_HEADS):
        src = t[:, LANES * (g // 2):LANES * (g // 2 + 1)]
        swapped = pltpu.roll(src, ATT_DIM, 1)
        out.append(jnp.where(low, src, swapped) if g % 2 == 0 else jnp.where(low, swapped, src))
    return jnp.concatenate(out, axis=1)


def _proj_kernel(x_ref, w_ref, lbl_ref, cos_ref, sin_ref, gain_ref, *refs):
    n_side = len(PROJ_SIDE_CASTS)
    side_in = refs[:n_side]
    oa_ref, oaq_ref, okv_ref, og_ref = refs[n_side:n_side + 4]
    side_out = refs[n_side + 4:2 * n_side + 4]
    xb_ref, hp_ref, lf_ref, state_ref, c_scr, v_scr = refs[2 * n_side + 4:]
    i = pl.program_id(0)
    j = pl.program_id(1)
    e = PROJ_EDGES
    tiles_per_section = HG_WIDTH // PROJ_TN
    heads_per_tile = PROJ_TN // HG_DIM

    @pl.when(j == 0)
    def _():
        xb_ref[...] = x_ref[...].astype(BF16)

    @pl.when((i == 0) & (j == 0))
    def _():
        state_ref[...] = jnp.zeros_like(state_ref)

    def segment(s, interleave=None, rs=PROJ_RS):
        def deco(epilogue):
            @pl.when((j >= e[s]) & (j < e[s + 1]))
            def _():
                _side_cast(side_in, side_out)
                n_sub = PROJ_TM // rs
                todo = iter(range(n_sub))

                def sub_block():
                    r = next(todo)
                    rows = slice(r * rs, (r + 1) * rs)
                    epilogue(rows, _dot(xb_ref[rows, :], w_ref[...]))

                if interleave is None:
                    for _ in range(n_sub):
                        sub_block()
                else:
                    interleave(sub_block, n_sub)
                    assert next(todo, None) is None
        return deco

    @segment(0)
    def _(rows, acc):
        hp_ref[j, rows, :] = (acc * jax.nn.sigmoid(acc)).astype(BF16)

    @segment(1)
    def _(rows, acc):
        lbl = lbl_ref[...]
        lexp = jnp.exp(lbl - jnp.max(lbl, axis=0, keepdims=True))
        lb = lexp[0:1, :] / jnp.sum(lexp, axis=0, keepdims=True)
        f = lb + (1.0 - lb) * jax.nn.sigmoid(acc)
        lf_ref[j - e[1], rows, :] = jnp.log(f)
        hp_ref[j, rows, :] = (1.0 - f).astype(BF16)

    @segment(2)
    def _(rows, acc):
        hp_ref[j, rows, :] = acc.astype(BF16)

    @segment(3)
    def _(rows, acc):
        hp_ref[j, rows, :] = (acc * jax.nn.sigmoid(acc)).astype(BF16)

    @segment(4)
    def _(rows, acc):
        oaq_ref[rows, :] = (_rope(acc, cos_ref[rows, :], sin_ref[rows, :]) * Q_SCALE).astype(BF16)

    @segment(5)
    def _(rows, acc):
        k = _rope(acc[:, :KV_WIDTH], cos_ref[rows, :], sin_ref[rows, :])
        okv_ref[rows, :] = jnp.concatenate(
            [_dup_heads(k), _dup_heads(acc[:, KV_WIDTH:])], axis=1).astype(BF16)

    def hgrn_chunks(sub_block, n_sub):
        consts = _hgrn_consts()
        chunks = []
        for u in range(HGRN_CHUNKS_PER_STEP):
            r0 = pl.multiple_of(((j - e[6]) * HGRN_CHUNKS_PER_STEP + u) * HG_CHUNK, HG_CHUNK)

            def load(name, h, r0=r0):
                lanes = slice((h % heads_per_tile) * HG_DIM, (h % heads_per_tile + 1) * HG_DIM)
                if name == 'lf':
                    return lf_ref[h // heads_per_tile, pl.ds(r0, HG_CHUNK), lanes]
                tile = HGRN_SECTION[name] * tiles_per_section + h // heads_per_tile
                return hp_ref[tile, pl.ds(r0, HG_CHUNK), lanes]

            chunks.append(_HgrnChunk(consts, load, gain_ref, oa_ref, r0, state_ref,
                                     c_scr.at[u], v_scr.at[u]))

        def stages():
            for c in chunks:
                yield from c.cumsum()
            for c in chunks:
                yield from c.diag()
            for c in chunks:
                yield from c.state()
                for k in range(_HgrnChunk.N_LEVELS):
                    yield from c.level(k)
                yield from c.off_values()
                yield from c.finish()

        todo, budget = stages(), 0.0
        for _ in range(n_sub):
            budget += len(chunks) * _HgrnChunk.TOTAL_WEIGHT / n_sub
            while budget > 0:
                weight = next(todo, None)
                if weight is None:
                    break
                budget -= weight
            sub_block()
        for _ in todo:
            pass

    @segment(6, interleave=hgrn_chunks, rs=PROJ_RS * 2)
    def _(rows, acc):
        og_ref[rows, :] = acc.astype(BF16)


PROJ_SIDE_CASTS = (128, 128)


def _proj(x, wb, lb_logits, cos, sin_signed, gain, side_weights):
    tm, tn = PROJ_TM, PROJ_TN
    m, k = x.shape
    e = PROJ_EDGES
    nj = D_IN // tn
    assert e[6] - e[5] == 1 and 2 * KV_WIDTH == tn
    assert (tm // HG_CHUNK) % (e[7] - e[6]) == 0

    def span_spec(s, width=tn):
        lo, hi = e[s], e[s + 1]
        return pl.BlockSpec((tm, width), lambda i, j: (i, jnp.clip(j - lo, 0, hi - lo - 1)))

    def sds(width, dtype):
        return jax.ShapeDtypeStruct((m, width), dtype)

    side_specs, side_shapes = [], []
    for w, steps in zip(side_weights, PROJ_SIDE_CASTS):
        slab = w.shape[0] // steps
        assert slab * steps == w.shape[0] and slab % 16 == 0 and steps <= (m // tm) * nj
        side_specs.append(pl.BlockSpec(
            (slab, w.shape[1]), lambda i, j, steps=steps: (jnp.minimum(i * nj + j, steps - 1), 0)))
        side_shapes.append(jax.ShapeDtypeStruct(w.shape, BF16))

    row_tile = lambda i, j: (i, 0)
    outs = pl.pallas_call(
        _proj_kernel,
        out_shape=(sds(HG_WIDTH, BF16), sds(ATT_WIDTH, BF16), sds(4 * KV_WIDTH, BF16),
                   sds(2 * D_MODEL, BF16), *side_shapes),
        grid=(m // tm, nj),
        in_specs=[pl.BlockSpec((tm, k), row_tile),
                  pl.BlockSpec((k, tn), lambda i, j: (0, j)),
                  pl.BlockSpec((lb_logits.shape[0], tn),
                               lambda i, j: (0, jnp.clip(j - e[1], 0, e[2] - e[1] - 1))),
                  pl.BlockSpec((tm, LANES), row_tile),
                  pl.BlockSpec((tm, LANES), row_tile),
                  pl.BlockSpec((1, HG_WIDTH), lambda i, j: (0, 0)), *side_specs],
        out_specs=(pl.BlockSpec((tm, HG_WIDTH), row_tile), span_spec(4),
                   pl.BlockSpec((tm, 4 * KV_WIDTH), row_tile), span_spec(6), *side_specs),
        scratch_shapes=[pltpu.VMEM((tm, k), BF16),
                        pltpu.VMEM((4 * HG_WIDTH // tn, tm, tn), BF16),
                        pltpu.VMEM((HG_WIDTH // tn, tm, tn), F32),
                        pltpu.VMEM((HG_HEADS, HG_DIM, HG_DIM), F32),
                        pltpu.VMEM((HGRN_CHUNKS_PER_STEP, HG_HEADS, HG_CHUNK, HG_DIM), F32),
                        pltpu.VMEM((HGRN_CHUNKS_PER_STEP, HG_HEADS, HG_CHUNK, HG_DIM), F32)],
        compiler_params=pltpu.CompilerParams(
            dimension_semantics=("arbitrary", "arbitrary"), vmem_limit_bytes=PROJ_VMEM_LIMIT),
        name="proj_hgrn2",
    )(x, wb, lb_logits, cos, sin_signed, gain, *side_weights)
    return outs[0], outs[1], outs[2], outs[3], outs[4:]


def _attn_kernel(q_ref, kvp_ref, kvc_ref, sink_ref, *refs):
    n_side = (len(refs) - 1) // 2
    side_in, o_ref, side_out = refs[:n_side], refs[n_side], refs[n_side + 1:]
    _side_cast(side_in, side_out)
    n = pl.program_id(0)
    blk = WINDOW
    nblk = q_ref.shape[0] // blk
    qi = lax.broadcasted_iota(jnp.int32, (blk, 2 * blk), 0)
    ci = lax.broadcasted_iota(jnp.int32, (blk, 2 * blk), 1)
    rel = qi + blk - ci
    band = (rel >= 0) & (rel < blk)
    lane = lax.broadcasted_iota(jnp.int32, (blk, LANES), 1)
    low = lane < ATT_DIM
    zero = jnp.zeros((), BF16)

    def keys_values(t, lanes):
        own = kvc_ref[t * blk:(t + 1) * blk, lanes]
        prev = kvp_ref[:, lanes] if t == 0 else kvc_ref[(t - 1) * blk:t * blk, lanes]
        return jnp.concatenate([prev, own], axis=0)

    units = [(t, g) for t in range(nblk) for g in range(ATT_KV_HEADS)]
    pairs = range(ATT_GROUP // 2)
    s = []
    for t, g in units:
        first_key = jnp.where(n > 0, 0, blk) if t == 0 else 0
        bias1 = jnp.where(band & (ci >= first_key), 0.0, NEG_BIG)
        bias = jnp.concatenate([bias1] * ATT_GROUP, axis=0)
        kd = keys_values(t, slice(g * LANES, (g + 1) * LANES))
        qs = []
        for pr in pairs:
            col = (g * len(pairs) + pr) * LANES
            qp = q_ref[t * blk:(t + 1) * blk, col:col + LANES]
            qs += [jnp.where(low, qp, zero), jnp.where(low, zero, qp)]
        s.append(_dot_nt(jnp.concatenate(qs, axis=0), kd) + bias)

    p, denom = [], []
    for u, (t, g) in enumerate(units):
        sink = jnp.concatenate(
            [jnp.broadcast_to(sink_ref[g * ATT_GROUP + h:g * ATT_GROUP + h + 1, :], (blk, LANES))
             for h in range(ATT_GROUP)], axis=0) * LOG2E
        m = jnp.max(jnp.maximum(s[u][:, :blk], s[u][:, blk:]), axis=-1, keepdims=True)
        m = jnp.maximum(jnp.broadcast_to(m, (ATT_GROUP * blk, LANES)), sink)
        w = jnp.exp2(s[u] - jnp.concatenate([m, m], axis=1))
        total = jnp.sum(w[:, :blk] + w[:, blk:], axis=-1, keepdims=True)
        denom.append(jnp.broadcast_to(total, (ATT_GROUP * blk, LANES)) + jnp.exp2(sink - m))
        p.append(w.astype(BF16))

    o = []
    for u, (t, g) in enumerate(units):
        vd = keys_values(t, slice(2 * KV_WIDTH + g * LANES, 2 * KV_WIDTH + (g + 1) * LANES))
        o.append(_dot(p[u], vd) / denom[u])

    for u, (t, g) in enumerate(units):
        for pr in pairs:
            out = jnp.where(low, o[u][2 * pr * blk:(2 * pr + 1) * blk],
                            o[u][(2 * pr + 1) * blk:(2 * pr + 2) * blk])
            col = (g * len(pairs) + pr) * LANES
            o_ref[t * blk:(t + 1) * blk, col:col + LANES] = out.astype(o_ref.dtype)


def _side_cast_specs(side_weights, steps):
    specs, shapes = [], []
    for w in side_weights:
        slab = w.shape[0] // steps
        assert slab * steps == w.shape[0] and slab % 16 == 0
        specs.append(pl.BlockSpec((slab, w.shape[1]), lambda n: (n, 0)))
        shapes.append(jax.ShapeDtypeStruct(w.shape, BF16))
    return specs, shapes


def _side_cast(side_in, side_out):
    for src, dst in zip(side_in, side_out):
        dst[...] = src[...].astype(BF16)


ATT_BLOCKS_PER_STEP = 4


def _attn(aq, akv, sinks_b, side_weights):
    blk, nblk = WINDOW, ATT_BLOCKS_PER_STEP
    s = aq.shape[0]
    steps = s // (blk * nblk)
    side_specs, side_shapes = _side_cast_specs(side_weights, steps)
    outs = pl.pallas_call(
        _attn_kernel,
        out_shape=(jax.ShapeDtypeStruct((s, ATT_WIDTH), BF16), *side_shapes),
        grid=(steps,),
        in_specs=[pl.BlockSpec((nblk * blk, ATT_WIDTH), lambda n: (n, 0)),
                  pl.BlockSpec((blk, 4 * KV_WIDTH), lambda n: (jnp.maximum(nblk * n - 1, 0), 0)),
                  pl.BlockSpec((nblk * blk, 4 * KV_WIDTH), lambda n: (n, 0)),
                  pl.BlockSpec((ATT_HEADS, LANES), lambda n: (0, 0)), *side_specs],
        out_specs=(pl.BlockSpec((nblk * blk, ATT_WIDTH), lambda n: (n, 0)), *side_specs),
        compiler_params=pltpu.CompilerParams(
            dimension_semantics=("parallel",), vmem_limit_bytes=VMEM_LIMIT),
        name="swa",
    )(aq, akv, akv, sinks_b, *side_weights)
    return outs[0], outs[1:]


def _layer_norm(r, gain, bias):
    mu = jnp.mean(r, axis=-1, keepdims=True)
    d = r - mu
    var = jnp.mean(d * d, axis=-1, keepdims=True)
    return d * lax.rsqrt(var + LN_EPS) * gain + bias


MERGE_TM = 512
MERGE_RS = 128


def _merge_kernel(oa_ref, ob_ref, ga_ref, gb_ref, x_ref, wa_ref, wb_ref, wo_ref, g_ref, b_ref,
                  x1_ref):
    for r in range(MERGE_TM // MERGE_RS):
        rows = slice(r * MERGE_RS, (r + 1) * MERGE_RS)
        ya = _dot(oa_ref[rows, :], wa_ref[...])
        yb = _dot(ob_ref[rows, :], wb_ref[...])
        mixed = (jax.nn.sigmoid(ga_ref[rows, :].astype(F32)) * ya
                 + jax.nn.sigmoid(gb_ref[rows, :].astype(F32)) * yb)
        z = _dot(mixed.astype(BF16), wo_ref[...])
        x1_ref[rows, :] = _layer_norm(ALPHA * x_ref[rows, :] + z, g_ref[...], b_ref[...])


def _merge(oa, ob, gates, x, wa, wb, wo, g1, b1):
    tm = MERGE_TM
    s = x.shape[0]
    row = lambda i: (i, 0)
    const = lambda i: (0, 0)
    resident = dict(pipeline_mode=pl.Buffered(1))
    return pl.pallas_call(
        _merge_kernel,
        out_shape=jax.ShapeDtypeStruct((s, D_MODEL), F32),
        grid=(s // tm,),
        in_specs=[pl.BlockSpec((tm, HG_WIDTH), row), pl.BlockSpec((tm, ATT_WIDTH), row),
                  pl.BlockSpec((tm, D_MODEL), row), pl.BlockSpec((tm, D_MODEL), lambda i: (i, 1)),
                  pl.BlockSpec((tm, D_MODEL), row),
                  pl.BlockSpec((HG_WIDTH, D_MODEL), const, **resident),
                  pl.BlockSpec((ATT_WIDTH, D_MODEL), const, **resident),
                  pl.BlockSpec((D_MODEL, D_MODEL), const, **resident),
                  pl.BlockSpec((1, D_MODEL), const), pl.BlockSpec((1, D_MODEL), const)],
        out_specs=pl.BlockSpec((tm, D_MODEL), row),
        compiler_params=pltpu.CompilerParams(
            dimension_semantics=("parallel",), vmem_limit_bytes=VMEM_LIMIT),
        name="merge_ln1",
    )(oa, ob, gates, gates, x, wa, wb, wo, g1, b1)


FFN_TM = 1024
FFN_TF = 512
FFN_RS = 256


def _ffn_kernel(x_ref, w1_ref, w2_ref, g_ref, b_ref, o_ref, xb_ref):
    acc_ref = o_ref
    j = pl.program_id(1)
    last = pl.num_programs(1) - 1

    @pl.when(j == 0)
    def _():
        xb_ref[...] = x_ref[...].astype(BF16)

    blocks = [slice(r * FFN_RS, (r + 1) * FFN_RS) for r in range(FFN_TM // FFN_RS)]

    def partials():
        hs = [jnp.maximum(_dot(xb_ref[rows, :], w1_ref[...]), 0.0) for rows in blocks]
        return [_dot((h * h).astype(BF16), w2_ref[...]) for h in hs]

    @pl.when(j == 0)
    def _():
        for rows, part in zip(blocks, partials()):
            acc_ref[rows, :] = part

    @pl.when((j > 0) & (j < last))
    def _():
        for rows, part in zip(blocks, partials()):
            acc_ref[rows, :] += part

    @pl.when(j == last)
    def _():
        for rows, part in zip(blocks, partials()):
            y = ALPHA * x_ref[rows, :] + acc_ref[rows, :] + part
            o_ref[rows, :] = _layer_norm(y, g_ref[...], b_ref[...])


def _ffn(x1, w1, w2, g2, b2):
    tm, tf = FFN_TM, FFN_TF
    assert D_FF // tf >= 2
    s = x1.shape[0]
    return pl.pallas_call(
        _ffn_kernel,
        out_shape=jax.ShapeDtypeStruct((s, D_MODEL), F32),
        grid=(s // tm, D_FF // tf),
        in_specs=[pl.BlockSpec((tm, D_MODEL), lambda i, j: (i, 0)),
                  pl.BlockSpec((D_MODEL, tf), lambda i, j: (0, j)),
                  pl.BlockSpec((tf, D_MODEL), lambda i, j: (j, 0)),
                  pl.BlockSpec((1, D_MODEL), lambda i, j: (0, 0)),
                  pl.BlockSpec((1, D_MODEL), lambda i, j: (0, 0))],
        out_specs=pl.BlockSpec((tm, D_MODEL), lambda i, j: (i, 0)),
        scratch_shapes=[pltpu.VMEM((tm, D_MODEL), BF16)],
        compiler_params=pltpu.CompilerParams(
            dimension_semantics=("parallel", "arbitrary"), vmem_limit_bytes=VMEM_LIMIT),
        name="ffn_ln2",
    )(x1, w1, w2, g2, b2)


def _rope_tables(seq):
    half = ATT_DIM // 2
    inv = ROPE_THETA ** (-np.arange(half, dtype=np.float64) / half)
    ang = np.arange(seq, dtype=np.float64)[:, None] * inv[None, :]
    cos, sin = np.cos(ang), np.sin(ang)
    cos_t = np.concatenate([cos, cos, cos, cos], axis=1).astype(np.float32)
    sin_t = np.concatenate([-sin, sin, -sin, sin], axis=1).astype(np.float32)
    return jnp.asarray(cos_t), jnp.asarray(sin_t)


def kernel(x, w_in, hg_lb_logits, hg_norm_gain, attn_sinks, w_branch_a, w_branch_b, w_out,
           ln1_gain, ln1_bias, w_ff1, w_ff2, ln2_gain, ln2_bias):
    b, s, d = x.shape
    assert (b, s, d) == (1, SEQ, D_MODEL) and w_in.shape == (1, D_MODEL, D_IN)
    x2 = x.reshape(s, d)
    cos_t, sin_t = _rope_tables(s)
    oa, aq, akv, gates, (w1, w2) = _proj(
        x2, w_in[0].astype(BF16), hg_lb_logits, cos_t, sin_t, hg_norm_gain, (w_ff1[0], w_ff2[0]))
    sinks_b = jnp.broadcast_to(attn_sinks[0][:, None], (ATT_HEADS, LANES))
    ob, (wa, wb, wo) = _attn(aq, akv, sinks_b, (w_branch_a[0], w_branch_b[0], w_out[0]))
    x1 = _merge(oa, ob, gates, x2, wa, wb, wo, ln1_gain, ln1_bias)
    out = _ffn(x1, w1, w2, ln2_gain, ln2_bias)
    return out.reshape(b, s, d)
```

```python
import math

import jax
import jax.numpy as jnp
import numpy as np
from jax import lax
from jax.experimental import pallas as pl
from jax.experimental.pallas import tpu as pltpu

F32 = jnp.float32
BF16 = jnp.bfloat16

D_MODEL = 2048
SEQ = 8192
HG_HEADS = 8
HG_DIM = 128
HG_WIDTH = HG_HEADS * HG_DIM
ATT_HEADS = 16
ATT_KV_HEADS = 4
ATT_DIM = 64
ATT_GROUP = ATT_HEADS // ATT_KV_HEADS
ATT_WIDTH = ATT_HEADS * ATT_DIM
KV_WIDTH = ATT_KV_HEADS * ATT_DIM
WINDOW = 128
ROPE_THETA = 10000.0
D_FF = 4 * D_MODEL
D_IN = 4 * HG_WIDTH + ATT_WIDTH + 2 * KV_WIDTH + 2 * D_MODEL
ALPHA = 2.0 ** 0.25
LN_EPS = 1e-5
RMS_EPS = 1e-6
LOG2E = math.log2(math.e)

LANES = 128
HG_CHUNK = 128
HG_SUB = 8
NEG_BIG = -1e30

VMEM_LIMIT = 56 * 1024 * 1024
PROJ_VMEM_LIMIT = 60 * 1024 * 1024


def _dot(a, b):
    return jnp.dot(a, b, preferred_element_type=F32)


def _dot_nt(a, b):
    return lax.dot_general(a, b, (((1,), (1,)), ((), ())), preferred_element_type=F32)


def _dot_tn(a, b):
    return lax.dot_general(a, b, (((0,), (0,)), ((), ())), preferred_element_type=F32)


def _hgrn_consts():
    C, SB = HG_CHUNK, HG_SUB
    ri = lax.broadcasted_iota(jnp.int32, (C, C), 0)
    ci = lax.broadcasted_iota(jnp.int32, (C, C), 1)
    tri = (ri >= ci).astype(BF16)
    row = lax.broadcasted_iota(jnp.int32, (C, 1), 0)
    sub_row = lax.broadcasted_iota(jnp.int32, (SB, 1), 0)
    levels = []
    half = C // 2
    while half >= SB:
        pair = ((ri // (2 * half)) == (ci // (2 * half))) & ((ri % (2 * half)) >= half) \
            & ((ci % (2 * half)) < half)
        levels.append((half, (row % (2 * half)) >= half, jnp.where(pair, 1.0, 0.0)))
        half //= 2
    return tri, sub_row, levels


class _HgrnChunk:
    WEIGHTS = dict(cumsum=8, diag=200, state=20, level=15, off_values=3, finish=40)
    N_LEVELS = (HG_CHUNK // HG_SUB).bit_length() - 1
    TOTAL_WEIGHT = HG_HEADS * (WEIGHTS['cumsum'] + WEIGHTS['diag'] + WEIGHTS['state']
                               + N_LEVELS * WEIGHTS['level'] + WEIGHTS['off_values']
                               + WEIGHTS['finish'])

    def __init__(self, consts, load, gain_ref, o_ref, r0, state_ref, c_scr, v_scr):
        self.tri, self.sub_row, self.levels = consts
        assert len(self.levels) == self.N_LEVELS
        self.load, self.gain_ref, self.o_ref, self.r0 = load, gain_ref, o_ref, r0
        self.state_ref, self.c_scr, self.v_scr = state_ref, c_scr, v_scr
        hs = range(HG_HEADS)
        self.qf = [load('q', h).astype(F32) for h in hs]
        self.kk = [load('k', h).astype(F32) for h in hs]
        self.vb = [load('v', h) for h in hs]
        self.b, self.o, self.r = [], [], []
        self.s_off = [None] * HG_HEADS

    def cumsum(self):
        for h in range(HG_HEADS):
            lf = self.load('lf', h)
            hi = lf.astype(BF16)
            lo = (lf - hi.astype(F32)).astype(BF16)
            both = _dot(self.tri, jnp.concatenate([hi, lo], axis=1))
            self.b.append((both[:, :HG_DIM] + both[:, HG_DIM:]) * LOG2E)
            yield self.WEIGHTS['cumsum']

    def state(self):
        C = HG_CHUNK
        for h in range(HG_HEADS):
            b, st = self.b[h], self.state_ref[h]
            vt = self.vb[h].T
            self.o.append(((self.qf[h] * jnp.exp2(b)).astype(BF16),
                           jnp.concatenate([st.astype(BF16), vt], axis=1)))
            b_last = b[C - 1:C, :]
            kd = self.kk[h] * jnp.exp2(b_last - b)
            self.state_ref[h] = st * jnp.exp2(b_last) + _dot(vt, kd.astype(BF16))
            yield self.WEIGHTS['state']

    def level(self, k):
        lvl, second_half, pair = self.levels[k]
        for h in range(HG_HEADS):
            b = self.b[h]
            ref = jnp.concatenate(
                [jnp.broadcast_to(b[base + lvl:base + lvl + 1, :], (2 * lvl, HG_DIM))
                 for base in range(0, HG_CHUNK, 2 * lvl)], axis=0)
            e = jnp.exp2(-jnp.abs(b - ref))
            qs = jnp.where(second_half, self.qf[h] * e, 0.0)
            ks = jnp.where(second_half, 0.0, self.kk[h] * e)
            s = _dot_nt(qs.astype(BF16), ks.astype(BF16)) * pair
            self.s_off[h] = s if self.s_off[h] is None else self.s_off[h] + s
            yield self.WEIGHTS['level']

    def off_values(self):
        for h in range(HG_HEADS):
            qe, state_and_vt = self.o[h]
            lhs = jnp.concatenate([qe, self.s_off[h].astype(BF16)], axis=1)
            self.o[h] = _dot_nt(lhs, state_and_vt)
            yield self.WEIGHTS['off_values']

    def diag(self):
        C, SB = HG_CHUNK, HG_SUB
        for h in range(HG_HEADS):
            b = self.b[h]
            self.c_scr[h] = b - jnp.log2(jnp.maximum(self.kk[h], 0.0))
            self.v_scr[h] = self.vb[h].astype(F32)
            scores = []
            for i in range(C // SB):
                base = i * SB
                bi = b[base:base + SB, :]
                qi = self.qf[h][base:base + SB, :]
                for j in range(SB):
                    cj = self.c_scr[h, base + j:base + j + 1, :]
                    p = qi * jnp.exp2(jnp.where(self.sub_row >= j, bi - cj, NEG_BIG))
                    scores.append(jnp.sum(p, axis=-1, keepdims=True))
            self.r.append(scores)
            yield self.WEIGHTS['diag']

    def finish(self):
        C, SB = HG_CHUNK, HG_SUB
        for h in range(HG_HEADS):
            sl = slice(h * HG_DIM, (h + 1) * HG_DIM)
            diag = []
            for i in range(C // SB):
                acc = None
                for j in range(SB):
                    term = self.r[h][i * SB + j] * self.v_scr[h, i * SB + j:i * SB + j + 1, :]
                    acc = term if acc is None else acc + term
                diag.append(acc)
            out = self.o[h] + jnp.concatenate(diag, axis=0)
            ms = jnp.mean(out * out, axis=-1, keepdims=True)
            out = out * lax.rsqrt(ms + RMS_EPS) * self.gain_ref[:, sl]
            out = out * self.load('g', h).astype(F32)
            self.o_ref[pl.ds(self.r0, C), sl] = out.astype(self.o_ref.dtype)
            yield self.WEIGHTS['finish']


PROJ_TM = 1024
PROJ_TN = 512
PROJ_RS = 128
PROJ_EDGES = tuple(c // PROJ_TN for c in (
    0, HG_WIDTH, 2 * HG_WIDTH, 3 * HG_WIDTH, 4 * HG_WIDTH, 4 * HG_WIDTH + ATT_WIDTH,
    4 * HG_WIDTH + ATT_WIDTH + 2 * KV_WIDTH, D_IN))
HGRN_SECTION = {'q': 0, 'k': 1, 'v': 2, 'g': 3}
HGRN_CHUNKS_PER_STEP = PROJ_TM // HG_CHUNK // (PROJ_EDGES[7] - PROJ_EDGES[6])
Q_SCALE = ATT_DIM ** -0.5 * LOG2E


def _rope(t, cos, sin_signed):
    w = t.shape[1]
    reps = w // cos.shape[1]
    cos_t = jnp.concatenate([cos] * reps, axis=1) if reps > 1 else cos
    sin_t = jnp.concatenate([sin_signed] * reps, axis=1) if reps > 1 else sin_signed
    lane = lax.broadcasted_iota(jnp.int32, t.shape, 1)
    first_half = (lane % ATT_DIM) < (ATT_DIM // 2)
    rot = jnp.where(first_half, pltpu.roll(t, w - ATT_DIM // 2, 1), pltpu.roll(t, ATT_DIM // 2, 1))
    return t * cos_t + rot * sin_t


def _dup_heads(t):
    lane = lax.broadcasted_iota(jnp.int32, (t.shape[0], LANES), 1)
    low = lane < ATT_DIM
    out = []
    for g in range(ATT_KV_HEADS):
        src = t[:, LANES * (g // 2):LANES * (g // 2 + 1)]
        swapped = pltpu.roll(src, ATT_DIM, 1)
        out.append(jnp.where(low, src, swapped) if g % 2 == 0 else jnp.where(low, swapped, src))
    return jnp.concatenate(out, axis=1)


def _proj_kernel(x_ref, w_ref, lbl_ref, cos_ref, sin_ref, gain_ref, *refs):
    n_side = len(PROJ_SIDE_CASTS)
    side_in = refs[:n_side]
    oa_ref, oaq_ref, okv_ref, og_ref = refs[n_side:n_side + 4]
    side_out = refs[n_side + 4:2 * n_side + 4]
    xb_ref, hp_ref, lf_ref, state_ref, c_scr, v_scr = refs[2 * n_side + 4:]
    i = pl.program_id(0)
    j = pl.program_id(1)
    e = PROJ_EDGES
    tiles_per_section = HG_WIDTH // PROJ_TN
    heads_per_tile = PROJ_TN // HG_DIM

    @pl.when(j == 0)
    def _():
        xb_ref[...] = x_ref[...].astype(BF16)

    @pl.when((i == 0) & (j == 0))
    def _():
        state_ref[...] = jnp.zeros_like(state_ref)

    def segment(s, interleave=None, rs=PROJ_RS):
        def deco(epilogue):
            @pl.when((j >= e[s]) & (j < e[s + 1]))
            def _():
                _side_cast(side_in, side_out)
                n_sub = PROJ_TM // rs
                todo = iter(range(n_sub))

                def sub_block():
                    r = next(todo)
                    rows = slice(r * rs, (r + 1) * rs)
                    epilogue(rows, _dot(xb_ref[rows, :], w_ref[...]))

                if interleave is None:
                    for _ in range(n_sub):
                        sub_block()
                else:
                    interleave(sub_block, n_sub)
                    assert next(todo, None) is None
        return deco

    @segment(0)
    def _(rows, acc):
        hp_ref[j, rows, :] = (acc * jax.nn.sigmoid(acc)).astype(BF16)

    @segment(1)
    def _(rows, acc):
        lbl = lbl_ref[...]
        lexp = jnp.exp(lbl - jnp.max(lbl, axis=0, keepdims=True))
        lb = lexp[0:1, :] / jnp.sum(lexp, axis=0, keepdims=True)
        f = lb + (1.0 - lb) * jax.nn.sigmoid(acc)
        lf_ref[j - e[1], rows, :] = jnp.log(f)
        hp_ref[j, rows, :] = (1.0 - f).astype(BF16)

    @segment(2)
    def _(rows, acc):
        hp_ref[j, rows, :] = acc.astype(BF16)

    @segment(3)
    def _(rows, acc):
        hp_ref[j, rows, :] = (acc * jax.nn.sigmoid(acc)).astype(BF16)

    @segment(4)
    def _(rows, acc):
        oaq_ref[rows, :] = (_rope(acc, cos_ref[rows, :], sin_ref[rows, :]) * Q_SCALE).astype(BF16)

    @segment(5)
    def _(rows, acc):
        k = _rope(acc[:, :KV_WIDTH], cos_ref[rows, :], sin_ref[rows, :])
        okv_ref[rows, :] = jnp.concatenate(
            [_dup_heads(k), _dup_heads(acc[:, KV_WIDTH:])], axis=1).astype(BF16)

    def hgrn_chunks(sub_block, n_sub):
        consts = _hgrn_consts()
        chunks = []
        for u in range(HGRN_CHUNKS_PER_STEP):
            r0 = pl.multiple_of(((j - e[6]) * HGRN_CHUNKS_PER_STEP + u) * HG_CHUNK, HG_CHUNK)

            def load(name, h, r0=r0):
                lanes = slice((h % heads_per_tile) * HG_DIM, (h % heads_per_tile + 1) * HG_DIM)
                if name == 'lf':
                    return lf_ref[h // heads_per_tile, pl.ds(r0, HG_CHUNK), lanes]
                tile = HGRN_SECTION[name] * tiles_per_section + h // heads_per_tile
                return hp_ref[tile, pl.ds(r0, HG_CHUNK), lanes]

            chunks.append(_HgrnChunk(consts, load, gain_ref, oa_ref, r0, state_ref,
                                     c_scr.at[u], v_scr.at[u]))

        def stages():
            for c in chunks:
                yield from c.cumsum()
            for c in chunks:
                yield from c.diag()
            for c in chunks:
                yield from c.state()
                for k in range(_HgrnChunk.N_LEVELS):
                    yield from c.level(k)
                yield from c.off_values()
                yield from c.finish()

        todo, budget = stages(), 0.0
        for _ in range(n_sub):
            budget += len(chunks) * _HgrnChunk.TOTAL_WEIGHT / n_sub
            while budget > 0:
                weight = next(todo, None)
                if weight is None:
                    break
                budget -= weight
            sub_block()
        for _ in todo:
            pass

    @segment(6, interleave=hgrn_chunks, rs=PROJ_RS * 2)
    def _(rows, acc):
        og_ref[rows, :] = acc.astype(BF16)


PROJ_SIDE_CASTS = (128, 128)


def _proj(x, wb, lb_logits, cos, sin_signed, gain, side_weights):
    tm, tn = PROJ_TM, PROJ_TN
    m, k = x.shape
    e = PROJ_EDGES
    nj = D_IN // tn
    assert e[6] - e[5] == 1 and 2 * KV_WIDTH == tn
    assert (tm // HG_CHUNK) % (e[7] - e[6]) == 0

    def span_spec(s, width=tn):
        lo, hi = e[s], e[s + 1]
        return pl.BlockSpec((tm, width), lambda i, j: (i, jnp.clip(j - lo, 0, hi - lo - 1)))

    def sds(width, dtype):
        return jax.ShapeDtypeStruct((m, width), dtype)

    side_specs, side_shapes = [], []
    for w, steps in zip(side_weights, PROJ_SIDE_CASTS):
        slab = w.shape[0] // steps
        assert slab * steps == w.shape[0] and slab % 16 == 0 and steps <= (m // tm) * nj
        side_specs.append(pl.BlockSpec(
            (slab, w.shape[1]), lambda i, j, steps=steps: (jnp.minimum(i * nj + j, steps - 1), 0)))
        side_shapes.append(jax.ShapeDtypeStruct(w.shape, BF16))

    row_tile = lambda i, j: (i, 0)
    outs = pl.pallas_call(
        _proj_kernel,
        out_shape=(sds(HG_WIDTH, BF16), sds(ATT_WIDTH, BF16), sds(4 * KV_WIDTH, BF16),
                   sds(2 * D_MODEL, BF16), *side_shapes),
        grid=(m // tm, nj),
        in_specs=[pl.BlockSpec((tm, k), row_tile),
                  pl.BlockSpec((k, tn), lambda i, j: (0, j)),
                  pl.BlockSpec((lb_logits.shape[0], tn),
                               lambda i, j: (0, jnp.clip(j - e[1], 0, e[2] - e[1] - 1))),
                  pl.BlockSpec((tm, LANES), row_tile),
                  pl.BlockSpec((tm, LANES), row_tile),
                  pl.BlockSpec((1, HG_WIDTH), lambda i, j: (0, 0)), *side_specs],
        out_specs=(pl.BlockSpec((tm, HG_WIDTH), row_tile), span_spec(4),
                   pl.BlockSpec((tm, 4 * KV_WIDTH), row_tile), span_spec(6), *side_specs),
        scratch_shapes=[pltpu.VMEM((tm, k), BF16),
                        pltpu.VMEM((4 * HG_WIDTH // tn, tm, tn), BF16),
                        pltpu.VMEM((HG_WIDTH // tn, tm, tn), F32),
                        pltpu.VMEM((HG_HEADS, HG_DIM, HG_DIM), F32),
                        pltpu.VMEM((HGRN_CHUNKS_PER_STEP, HG_HEADS, HG_CHUNK, HG_DIM), F32),
                        pltpu.VMEM((HGRN_CHUNKS_PER_STEP, HG_HEADS, HG_CHUNK, HG_DIM), F32)],
        compiler_params=pltpu.CompilerParams(
            dimension_semantics=("arbitrary", "arbitrary"), vmem_limit_bytes=PROJ_VMEM_LIMIT),
        name="proj_hgrn2",
    )(x, wb, lb_logits, cos, sin_signed, gain, *side_weights)
    return outs[0], outs[1], outs[2], outs[3], outs[4:]


def _attn_kernel(q_ref, kvp_ref, kvc_ref, sink_ref, *refs):
    n_side = (len(refs) - 1) // 2
    side_in, o_ref, side_out = refs[:n_side], refs[n_side], refs[n_side + 1:]
    _side_cast(side_in, side_out)
    n = pl.program_id(0)
    blk = WINDOW
    nblk = q_ref.shape[0] // blk
    qi = lax.broadcasted_iota(jnp.int32, (blk, 2 * blk), 0)
    ci = lax.broadcasted_iota(jnp.int32, (blk, 2 * blk), 1)
    rel = qi + blk - ci
    band = (rel >= 0) & (rel < blk)
    lane = lax.broadcasted_iota(jnp.int32, (blk, LANES), 1)
    low = lane < ATT_DIM
    zero = jnp.zeros((), BF16)

    def keys_values(t, lanes):
        own = kvc_ref[t * blk:(t + 1) * blk, lanes]
        prev = kvp_ref[:, lanes] if t == 0 else kvc_ref[(t - 1) * blk:t * blk, lanes]
        return jnp.concatenate([prev, own], axis=0)

    units = [(t, g) for t in range(nblk) for g in range(ATT_KV_HEADS)]
    pairs = range(ATT_GROUP // 2)
    s = []
    for t, g in units:
        first_key = jnp.where(n > 0, 0, blk) if t == 0 else 0
        bias1 = jnp.where(band & (ci >= first_key), 0.0, NEG_BIG)
        bias = jnp.concatenate([bias1] * ATT_GROUP, axis=0)
        kd = keys_values(t, slice(g * LANES, (g + 1) * LANES))
        qs = []
        for pr in pairs:
            col = (g * len(pairs) + pr) * LANES
            qp = q_ref[t * blk:(t + 1) * blk, col:col + LANES]
            qs += [jnp.where(low, qp, zero), jnp.where(low, zero, qp)]
        s.append(_dot_nt(jnp.concatenate(qs, axis=0), kd) + bias)

    p, denom = [], []
    for u, (t, g) in enumerate(units):
        sink = jnp.concatenate(
            [jnp.broadcast_to(sink_ref[g * ATT_GROUP + h:g * ATT_GROUP + h + 1, :], (blk, LANES))
             for h in range(ATT_GROUP)], axis=0) * LOG2E
        m = jnp.max(jnp.maximum(s[u][:, :blk], s[u][:, blk:]), axis=-1, keepdims=True)
        m = jnp.maximum(jnp.broadcast_to(m, (ATT_GROUP * blk, LANES)), sink)
        w = jnp.exp2(s[u] - jnp.concatenate([m, m], axis=1))
        total = jnp.sum(w[:, :blk] + w[:, blk:], axis=-1, keepdims=True)
        denom.append(jnp.broadcast_to(total, (ATT_GROUP * blk, LANES)) + jnp.exp2(sink - m))
        p.append(w.astype(BF16))

    o = []
    for u, (t, g) in enumerate(units):
        vd = keys_values(t, slice(2 * KV_WIDTH + g * LANES, 2 * KV_WIDTH + (g + 1) * LANES))
        o.append(_dot(p[u], vd) / denom[u])

    for u, (t, g) in enumerate(units):
        for pr in pairs:
            out = jnp.where(low, o[u][2 * pr * blk:(2 * pr + 1) * blk],
                            o[u][(2 * pr + 1) * blk:(2 * pr + 2) * blk])
            col = (g * len(pairs) + pr) * LANES
            o_ref[t * blk:(t + 1) * blk, col:col + LANES] = out.astype(o_ref.dtype)


def _side_cast_specs(side_weights, steps):
    specs, shapes = [], []
    for w in side_weights:
        slab = w.shape[0] // steps
        assert slab * steps == w.shape[0] and slab % 16 == 0
        specs.append(pl.BlockSpec((slab, w.shape[1]), lambda n: (n, 0)))
        shapes.append(jax.ShapeDtypeStruct(w.shape, BF16))
    return specs, shapes


def _side_cast(side_in, side_out):
    for src, dst in zip(side_in, side_out):
        dst[...] = src[...].astype(BF16)


ATT_BLOCKS_PER_STEP = 4


def _attn(aq, akv, sinks_b, side_weights):
    blk, nblk = WINDOW, ATT_BLOCKS_PER_STEP
    s = aq.shape[0]
    steps = s // (blk * nblk)
    side_specs, side_shapes = _side_cast_specs(side_weights, steps)
    outs = pl.pallas_call(
        _attn_kernel,
        out_shape=(jax.ShapeDtypeStruct((s, ATT_WIDTH), BF16), *side_shapes),
        grid=(steps,),
        in_specs=[pl.BlockSpec((nblk * blk, ATT_WIDTH), lambda n: (n, 0)),
                  pl.BlockSpec((blk, 4 * KV_WIDTH), lambda n: (jnp.maximum(nblk * n - 1, 0), 0)),
                  pl.BlockSpec((nblk * blk, 4 * KV_WIDTH), lambda n: (n, 0)),
                  pl.BlockSpec((ATT_HEADS, LANES), lambda n: (0, 0)), *side_specs],
        out_specs=(pl.BlockSpec((nblk * blk, ATT_WIDTH), lambda n: (n, 0)), *side_specs),
        compiler_params=pltpu.CompilerParams(
            dimension_semantics=("parallel",), vmem_limit_bytes=VMEM_LIMIT),
        name="swa",
    )(aq, akv, akv, sinks_b, *side_weights)
    return outs[0], outs[1:]


def _layer_norm(r, gain, bias):
    mu = jnp.mean(r, axis=-1, keepdims=True)
    d = r - mu
    var = jnp.mean(d * d, axis=-1, keepdims=True)
    return d * lax.rsqrt(var + LN_EPS) * gain + bias


MERGE_TM = 512
MERGE_RS = 128


def _merge_kernel(oa_ref, ob_ref, ga_ref, gb_ref, x_ref, wa_ref, wb_ref, wo_ref, g_ref, b_ref,
                  x1_ref):
    for r in range(MERGE_TM // MERGE_RS):
        rows = slice(r * MERGE_RS, (r + 1) * MERGE_RS)
        ya = _dot(oa_ref[rows, :], wa_ref[...])
        yb = _dot(ob_ref[rows, :], wb_ref[...])
        mixed = (jax.nn.sigmoid(ga_ref[rows, :].astype(F32)) * ya
                 + jax.nn.sigmoid(gb_ref[rows, :].astype(F32)) * yb)
        z = _dot(mixed.astype(BF16), wo_ref[...])
        x1_ref[rows, :] = _layer_norm(ALPHA * x_ref[rows, :] + z, g_ref[...], b_ref[...])


def _merge(oa, ob, gates, x, wa, wb, wo, g1, b1):
    tm = MERGE_TM
    s = x.shape[0]
    row = lambda i: (i, 0)
    const = lambda i: (0, 0)
    resident = dict(pipeline_mode=pl.Buffered(1))
    return pl.pallas_call(
        _merge_kernel,
        out_shape=jax.ShapeDtypeStruct((s, D_MODEL), F32),
        grid=(s // tm,),
        in_specs=[pl.BlockSpec((tm, HG_WIDTH), row), pl.BlockSpec((tm, ATT_WIDTH), row),
                  pl.BlockSpec((tm, D_MODEL), row), pl.BlockSpec((tm, D_MODEL), lambda i: (i, 1)),
                  pl.BlockSpec((tm, D_MODEL), row),
                  pl.BlockSpec((HG_WIDTH, D_MODEL), const, **resident),
                  pl.BlockSpec((ATT_WIDTH, D_MODEL), const, **resident),
                  pl.BlockSpec((D_MODEL, D_MODEL), const, **resident),
                  pl.BlockSpec((1, D_MODEL), const), pl.BlockSpec((1, D_MODEL), const)],
        out_specs=pl.BlockSpec((tm, D_MODEL), row),
        compiler_params=pltpu.CompilerParams(
            dimension_semantics=("parallel",), vmem_limit_bytes=VMEM_LIMIT),
        name="merge_ln1",
    )(oa, ob, gates, gates, x, wa, wb, wo, g1, b1)


FFN_TM = 1024
FFN_TF = 512
FFN_RS = 256


def _ffn_kernel(x_ref, w1_ref, w2_ref, g_ref, b_ref, o_ref, xb_ref):
    acc_ref = o_ref
    j = pl.program_id(1)
    last = pl.num_programs(1) - 1

    @pl.when(j == 0)
    def _():
        xb_ref[...] = x_ref[...].astype(BF16)

    blocks = [slice(r * FFN_RS, (r + 1) * FFN_RS) for r in range(FFN_TM // FFN_RS)]

    def partials():
        hs = [jnp.maximum(_dot(xb_ref[rows, :], w1_ref[...]), 0.0) for rows in blocks]
        return [_dot((h * h).astype(BF16), w2_ref[...]) for h in hs]

    @pl.when(j == 0)
    def _():
        for rows, part in zip(blocks, partials()):
            acc_ref[rows, :] = part

    @pl.when((j > 0) & (j < last))
    def _():
        for rows, part in zip(blocks, partials()):
            acc_ref[rows, :] += part

    @pl.when(j == last)
    def _():
        for rows, part in zip(blocks, partials()):
            y = ALPHA * x_ref[rows, :] + acc_ref[rows, :] + part
            o_ref[rows, :] = _layer_norm(y, g_ref[...], b_ref[...])


def _ffn(x1, w1, w2, g2, b2):
    tm, tf = FFN_TM, FFN_TF
    assert D_FF // tf >= 2
    s = x1.shape[0]
    return pl.pallas_call(
        _ffn_kernel,
        out_shape=jax.ShapeDtypeStruct((s, D_MODEL), F32),
        grid=(s // tm, D_FF // tf),
        in_specs=[pl.BlockSpec((tm, D_MODEL), lambda i, j: (i, 0)),
                  pl.BlockSpec((D_MODEL, tf), lambda i, j: (0, j)),
                  pl.BlockSpec((tf, D_MODEL), lambda i, j: (j, 0)),
                  pl.BlockSpec((1, D_MODEL), lambda i, j: (0, 0)),
                  pl.BlockSpec((1, D_MODEL), lambda i, j: (0, 0))],
        out_specs=pl.BlockSpec((tm, D_MODEL), lambda i, j: (i, 0)),
        scratch_shapes=[pltpu.VMEM((tm, D_MODEL), BF16)],
        compiler_params=pltpu.CompilerParams(
            dimension_semantics=("parallel", "arbitrary"), vmem_limit_bytes=VMEM_LIMIT),
        name="ffn_ln2",
    )(x1, w1, w2, g2, b2)


def _rope_tables(seq):
    half = ATT_DIM // 2
    inv = ROPE_THETA ** (-np.arange(half, dtype=np.float64) / half)
    ang = np.arange(seq, dtype=np.float64)[:, None] * inv[None, :]
    cos, sin = np.cos(ang), np.sin(ang)
    cos_t = np.concatenate([cos, cos, cos, cos], axis=1).astype(np.float32)
    sin_t = np.concatenate([-sin, sin, -sin, sin], axis=1).astype(np.float32)
    return jnp.asarray(cos_t), jnp.asarray(sin_t)


def kernel(x, w_in, hg_lb_logits, hg_norm_gain, attn_sinks, w_branch_a, w_branch_b, w_out,
           ln1_gain, ln1_bias, w_ff1, w_ff2, ln2_gain, ln2_bias):
    b, s, d = x.shape
    assert (b, s, d) == (1, SEQ, D_MODEL) and w_in.shape == (1, D_MODEL, D_IN)
    x2 = x.reshape(s, d)
    cos_t, sin_t = _rope_tables(s)
    oa, aq, akv, gates, (w1, w2) = _proj(
        x2, w_in[0].astype(BF16), hg_lb_logits, cos_t, sin_t, hg_norm_gain, (w_ff1[0], w_ff2[0]))
    sinks_b = jnp.broadcast_to(attn_sinks[0][:, None], (ATT_HEADS, LANES))
    ob, (wa, wb, wo) = _attn(aq, akv, sinks_b, (w_branch_a[0], w_branch_b[0], w_out[0]))
    x1 = _merge(oa, ob, gates, x2, wa, wb, wo, ln1_gain, ln1_bias)
    out = _ffn(x1, w1, w2, ln2_gain, ln2_bias)
    return out.reshape(b, s, d)
```

```python
import math

import jax
import jax.numpy as jnp
import numpy as np
from jax import lax
from jax.experimental import pallas as pl
from jax.experimental.pallas import tpu as pltpu

F32 = jnp.float32
BF16 = jnp.bfloat16

D_MODEL = 2048
SEQ = 8192
HG_HEADS = 8
HG_DIM = 128
HG_WIDTH = HG_HEADS * HG_DIM
ATT_HEADS = 16
ATT_KV_HEADS = 4
ATT_DIM = 64
ATT_GROUP = ATT_HEADS // ATT_KV_HEADS
ATT_WIDTH = ATT_HEADS * ATT_DIM
KV_WIDTH = ATT_KV_HEADS * ATT_DIM
WINDOW = 128
ROPE_THETA = 10000.0
D_FF = 4 * D_MODEL
D_IN = 4 * HG_WIDTH + ATT_WIDTH + 2 * KV_WIDTH + 2 * D_MODEL
ALPHA = 2.0 ** 0.25
LN_EPS = 1e-5
RMS_EPS = 1e-6
LOG2E = math.log2(math.e)

LANES = 128
HG_CHUNK = 128
HG_SUB = 8
NEG_BIG = -1e30

VMEM_LIMIT = 56 * 1024 * 1024
PROJ_VMEM_LIMIT = 60 * 1024 * 1024


def _dot(a, b):
    return jnp.dot(a, b, preferred_element_type=F32)


def _dot_nt(a, b):
    return lax.dot_general(a, b, (((1,), (1,)), ((), ())), preferred_element_type=F32)


def _hgrn_consts():
    C, SB = HG_CHUNK, HG_SUB
    ri = lax.broadcasted_iota(jnp.int32, (C, C), 0)
    ci = lax.broadcasted_iota(jnp.int32, (C, C), 1)
    tri = (ri >= ci).astype(BF16)
    row = lax.broadcasted_iota(jnp.int32, (C, 1), 0)
    sub_row = lax.broadcasted_iota(jnp.int32, (SB, 1), 0)
    levels = []
    half = C // 2
    while half >= SB:
        pair = ((ri // (2 * half)) == (ci // (2 * half))) & ((ri % (2 * half)) >= half) \
            & ((ci % (2 * half)) < half)
        levels.append((half, (row % (2 * half)) >= half, jnp.where(pair, 1.0, 0.0)))
        half //= 2
    return tri, sub_row, levels


class _HgrnChunk:
    WEIGHTS = dict(cumsum=8, diag=200, state=20, level=15, off_values=3, finish=40)
    N_LEVELS = (HG_CHUNK // HG_SUB).bit_length() - 1
    TOTAL_WEIGHT = HG_HEADS * (WEIGHTS['cumsum'] + WEIGHTS['diag'] + WEIGHTS['state']
                               + N_LEVELS * WEIGHTS['level'] + WEIGHTS['off_values']
                               + WEIGHTS['finish'])

    def __init__(self, consts, load, gain_ref, o_ref, r0, state_ref, c_scr, v_scr):
        self.tri, self.sub_row, self.levels = consts
        assert len(self.levels) == self.N_LEVELS
        self.load, self.gain_ref, self.o_ref, self.r0 = load, gain_ref, o_ref, r0
        self.state_ref, self.c_scr, self.v_scr = state_ref, c_scr, v_scr
        hs = range(HG_HEADS)
        self.qf = [load('q', h).astype(F32) for h in hs]
        self.kk = [load('k', h).astype(F32) for h in hs]
        self.vb = [load('v', h) for h in hs]
        self.b, self.o, self.r = [], [], []
        self.s_off = [None] * HG_HEADS

    def cumsum(self):
        for h in range(HG_HEADS):
            lf = self.load('lf', h)
            hi = lf.astype(BF16)
            lo = (lf - hi.astype(F32)).astype(BF16)
            both = _dot(self.tri, jnp.concatenate([hi, lo], axis=1))
            self.b.append((both[:, :HG_DIM] + both[:, HG_DIM:]) * LOG2E)
            yield self.WEIGHTS['cumsum']

    def state(self):
        C = HG_CHUNK
        for h in range(HG_HEADS):
            b, st = self.b[h], self.state_ref[h]
            vt = self.vb[h].T
            self.o.append(((self.qf[h] * jnp.exp2(b)).astype(BF16),
                           jnp.concatenate([st.astype(BF16), vt], axis=1)))
            b_last = b[C - 1:C, :]
            kd = self.kk[h] * jnp.exp2(b_last - b)
            self.state_ref[h] = st * jnp.exp2(b_last) + _dot(vt, kd.astype(BF16))
            yield self.WEIGHTS['state']

    def level(self, k):
        lvl, second_half, pair = self.levels[k]
        for h in range(HG_HEADS):
            b = self.b[h]
            ref = jnp.concatenate(
                [jnp.broadcast_to(b[base + lvl:base + lvl + 1, :], (2 * lvl, HG_DIM))
                 for base in range(0, HG_CHUNK, 2 * lvl)], axis=0)
            e = jnp.exp2(-jnp.abs(b - ref))
            qs = jnp.where(second_half, self.qf[h] * e, 0.0)
            ks = jnp.where(second_half, 0.0, self.kk[h] * e)
            s = _dot_nt(qs.astype(BF16), ks.astype(BF16)) * pair
            self.s_off[h] = s if self.s_off[h] is None else self.s_off[h] + s
            yield self.WEIGHTS['level']

    def off_values(self):
        for h in range(HG_HEADS):
            qe, state_and_vt = self.o[h]
            lhs = jnp.concatenate([qe, self.s_off[h].astype(BF16)], axis=1)
            self.o[h] = _dot_nt(lhs, state_and_vt)
            yield self.WEIGHTS['off_values']

    def diag(self):
        C, SB = HG_CHUNK, HG_SUB
        for h in range(HG_HEADS):
            b = self.b[h]
            self.c_scr[h] = b - jnp.log2(jnp.maximum(self.kk[h], 0.0))
            self.v_scr[h] = self.vb[h].astype(F32)
            scores = []
            for i in range(C // SB):
                base = i * SB
                bi = b[base:base + SB, :]
                qi = self.qf[h][base:base + SB, :]
                for j in range(SB):
                    cj = self.c_scr[h, base + j:base + j + 1, :]
                    p = qi * jnp.exp2(jnp.where(self.sub_row >= j, bi - cj, NEG_BIG))
                    scores.append(jnp.sum(p, axis=-1, keepdims=True))
            self.r.append(scores)
            yield self.WEIGHTS['diag']

    def finish(self):
        C, SB = HG_CHUNK, HG_SUB
        for h in range(HG_HEADS):
            sl = slice(h * HG_DIM, (h + 1) * HG_DIM)
            diag = []
            for i in range(C // SB):
                acc = None
                for j in range(SB):
                    term = self.r[h][i * SB + j] * self.v_scr[h, i * SB + j:i * SB + j + 1, :]
                    acc = term if acc is None else acc + term
                diag.append(acc)
            out = self.o[h] + jnp.concatenate(diag, axis=0)
            ms = jnp.mean(out * out, axis=-1, keepdims=True)
            out = out * lax.rsqrt(ms + RMS_EPS) * self.gain_ref[:, sl]
            out = out * self.load('g', h).astype(F32)
            self.o_ref[pl.ds(self.r0, C), sl] = out.astype(self.o_ref.dtype)
            yield self.WEIGHTS['finish']


PROJ_TM = 1024
PROJ_TN = 512
PROJ_RS = 128
PROJ_EDGES = tuple(c // PROJ_TN for c in (
    0, HG_WIDTH, 2 * HG_WIDTH, 3 * HG_WIDTH, 4 * HG_WIDTH, 4 * HG_WIDTH + ATT_WIDTH,
    4 * HG_WIDTH + ATT_WIDTH + 2 * KV_WIDTH, D_IN))
HGRN_SECTION = {'q': 0, 'k': 1, 'v': 2, 'g': 3}
HGRN_CHUNKS_PER_STEP = PROJ_TM // HG_CHUNK // (PROJ_EDGES[7] - PROJ_EDGES[6])
Q_SCALE = ATT_DIM ** -0.5 * LOG2E


def _rope(t, cos, sin_signed):
    w = t.shape[1]
    reps = w // cos.shape[1]
    cos_t = jnp.concatenate([cos] * reps, axis=1) if reps > 1 else cos
    sin_t = jnp.concatenate([sin_signed] * reps, axis=1) if reps > 1 else sin_signed
    lane = lax.broadcasted_iota(jnp.int32, t.shape, 1)
    first_half = (lane % ATT_DIM) < (ATT_DIM // 2)
    rot = jnp.where(first_half, pltpu.roll(t, w - ATT_DIM // 2, 1), pltpu.roll(t, ATT_DIM // 2, 1))
    return t * cos_t + rot * sin_t


def _dup_heads(t):
    lane = lax.broadcasted_iota(jnp.int32, (t.shape[0], LANES), 1)
    low = lane < ATT_DIM
    out = []
    for g in range(ATT_KV_HEADS):
        src = t[:, LANES * (g // 2):LANES * (g // 2 + 1)]
        swapped = pltpu.roll(src, ATT_DIM, 1)
        out.append(jnp.where(low, src, swapped) if g % 2 == 0 else jnp.where(low, swapped, src))
    return jnp.concatenate(out, axis=1)


def _proj_kernel(x_ref, w_ref, lbl_ref, cos_ref, sin_ref, gain_ref, *refs):
    n_side = len(PROJ_SIDE_CASTS)
    side_in = refs[:n_side]
    oa_ref, oaq_ref, okv_ref, og_ref = refs[n_side:n_side + 4]
    side_out = refs[n_side + 4:2 * n_side + 4]
    xb_ref, hp_ref, lf_ref, state_ref, c_scr, v_scr = refs[2 * n_side + 4:]
    i = pl.program_id(0)
    j = pl.program_id(1)
    e = PROJ_EDGES
    tiles_per_section = HG_WIDTH // PROJ_TN
    heads_per_tile = PROJ_TN // HG_DIM

    @pl.when(j == 0)
    def _():
        xb_ref[...] = x_ref[...].astype(BF16)

    @pl.when((i == 0) & (j == 0))
    def _():
        state_ref[...] = jnp.zeros_like(state_ref)

    def segment(s, interleave=None, rs=PROJ_RS):
        def deco(epilogue):
            @pl.when((j >= e[s]) & (j < e[s + 1]))
            def _():
                _side_cast(side_in, side_out)
                n_sub = PROJ_TM // rs
                todo = iter(range(n_sub))

                def sub_block():
                    r = next(todo)
                    rows = slice(r * rs, (r + 1) * rs)
                    epilogue(rows, _dot(xb_ref[rows, :], w_ref[...]))

                if interleave is None:
                    for _ in range(n_sub):
                        sub_block()
                else:
                    interleave(sub_block, n_sub)
                    assert next(todo, None) is None
        return deco

    @segment(0)
    def _(rows, acc):
        hp_ref[j, rows, :] = (acc * jax.nn.sigmoid(acc)).astype(BF16)

    @segment(1)
    def _(rows, acc):
        lbl = lbl_ref[...]
        lexp = jnp.exp(lbl - jnp.max(lbl, axis=0, keepdims=True))
        lb = lexp[0:1, :] / jnp.sum(lexp, axis=0, keepdims=True)
        f = lb + (1.0 - lb) * jax.nn.sigmoid(acc)
        lf_ref[j - e[1], rows, :] = jnp.log(f)
        hp_ref[j, rows, :] = (1.0 - f).astype(BF16)

    @segment(2)
    def _(rows, acc):
        hp_ref[j, rows, :] = acc.astype(BF16)

    @segment(3)
    def _(rows, acc):
        hp_ref[j, rows, :] = (acc * jax.nn.sigmoid(acc)).astype(BF16)

    @segment(4)
    def _(rows, acc):
        oaq_ref[rows, :] = (_rope(acc, cos_ref[rows, :], sin_ref[rows, :]) * Q_SCALE).astype(BF16)

    @segment(5)
    def _(rows, acc):
        k = _rope(acc[:, :KV_WIDTH], cos_ref[rows, :], sin_ref[rows, :])
        okv_ref[rows, :] = jnp.concatenate(
            [_dup_heads(k), _dup_heads(acc[:, KV_WIDTH:])], axis=1).astype(BF16)

    def hgrn_chunks(sub_block, n_sub):
        consts = _hgrn_consts()
        chunks = []
        for u in range(HGRN_CHUNKS_PER_STEP):
            r0 = pl.multiple_of(((j - e[6]) * HGRN_CHUNKS_PER_STEP + u) * HG_CHUNK, HG_CHUNK)

            def load(name, h, r0=r0):
                lanes = slice((h % heads_per_tile) * HG_DIM, (h % heads_per_tile + 1) * HG_DIM)
                if name == 'lf':
                    return lf_ref[h // heads_per_tile, pl.ds(r0, HG_CHUNK), lanes]
                tile = HGRN_SECTION[name] * tiles_per_section + h // heads_per_tile
                return hp_ref[tile, pl.ds(r0, HG_CHUNK), lanes]

            chunks.append(_HgrnChunk(consts, load, gain_ref, oa_ref, r0, state_ref,
                                     c_scr.at[u], v_scr.at[u]))

        def stages():
            for c in chunks:
                yield from c.cumsum()
            for c in chunks:
                yield from c.diag()
            for c in chunks:
                yield from c.state()
                for k in range(_HgrnChunk.N_LEVELS):
                    yield from c.level(k)
                yield from c.off_values()
                yield from c.finish()

        todo, budget = stages(), 0.0
        for _ in range(n_sub):
            budget += len(chunks) * _HgrnChunk.TOTAL_WEIGHT / n_sub
            while budget > 0:
                weight = next(todo, None)
                if weight is None:
                    break
                budget -= weight
            sub_block()
        for _ in todo:
            pass

    @segment(6, interleave=hgrn_chunks, rs=PROJ_RS * 2)
    def _(rows, acc):
        og_ref[rows, :] = acc.astype(BF16)


PROJ_SIDE_CASTS = (128, 128)


def _proj(x, wb, lb_logits, cos, sin_signed, gain, side_weights):
    tm, tn = PROJ_TM, PROJ_TN
    m, k = x.shape
    e = PROJ_EDGES
    nj = D_IN // tn
    assert e[6] - e[5] == 1 and 2 * KV_WIDTH == tn
    assert (tm // HG_CHUNK) % (e[7] - e[6]) == 0

    def span_spec(s, width=tn):
        lo, hi = e[s], e[s + 1]
        return pl.BlockSpec((tm, width), lambda i, j: (i, jnp.clip(j - lo, 0, hi - lo - 1)))

    def sds(width, dtype):
        return jax.ShapeDtypeStruct((m, width), dtype)

    side_specs, side_shapes = [], []
    for w, steps in zip(side_weights, PROJ_SIDE_CASTS):
        slab = w.shape[0] // steps
        assert slab * steps == w.shape[0] and slab % 16 == 0 and steps <= (m // tm) * nj
        side_specs.append(pl.BlockSpec(
            (slab, w.shape[1]), lambda i, j, steps=steps: (jnp.minimum(i * nj + j, steps - 1), 0)))
        side_shapes.append(jax.ShapeDtypeStruct(w.shape, BF16))

    row_tile = lambda i, j: (i, 0)
    outs = pl.pallas_call(
        _proj_kernel,
        out_shape=(sds(HG_WIDTH, BF16), sds(ATT_WIDTH, BF16), sds(4 * KV_WIDTH, BF16),
                   sds(2 * D_MODEL, BF16), *side_shapes),
        grid=(m // tm, nj),
        in_specs=[pl.BlockSpec((tm, k), row_tile),
                  pl.BlockSpec((k, tn), lambda i, j: (0, j)),
                  pl.BlockSpec((lb_logits.shape[0], tn),
                               lambda i, j: (0, jnp.clip(j - e[1], 0, e[2] - e[1] - 1))),
                  pl.BlockSpec((tm, LANES), row_tile),
                  pl.BlockSpec((tm, LANES), row_tile),
                  pl.BlockSpec((1, HG_WIDTH), lambda i, j: (0, 0)), *side_specs],
        out_specs=(pl.BlockSpec((tm, HG_WIDTH), row_tile), span_spec(4),
                   pl.BlockSpec((tm, 4 * KV_WIDTH), row_tile), span_spec(6), *side_specs),
        scratch_shapes=[pltpu.VMEM((tm, k), BF16),
                        pltpu.VMEM((4 * HG_WIDTH // tn, tm, tn), BF16),
                        pltpu.VMEM((HG_WIDTH // tn, tm, tn), F32),
                        pltpu.VMEM((HG_HEADS, HG_DIM, HG_DIM), F32),
                        pltpu.VMEM((HGRN_CHUNKS_PER_STEP, HG_HEADS, HG_CHUNK, HG_DIM), F32),
                        pltpu.VMEM((HGRN_CHUNKS_PER_STEP, HG_HEADS, HG_CHUNK, HG_DIM), F32)],
        compiler_params=pltpu.CompilerParams(
            dimension_semantics=("arbitrary", "arbitrary"), vmem_limit_bytes=PROJ_VMEM_LIMIT),
        name="proj_hgrn2",
    )(x, wb, lb_logits, cos, sin_signed, gain, *side_weights)
    return outs[0], outs[1], outs[2], outs[3], outs[4:]


def _attn_kernel(q_ref, kvp_ref, kvc_ref, sink_ref, *refs):
    n_side = (len(refs) - 1) // 2
    side_in, o_ref, side_out = refs[:n_side], refs[n_side], refs[n_side + 1:]
    _side_cast(side_in, side_out)
    n = pl.program_id(0)
    blk = WINDOW
    nblk = q_ref.shape[0] // blk
    qi = lax.broadcasted_iota(jnp.int32, (blk, 2 * blk), 0)
    ci = lax.broadcasted_iota(jnp.int32, (blk, 2 * blk), 1)
    rel = qi + blk - ci
    band = (rel >= 0) & (rel < blk)
    lane = lax.broadcasted_iota(jnp.int32, (blk, LANES), 1)
    low = lane < ATT_DIM
    zero = jnp.zeros((), BF16)

    def keys_values(t, lanes):
        own = kvc_ref[t * blk:(t + 1) * blk, lanes]
        prev = kvp_ref[:, lanes] if t == 0 else kvc_ref[(t - 1) * blk:t * blk, lanes]
        return jnp.concatenate([prev, own], axis=0)

    units = [(t, g) for t in range(nblk) for g in range(ATT_KV_HEADS)]
    pairs = range(ATT_GROUP // 2)
    s = []
    for t, g in units:
        first_key = jnp.where(n > 0, 0, blk) if t == 0 else 0
        bias1 = jnp.where(band & (ci >= first_key), 0.0, NEG_BIG)
        bias = jnp.concatenate([bias1] * ATT_GROUP, axis=0)
        kd = keys_values(t, slice(g * LANES, (g + 1) * LANES))
        qs = []
        for pr in pairs:
            col = (g * len(pairs) + pr) * LANES
            qp = q_ref[t * blk:(t + 1) * blk, col:col + LANES]
            qs += [jnp.where(low, qp, zero), jnp.where(low, zero, qp)]
        s.append(_dot_nt(jnp.concatenate(qs, axis=0), kd) + bias)

    p, denom = [], []
    for u, (t, g) in enumerate(units):
        sink = jnp.concatenate(
            [jnp.broadcast_to(sink_ref[g * ATT_GROUP + h:g * ATT_GROUP + h + 1, :], (blk, LANES))
             for h in range(ATT_GROUP)], axis=0) * LOG2E
        m = jnp.max(jnp.maximum(s[u][:, :blk], s[u][:, blk:]), axis=-1, keepdims=True)
        m = jnp.maximum(jnp.broadcast_to(m, (ATT_GROUP * blk, LANES)), sink)
        w = jnp.exp2(s[u] - jnp.concatenate([m, m], axis=1))
        total = jnp.sum(w[:, :blk] + w[:, blk:], axis=-1, keepdims=True)
        denom.append(jnp.broadcast_to(total, (ATT_GROUP * blk, LANES)) + jnp.exp2(sink - m))
        p.append(w.astype(BF16))

    o = []
    for u, (t, g) in enumerate(units):
        vd = keys_values(t, slice(2 * KV_WIDTH + g * LANES, 2 * KV_WIDTH + (g + 1) * LANES))
        o.append(_dot(p[u], vd) / denom[u])

    for u, (t, g) in enumerate(units):
        for pr in pairs:
            out = jnp.where(low, o[u][2 * pr * blk:(2 * pr + 1) * blk],
                            o[u][(2 * pr + 1) * blk:(2 * pr + 2) * blk])
            col = (g * len(pairs) + pr) * LANES
            o_ref[t * blk:(t + 1) * blk, col:col + LANES] = out.astype(o_ref.dtype)


def _side_cast_specs(side_weights, steps):
    specs, shapes = [], []
    for w in side_weights:
        slab = w.shape[0] // steps
        assert slab * steps == w.shape[0] and slab % 16 == 0
        specs.append(pl.BlockSpec((slab, w.shape[1]), lambda n: (n, 0)))
        shapes.append(jax.ShapeDtypeStruct(w.shape, BF16))
    return specs, shapes


def _side_cast(side_in, side_out):
    for src, dst in zip(side_in, side_out):
        dst[...] = src[...].astype(BF16)


ATT_BLOCKS_PER_STEP = 4


def _attn(aq, akv, sinks_b, side_weights):
    blk, nblk = WINDOW, ATT_BLOCKS_PER_STEP
    s = aq.shape[0]
    steps = s // (blk * nblk)
    side_specs, side_shapes = _side_cast_specs(side_weights, steps)
    outs = pl.pallas_call(
        _attn_kernel,
        out_shape=(jax.ShapeDtypeStruct((s, ATT_WIDTH), BF16), *side_shapes),
        grid=(steps,),
        in_specs=[pl.BlockSpec((nblk * blk, ATT_WIDTH), lambda n: (n, 0)),
                  pl.BlockSpec((blk, 4 * KV_WIDTH), lambda n: (jnp.maximum(nblk * n - 1, 0), 0)),
                  pl.BlockSpec((nblk * blk, 4 * KV_WIDTH), lambda n: (n, 0)),
                  pl.BlockSpec((ATT_HEADS, LANES), lambda n: (0, 0)), *side_specs],
        out_specs=(pl.BlockSpec((nblk * blk, ATT_WIDTH), lambda n: (n, 0)), *side_specs),
        compiler_params=pltpu.CompilerParams(
            dimension_semantics=("parallel",), vmem_limit_bytes=VMEM_LIMIT),
        name="swa",
    )(aq, akv, akv, sinks_b, *side_weights)
    return outs[0], outs[1:]


def _layer_norm(r, gain, bias):
    mu = jnp.mean(r, axis=-1, keepdims=True)
    d = r - mu
    var = jnp.mean(d * d, axis=-1, keepdims=True)
    return d * lax.rsqrt(var + LN_EPS) * gain + bias


MERGE_TM = 512
MERGE_RS = 128


def _merge_kernel(oa_ref, ob_ref, ga_ref, gb_ref, x_ref, wa_ref, wb_ref, wo_ref, g_ref, b_ref,
                  x1_ref):
    for r in range(MERGE_TM // MERGE_RS):
        rows = slice(r * MERGE_RS, (r + 1) * MERGE_RS)
        ya = _dot(oa_ref[rows, :], wa_ref[...])
        yb = _dot(ob_ref[rows, :], wb_ref[...])
        mixed = (jax.nn.sigmoid(ga_ref[rows, :].astype(F32)) * ya
                 + jax.nn.sigmoid(gb_ref[rows, :].astype(F32)) * yb)
        z = _dot(mixed.astype(BF16), wo_ref[...])
        x1_ref[rows, :] = _layer_norm(ALPHA * x_ref[rows, :] + z, g_ref[...], b_ref[...])


def _merge(oa, ob, gates, x, wa, wb, wo, g1, b1):
    tm = MERGE_TM
    s = x.shape[0]
    row = lambda i: (i, 0)
    const = lambda i: (0, 0)
    resident = dict(pipeline_mode=pl.Buffered(1))
    return pl.pallas_call(
        _merge_kernel,
        out_shape=jax.ShapeDtypeStruct((s, D_MODEL), F32),
        grid=(s // tm,),
        in_specs=[pl.BlockSpec((tm, HG_WIDTH), row), pl.BlockSpec((tm, ATT_WIDTH), row),
                  pl.BlockSpec((tm, D_MODEL), row), pl.BlockSpec((tm, D_MODEL), lambda i: (i, 1)),
                  pl.BlockSpec((tm, D_MODEL), row),
                  pl.BlockSpec((HG_WIDTH, D_MODEL), const, **resident),
                  pl.BlockSpec((ATT_WIDTH, D_MODEL), const, **resident),
                  pl.BlockSpec((D_MODEL, D_MODEL), const, **resident),
                  pl.BlockSpec((1, D_MODEL), const), pl.BlockSpec((1, D_MODEL), const)],
        out_specs=pl.BlockSpec((tm, D_MODEL), row),
        compiler_params=pltpu.CompilerParams(
            dimension_semantics=("parallel",), vmem_limit_bytes=VMEM_LIMIT),
        name="merge_ln1",
    )(oa, ob, gates, gates, x, wa, wb, wo, g1, b1)


FFN_TM = 1024
FFN_TF = 512
FFN_RS = 256


def _ffn_kernel(x_ref, w1_ref, w2_ref, g_ref, b_ref, o_ref, xb_ref):
    acc_ref = o_ref
    j = pl.program_id(1)
    last = pl.num_programs(1) - 1

    @pl.when(j == 0)
    def _():
        xb_ref[...] = x_ref[...].astype(BF16)

    blocks = [slice(r * FFN_RS, (r + 1) * FFN_RS) for r in range(FFN_TM // FFN_RS)]

    def partials():
        hs = [jnp.maximum(_dot(xb_ref[rows, :], w1_ref[...]), 0.0) for rows in blocks]
        return [_dot((h * h).astype(BF16), w2_ref[...]) for h in hs]

    @pl.when(j == 0)
    def _():
        for rows, part in zip(blocks, partials()):
            acc_ref[rows, :] = part

    @pl.when((j > 0) & (j < last))
    def _():
        for rows, part in zip(blocks, partials()):
            acc_ref[rows, :] += part

    @pl.when(j == last)
    def _():
        for rows, part in zip(blocks, partials()):
            y = ALPHA * x_ref[rows, :] + acc_ref[rows, :] + part
            o_ref[rows, :] = _layer_norm(y, g_ref[...], b_ref[...])


def _ffn(x1, w1, w2, g2, b2):
    tm, tf = FFN_TM, FFN_TF
    assert D_FF // tf >= 2
    s = x1.shape[0]
    return pl.pallas_call(
        _ffn_kernel,
        out_shape=jax.ShapeDtypeStruct((s, D_MODEL), F32),
        grid=(s // tm, D_FF // tf),
        in_specs=[pl.BlockSpec((tm, D_MODEL), lambda i, j: (i, 0)),
                  pl.BlockSpec((D_MODEL, tf), lambda i, j: (0, j)),
                  pl.BlockSpec((tf, D_MODEL), lambda i, j: (j, 0)),
                  pl.BlockSpec((1, D_MODEL), lambda i, j: (0, 0)),
                  pl.BlockSpec((1, D_MODEL), lambda i, j: (0, 0))],
        out_specs=pl.BlockSpec((tm, D_MODEL), lambda i, j: (i, 0)),
        scratch_shapes=[pltpu.VMEM((tm, D_MODEL), BF16)],
        compiler_params=pltpu.CompilerParams(
            dimension_semantics=("parallel", "arbitrary"), vmem_limit_bytes=VMEM_LIMIT),
        name="ffn_ln2",
    )(x1, w1, w2, g2, b2)


def _rope_tables(seq):
    half = ATT_DIM // 2
    inv = ROPE_THETA ** (-np.arange(half, dtype=np.float64) / half)
    ang = np.arange(seq, dtype=np.float64)[:, None] * inv[None, :]
    cos, sin = np.cos(ang), np.sin(ang)
    cos_t = np.concatenate([cos, cos, cos, cos], axis=1).astype(np.float32)
    sin_t = np.concatenate([-sin, sin, -sin, sin], axis=1).astype(np.float32)
    return jnp.asarray(cos_t), jnp.asarray(sin_t)


def kernel(x, w_in, hg_lb_logits, hg_norm_gain, attn_sinks, w_branch_a, w_branch_b, w_out,
           ln1_gain, ln1_bias, w_ff1, w_ff2, ln2_gain, ln2_bias):
    b, s, d = x.shape
    assert (b, s, d) == (1, SEQ, D_MODEL) and w_in.shape == (1, D_MODEL, D_IN)
    x2 = x.reshape(s, d)
    cos_t, sin_t = _rope_tables(s)
    oa, aq, akv, gates, (w1, w2) = _proj(
        x2, w_in[0].astype(BF16), hg_lb_logits, cos_t, sin_t, hg_norm_gain, (w_ff1[0], w_ff2[0]))
    sinks_b = jnp.broadcast_to(attn_sinks[0][:, None], (ATT_HEADS, LANES))
    ob, (wa, wb, wo) = _attn(aq, akv, sinks_b, (w_branch_a[0], w_branch_b[0], w_out[0]))
    x1 = _merge(oa, ob, gates, x2, wa, wb, wo, ln1_gain, ln1_bias)
    out = _ffn(x1, w1, w2, ln2_gain, ln2_bias)
    return out.reshape(b, s, d)
```
